```python
import math
import jax, jax.numpy as jnp
from jax import lax
import numpy as np

D_MODEL = 1024
BATCH = 16
SEQ = 256
DEPTH = 2
DEC_BATCH = 4
DEC_SEQ = 1024
PAST_LEN = 512

GRID_W = 64
Q_BLOCK = 128
ROPE_THETA = 10000.0
EPS = 1e-6
RWKV_GN_EPS = 64e-5
HA = 4
DH_A = 64
DV_A = 2 * DH_A
HB = 8
HKV_B = 2
DH_B = 64
HC = 4
DK_C = 128
DV_C = 128
HGRN_CHUNK = 32
HD = 8
DH_D = 64
W_LORA = 64
A_LORA = 64
G_LORA = 128
D_FF = 2816

N_EVEN = (DEPTH + 1) // 2
N_ODD = DEPTH // 2
D_A = HA * DV_A
D_B = HB * DH_B
D_C = HC * DV_C
D_D = HD * DH_D
EVEN_SIZES = (HA * 2 * DH_A, HA * 2 * DH_A, HA * DV_A, HB * DH_B, HKV_B * DH_B, HKV_B * DH_B)
EVEN_PROJ = sum(EVEN_SIZES)
HGRN_SIZES = (HC * DK_C, HC * DK_C, HC * DK_C, HC * DV_C, D_C)
HGRN_PROJ = sum(HGRN_SIZES)
RWKV_SIZES = (D_D, D_D, D_D, 2 * W_LORA, A_LORA, G_LORA)
RWKV_PROJ = sum(RWKV_SIZES)
ODD_PROJ = HGRN_PROJ + RWKV_PROJ
F32 = jnp.float32

kernel_name = 'hybrid_diffusion_prefix_trunk_step'


def split_last(x, sizes):
    outs, start = [], 0
    for n in sizes:
        outs.append(x[..., start:start + n])
        start += n
    return outs


def rms_norm(x, g):
    xf = x.astype(F32)
    y = xf * lax.rsqrt(jnp.mean(xf * xf, axis=-1, keepdims=True) + EPS)
    return (y * g.astype(F32)).astype(x.dtype)


def centred_conv3(x, w, b):
    xp = jnp.pad(x, ((0, 0), (1, 1), (0, 0)))
    return w[0] * xp[:, :-2] + w[1] * xp[:, 1:-1] + w[2] * xp[:, 2:] + b


def token_shift_mix(x, mu):
    xp = jnp.pad(x, ((0, 0), (1, 1), (0, 0)))
    return x + mu * (0.5 * (xp[:, :-2] + xp[:, 2:]) - x)


def grid_positions(rows):
    r = jnp.broadcast_to(jnp.arange(rows, dtype=jnp.int32)[:, None], (rows, GRID_W)).reshape(-1)
    c = jnp.broadcast_to(jnp.arange(GRID_W, dtype=jnp.int32)[None, :], (rows, GRID_W)).reshape(-1)
    return r, c


def rope_axis(x, pos):
    half = x.shape[-1] // 2
    freq = jnp.power(ROPE_THETA, -jnp.arange(half, dtype=F32) / half)
    ang = pos.astype(F32)[:, None] * freq[None, :]
    cos, sin = jnp.cos(ang)[:, None, :], jnp.sin(ang)[:, None, :]
    xf = x.astype(F32)
    x1, x2 = xf[..., :half], xf[..., half:]
    return jnp.concatenate([x1 * cos - x2 * sin, x1 * sin + x2 * cos], axis=-1).astype(x.dtype)


def rope_2d(x, rows, cols):
    d = x.shape[-1]
    return jnp.concatenate([rope_axis(x[..., :d // 2], rows), rope_axis(x[..., d // 2:], cols)], axis=-1)


def block_attend(q, k, v):
    b, g, r, s, d = q.shape
    nb = s // Q_BLOCK
    scale = d ** -0.5
    qb = jnp.moveaxis(q.reshape(b, g, r, nb, Q_BLOCK, d), 3, 0)

    def one_block(qi):
        sc = jnp.einsum('bgrqd,bgtd->bgrqt', qi, k).astype(F32) * scale
        p = jax.nn.softmax(sc, axis=-1).astype(v.dtype)
        return jnp.einsum('bgrqt,bgte->bgrqe', p, v)

    out = lax.map(one_block, qb)
    return jnp.moveaxis(out, 0, 3).reshape(b, g, r, s, v.shape[-1])


def even_mixer(h, P, i, lidx, pos, ctx):
    bsz, s, _ = h.shape
    qa, ka, va, qb, kb, vb = split_last(h @ P['ev_w_in'][i], EVEN_SIZES)
    qa = qa.reshape(bsz, s, 2 * HA, DH_A)
    ka = ka.reshape(bsz, s, 2 * HA, DH_A)
    qb = rms_norm(qb.reshape(bsz, s, HB, DH_B), P['b_q_norm_g'][i])
    kb = rms_norm(kb.reshape(bsz, s, HKV_B, DH_B), P['b_k_norm_g'][i])
    if pos is not None:
        qa, ka, qb, kb = [rope_2d(t, pos[0], pos[1]) for t in (qa, ka, qb, kb)]

    def heads_first(t):
        return jnp.transpose(t, (0, 2, 1, 3))

    own = (heads_first(ka.reshape(bsz, s, HA, 2 * DH_A)),
           heads_first(va.reshape(bsz, s, HA, DV_A)),
           heads_first(kb),
           heads_first(vb.reshape(bsz, s, HKV_B, DH_B)))
    if ctx is None:
        k_a, v_a, k_b, v_b = own
    else:
        k_a, v_a, k_b, v_b = [jnp.concatenate([cx, ow], axis=2) for cx, ow in zip(ctx, own)]

    qa = heads_first(qa.reshape(bsz, s, HA, 2 * DH_A))
    a1 = block_attend(qa[:, :, None, :, :DH_A], k_a[..., :DH_A], v_a)[:, :, 0]
    a2 = block_attend(qa[:, :, None, :, DH_A:], k_a[..., DH_A:], v_a)[:, :, 0]
    lam_init = 0.8 - 0.6 * math.exp(-0.3 * lidx)
    lq1, lk1, lq2, lk2 = P['a_lambda'][i].astype(F32)
    lam = jnp.exp(jnp.sum(lq1 * lk1)) - jnp.exp(jnp.sum(lq2 * lk2)) + lam_init
    oa = rms_norm(a1.astype(F32) - lam * a2.astype(F32), P['a_subln_g'][i]) * (1.0 - lam_init)
    oa = jnp.transpose(oa, (0, 2, 1, 3)).reshape(bsz, s, D_A).astype(h.dtype)

    qg = heads_first(qb).reshape(bsz, HKV_B, HB // HKV_B, s, DH_B)
    ob = block_attend(qg, k_b, v_b).reshape(bsz, HB, s, DH_B)
    ob = jnp.transpose(ob, (0, 2, 1, 3)).reshape(bsz, s, D_B)
    return jnp.concatenate([oa, ob], axis=-1) @ P['ev_w_out'][i], own


def hgrn_chunk_scan(q, k, logf, v, s0):
    b, s, h, dk = q.shape
    dv = v.shape[-1]
    nc, L = s // HGRN_CHUNK, HGRN_CHUNK

    def chunks(t):
        return jnp.moveaxis(t.reshape(b, nc, L, h, t.shape[-1]), 1, 0)

    causal = jnp.tril(jnp.ones((L, L), dtype=bool))[None, :, :, None, None]

    def step(state, inp):
        qc, kc, lfc, vc = inp
        cum = jnp.cumsum(lfc, axis=1)
        o_inter = jnp.einsum('bthk,bhkv->bthv', qc * jnp.exp(cum), state)
        diff = jnp.where(causal, cum[:, :, None] - cum[:, None, :], -jnp.inf)
        scores = jnp.einsum('bthk,bshk,btshk->bths', qc, kc, jnp.exp(diff))
        o_intra = jnp.einsum('bths,bshv->bthv', scores, vc)
        dec_end = jnp.exp(cum[:, -1:] - cum)
        state = state * jnp.exp(cum[:, -1])[..., None] + jnp.einsum('bshk,bshv->bhkv', kc * dec_end, vc)
        return state, o_inter + o_intra

    s_fin, o = lax.scan(step, s0, (chunks(q), chunks(k), chunks(logf), chunks(v)))
    return jnp.moveaxis(o, 0, 1).reshape(b, s, h, dv), s_fin


def hgrn_direction(q, f_logit, lb, v, s0, reverse):
    if reverse:
        q, f_logit, v = [jnp.flip(t, axis=1) for t in (q, f_logit, v)]
    f = lb + (1.0 - lb) * jax.nn.sigmoid(f_logit.astype(F32))
    o, s_fin = hgrn_chunk_scan(q.astype(F32), 1.0 - f, jnp.log(f), v.astype(F32), s0)
    if reverse:
        o = jnp.flip(o, axis=1)
    return o, s_fin


def rwkv_direction(r, w, kk, a, v, kt, s0, reverse):
    def step(state, inp):
        r_t, w_t, kk_t, a_t, v_t, k_t = inp
        sa = jnp.einsum('bhvk,bhk->bhv', state, -kk_t)
        state = (state * w_t[:, :, None, :] + sa[..., None] * (kk_t * a_t)[:, :, None, :]
                 + v_t[..., None] * k_t[:, :, None, :])
        return state, jnp.einsum('bhvk,bhk->bhv', state, r_t)

    xs = tuple(jnp.moveaxis(t.astype(F32), 1, 0) for t in (r, w, kk, a, v, kt))
    s_fin, o = lax.scan(step, s0, xs, reverse=reverse)
    return jnp.moveaxis(o, 0, 1), s_fin


def odd_mixer(h, P, i, lidx, ctx):
    bsz, s, _ = h.shape
    proj = h @ P['od_w_in'][i]
    if ctx is None:
        s_hgrn0 = jnp.zeros((bsz, 2, HC, DK_C, DV_C), F32)
        s_rwkv0 = jnp.zeros((bsz, 2, HD, DH_D, DH_D), F32)
    else:
        s_hgrn0, s_rwkv0 = ctx[0].astype(F32), ctx[1].astype(F32)

    def heads(t, nh):
        return t.reshape(bsz, s, nh, -1)

    q, f_fw, f_bw, v_c, g_c = split_last(proj[..., :HGRN_PROJ], HGRN_SIZES)
    lb_sm = jax.nn.softmax(P['hgrn_lb_logits'].astype(F32), axis=1)
    lb = (jnp.cumsum(lb_sm, axis=1)[:, lidx] - lb_sm[:, 0]).reshape(2, HC, DK_C)
    qh, vh = heads(jax.nn.silu(q), HC), heads(v_c, HC)
    oc_f, sc_f = hgrn_direction(qh, heads(f_fw, HC), lb[0], vh, s_hgrn0[:, 0], False)
    oc_b, sc_b = hgrn_direction(qh, heads(f_bw, HC), lb[1], vh, s_hgrn0[:, 1], True)
    o_c = rms_norm(oc_f + oc_b, P['hgrn_norm_g'][i]).reshape(bsz, s, D_C) * jax.nn.silu(g_c.astype(F32))

    pd = token_shift_mix(proj[..., HGRN_PROJ:], P['rwkv_mu'][i])
    r, k, v, wd, ad, gd = [t.astype(F32) for t in split_last(pd, RWKV_SIZES)]
    a = jax.nn.sigmoid(P['rwkv_a0'][i] + ad @ P['rwkv_a_up'][i])
    g = jax.nn.sigmoid(gd) @ P['rwkv_g_up'][i]
    kk = heads(k * P['rwkv_k_k'][i], HD)
    kk = kk / jnp.maximum(jnp.sqrt(jnp.sum(kk * kk, axis=-1, keepdims=True)), 1e-12)
    kt = heads(k * (1.0 + (a - 1.0) * P['rwkv_k_a'][i]), HD)
    rh, vh_d, ah = heads(r, HD), heads(v, HD), heads(a, HD)
    o_dirs, s_dirs = [], []
    for dr in range(2):
        z = P['rwkv_w0'][i, dr] + jnp.tanh(wd[..., dr * W_LORA:(dr + 1) * W_LORA]) @ P['rwkv_w_up'][i, dr]
        w = jnp.exp(-jnp.exp(-jax.nn.softplus(-z) - 0.5))
        o_dr, s_dr = rwkv_direction(rh, heads(w, HD), kk, ah, vh_d, kt, s_rwkv0[:, dr], dr == 1)
        o_dirs.append(o_dr)
        s_dirs.append(s_dr)
    o = o_dirs[0] + o_dirs[1]
    mu = jnp.mean(o, axis=-1, keepdims=True)
    var = jnp.mean(jnp.square(o - mu), axis=-1, keepdims=True)
    o = ((o - mu) * lax.rsqrt(var + RWKV_GN_EPS) * P['rwkv_ln_g'][i].reshape(HD, DH_D)
         + P['rwkv_ln_b'][i].reshape(HD, DH_D))
    o = o + jnp.sum(rh * kt * P['rwkv_r_k'][i].reshape(HD, DH_D), axis=-1, keepdims=True) * vh_d
    o_d = o.reshape(bsz, s, D_D) * g
    out = jnp.concatenate([o_c, o_d], axis=-1).astype(h.dtype) @ P['od_w_out'][i]
    return out, (jnp.stack([sc_f, sc_b], axis=1), jnp.stack(s_dirs, axis=1))


def conv_ffn(h, P, l):
    u = centred_conv3(h @ P['ffn_w_up'][l], P['ffn_conv_w'][l], P['ffn_conv_b'][l])
    val, gate = jnp.split(u, 2, axis=-1)
    return (jax.nn.silu(gate) * val) @ P['ffn_w_down'][l]


def run_trunk(x, cond, pos, ctx_in, P):
    collect = ([], [], [], [], [], [])
    for l in range(DEPTH):
        i = l // 2
        mod = jax.nn.silu(cond) @ P['ada_w'][l] + P['ada_b'][l]
        sh1, sc1, g1, sh2, sc2, g2 = jnp.split(mod[:, None, :], 6, axis=-1)
        h = rms_norm(x, P['norm_mix_g'][l]) * (1.0 + sc1) + sh1
        if l % 2 == 0:
            ctx = None if ctx_in is None else tuple(t[:, i] for t in ctx_in[:4])
            mix, own = even_mixer(h, P, i, l, pos, ctx)
            if ctx_in is None:
                for j in range(4):
                    collect[j].append(own[j])
        else:
            ctx = None if ctx_in is None else tuple(t[:, i] for t in ctx_in[4:])
            mix, fin = odd_mixer(h, P, i, l, ctx)
            if ctx_in is None:
                collect[4].append(fin[0].astype(x.dtype))
                collect[5].append(fin[1].astype(x.dtype))
        x = x + g1 * mix
        h = rms_norm(x, P['norm_ffn_g'][l]) * (1.0 + sc2) + sh2
        x = x + g2 * conv_ffn(h, P, l)
    y = rms_norm(x, P['final_norm_g'])
    new_ctx = None if ctx_in is not None else tuple(jnp.stack(cl, axis=1) for cl in collect)
    return y, new_ctx


def setup_inputs(seed: int = 0) -> dict:
    key = jax.random.key(seed)
    ks = iter(jax.random.split(key, 64))

    def nrm(shape, scale):
        return scale * jax.random.normal(next(ks), shape, F32)

    def gain(shape):
        return 1.0 + 0.05 * jax.random.normal(next(ks), shape, F32)

    d = D_MODEL
    return {
        'x_prompt': nrm((BATCH, SEQ, d), 1.0),
        'x_sample': nrm((DEC_BATCH, DEC_SEQ, d), 1.0),
        'cache_a_k': nrm((DEC_BATCH, N_EVEN, HA, PAST_LEN, 2 * DH_A), 1.0),
        'cache_a_v': nrm((DEC_BATCH, N_EVEN, HA, PAST_LEN, DV_A), 1.0),
        'cache_b_k': nrm((DEC_BATCH, N_EVEN, HKV_B, PAST_LEN, DH_B), 1.0),
        'cache_b_v': nrm((DEC_BATCH, N_EVEN, HKV_B, PAST_LEN, DH_B), 1.0),
        'state_hgrn': nrm((DEC_BATCH, N_ODD, 2, HC, DK_C, DV_C), 0.5),
        'state_rwkv': nrm((DEC_BATCH, N_ODD, 2, HD, DH_D, DH_D), 0.3),
        'c': nrm((DEC_BATCH, d), 1.0),
        'c_ctx': nrm((d,), 1.0),
        'ada_w': nrm((DEPTH, d, 6 * d), 0.3 * d ** -0.5),
        'ada_b': nrm((DEPTH, 6 * d), 0.02),
        'norm_mix_g': gain((DEPTH, d)),
        'norm_ffn_g': gain((DEPTH, d)),
        'final_norm_g': gain((d,)),
        'ev_w_in': nrm((N_EVEN, d, EVEN_PROJ), d ** -0.5),
        'ev_w_out': nrm((N_EVEN, D_A + D_B, d), (D_A + D_B) ** -0.5),
        'a_lambda': nrm((N_EVEN, 4, DH_A), 0.1),
        'a_subln_g': gain((N_EVEN, DV_A)),
        'b_q_norm_g': gain((N_EVEN, DH_B)),
        'b_k_norm_g': gain((N_EVEN, DH_B)),
        'od_w_in': nrm((N_ODD, d, ODD_PROJ), d ** -0.5),
        'od_w_out': nrm((N_ODD, D_C + D_D, d), (D_C + D_D) ** -0.5),
        'hgrn_lb_logits': nrm((2, DEPTH, HC * DK_C), 0.5),
        'hgrn_norm_g': gain((N_ODD, DV_C)),
        'rwkv_mu': jax.random.uniform(next(ks), (N_ODD, RWKV_PROJ), F32),
        'rwkv_w0': nrm((N_ODD, 2, D_D), 0.5),
        'rwkv_w_up': nrm((N_ODD, 2, W_LORA, D_D), 0.1),
        'rwkv_a0': nrm((N_ODD, D_D), 0.3),
        'rwkv_a_up': nrm((N_ODD, A_LORA, D_D), 0.1),
        'rwkv_g_up': nrm((N_ODD, G_LORA, D_D), G_LORA ** -0.5),
        'rwkv_k_k': 0.85 + nrm((N_ODD, D_D), 0.1),
        'rwkv_k_a': gain((N_ODD, D_D)),
        'rwkv_r_k': nrm((N_ODD, D_D), 0.1),
        'rwkv_ln_g': gain((N_ODD, D_D)),
        'rwkv_ln_b': nrm((N_ODD, D_D), 0.02),
        'ffn_w_up': nrm((DEPTH, d, 2 * D_FF), d ** -0.5),
        'ffn_conv_w': nrm((DEPTH, 3, 2 * D_FF), 0.2) + jnp.array([0.0, 1.0, 0.0], F32)[None, :, None],
        'ffn_conv_b': nrm((DEPTH, 2 * D_FF), 0.02),
        'ffn_w_down': nrm((DEPTH, D_FF, d), D_FF ** -0.5),
    }


def reference(x_prompt, x_sample, cache_a_k, cache_a_v, cache_b_k, cache_b_v, state_hgrn, state_rwkv,
              c, c_ctx, ada_w, ada_b, norm_mix_g, norm_ffn_g, final_norm_g,
              ev_w_in, ev_w_out, a_lambda, a_subln_g, b_q_norm_g, b_k_norm_g,
              od_w_in, od_w_out, hgrn_lb_logits, hgrn_norm_g,
              rwkv_mu, rwkv_w0, rwkv_w_up, rwkv_a0, rwkv_a_up, rwkv_g_up,
              rwkv_k_k, rwkv_k_a, rwkv_r_k, rwkv_ln_g, rwkv_ln_b,
              ffn_w_up, ffn_conv_w, ffn_conv_b, ffn_w_down):
    P = {
        'ada_w': ada_w, 'ada_b': ada_b, 'norm_mix_g': norm_mix_g, 'norm_ffn_g': norm_ffn_g,
        'final_norm_g': final_norm_g, 'ev_w_in': ev_w_in, 'ev_w_out': ev_w_out,
        'a_lambda': a_lambda, 'a_subln_g': a_subln_g, 'b_q_norm_g': b_q_norm_g, 'b_k_norm_g': b_k_norm_g,
        'od_w_in': od_w_in, 'od_w_out': od_w_out, 'hgrn_lb_logits': hgrn_lb_logits,
        'hgrn_norm_g': hgrn_norm_g, 'rwkv_mu': rwkv_mu, 'rwkv_w0': rwkv_w0, 'rwkv_w_up': rwkv_w_up,
        'rwkv_a0': rwkv_a0, 'rwkv_a_up': rwkv_a_up, 'rwkv_g_up': rwkv_g_up, 'rwkv_k_k': rwkv_k_k,
        'rwkv_k_a': rwkv_k_a, 'rwkv_r_k': rwkv_r_k, 'rwkv_ln_g': rwkv_ln_g, 'rwkv_ln_b': rwkv_ln_b,
        'ffn_w_up': ffn_w_up, 'ffn_conv_w': ffn_conv_w, 'ffn_conv_b': ffn_conv_b, 'ffn_w_down': ffn_w_down,
    }
    y_prompt, new_ctx = run_trunk(x_prompt, c_ctx[None, :], None, None, P)
    new_cache_a_k, new_cache_a_v, new_cache_b_k, new_cache_b_v, new_state_hgrn, new_state_rwkv = new_ctx
    ROWS = x_sample.shape[1] // GRID_W
    pos = grid_positions(ROWS)
    ctx_cached = (cache_a_k, cache_a_v, cache_b_k, cache_b_v, state_hgrn, state_rwkv)
    y_sample, _ = run_trunk(x_sample, c, pos, ctx_cached, P)
    return (y_prompt, y_sample, new_cache_a_k, new_cache_a_v, new_cache_b_k, new_cache_b_v, new_state_hgrn, new_state_rwkv)
```

```python
import functools
import math

import jax
import jax.numpy as jnp
import numpy as np
from jax import lax
from jax.experimental import pallas as pl
from jax.experimental.pallas import tpu as pltpu

D_MODEL = 1024
BATCH = 16
SEQ = 256
DEPTH = 2
DEC_BATCH = 4
DEC_SEQ = 1024
PAST_LEN = 512
GRID_W = 64
ROPE_THETA = 10000.0
EPS = 1e-6
RWKV_GN_EPS = 64e-5
HA = 4
DH_A = 64
DV_A = 2 * DH_A
HB = 8
HKV_B = 2
DH_B = 64
HC = 4
DK_C = 128
DV_C = 128
HD = 8
DH_D = 64
W_LORA = 64
A_LORA = 64
G_LORA = 128
D_FF = 2816

D_A = HA * DV_A
D_B = HB * DH_B
D_C = HC * DV_C
D_D = HD * DH_D
EVEN_PROJ = 2304
HGRN_PROJ = 2560
RWKV_PROJ = 1856
RWKV_PAD = 1920

N_CTX = BATCH * SEQ
N_DEC = DEC_BATCH * DEC_SEQ
N_TOK = N_CTX + N_DEC
COND_ROWS = 8

LANES = 128
SUBLANES = 8
VMEM_LIMIT = 56 * 1024 * 1024

F32 = jnp.float32
BF16 = jnp.bfloat16
HI = lax.Precision.HIGHEST

FF_TILE = 256
HGRN_L = 64
RWKV_TB = 128


def _dot(a, b):
    return jnp.dot(a, b, preferred_element_type=F32)


def _dot_hi(a, b):
    return jnp.dot(a, b, preferred_element_type=F32, precision=HI)


def _dot_nt(a, b, precision=None):
    return lax.dot_general(a, b, (((1,), (1,)), ((), ())), preferred_element_type=F32, precision=precision)


def _dot_tn(a, b, precision=None):
    return lax.dot_general(a, b, (((0,), (0,)), ((), ())), preferred_element_type=F32, precision=precision)


def _sigmoid(x):
    return 1.0 / (1.0 + jnp.exp(-x))


def _silu(x):
    return x * _sigmoid(x)


def _norm_mod(x, g, sc, sh):
    ms = jnp.mean(x * x, axis=-1, keepdims=True)
    return (x * lax.rsqrt(ms + EPS) * g) * (1.0 + sc) + sh


def _params(sem):
    return pltpu.CompilerParams(dimension_semantics=sem, vmem_limit_bytes=VMEM_LIMIT)


def _cond_row(i, tm):
    r0 = i * tm
    return jnp.where(r0 < N_CTX, 0, 1 + (r0 - N_CTX) // DEC_SEQ)


def _seq_len(i, tm):
    return jnp.where(i * tm < N_CTX, SEQ, DEC_SEQ)


def _shifted(u, seq_len):
    m = u.shape[0]
    row = lax.broadcasted_iota(jnp.int32, (m, 1), 0)
    pos = row & (seq_len - 1)
    prev = jnp.where(pos == 0, 0.0, pltpu.roll(u, 1, 0))
    nxt = jnp.where(pos == seq_len - 1, 0.0, pltpu.roll(u, m - 1, 0))
    return prev, nxt


def _mod_kernel(c_ref, w_ref, b_ref, o_ref):
    s = _silu(c_ref[...]).astype(BF16)
    o_ref[...] = _dot(s, w_ref[...].astype(BF16)) + b_ref[...]


def _mod_call(cond, ada_w, ada_b):
    tn = 1536
    n = 6 * D_MODEL
    return pl.pallas_call(
        _mod_kernel,
        grid=(DEPTH, n // tn),
        in_specs=[
            pl.BlockSpec((COND_ROWS, D_MODEL), lambda l, j: (0, 0)),
            pl.BlockSpec((None, D_MODEL, tn), lambda l, j: (l, 0, j)),
            pl.BlockSpec((None, 1, tn), lambda l, j: (l, 0, j)),
        ],
        out_specs=pl.BlockSpec((None, COND_ROWS, tn), lambda l, j: (l, 0, j)),
        out_shape=jax.ShapeDtypeStruct((DEPTH, COND_ROWS, n), F32),
        compiler_params=_params(("arbitrary", "arbitrary")),
        name="ada_mod",
    )(cond, ada_w, ada_b.reshape(DEPTH, 1, n))


def _even_proj_kernel(x_ref, mod_ref, g_ref, w_ref, qg_ref, kg_ref, cos_ref, s1_ref, s2_ref, gm_ref,
                      qa_ref, ka_ref, va_ref, qb_ref, kb_ref, vb_ref):
    d = D_MODEL
    h = _norm_mod(x_ref[...], g_ref[...], mod_ref[:, d:2 * d], mod_ref[:, 0:d]).astype(BF16)
    proj = _dot(h, w_ref[...])
    cos, s1, s2, gm = cos_ref[...], s1_ref[...], s2_ref[...], gm_ref[...]
    scale = DH_A ** -0.5

    def rope(t):
        return t * cos + pltpu.roll(t, LANES - 16, 1) * s1 + pltpu.roll(t, 16, 1) * s2

    def head_norm(t, g):
        return t * lax.rsqrt(_dot_hi(t * t, gm) + EPS) * g

    for j in range(4):
        sl = slice(j * LANES, (j + 1) * LANES)
        qa_ref[:, sl] = rope(proj[:, j * LANES:(j + 1) * LANES]) * scale
        ka_ref[:, sl] = rope(proj[:, 512 + j * LANES:512 + (j + 1) * LANES])
        va_ref[:, sl] = proj[:, 1024 + j * LANES:1024 + (j + 1) * LANES]
        qb_ref[:, sl] = rope(head_norm(proj[:, 1536 + j * LANES:1536 + (j + 1) * LANES], qg_ref[...])) * scale
    kb_ref[...] = rope(head_norm(proj[:, 2048:2176], kg_ref[...]))
    vb_ref[...] = proj[:, 2176:2304]


def _even_proj_call(x, mod_l, g, w, qg, kg, cos, s1, s2, gm):
    tm = 512
    nt = N_TOK // tm
    n_rope_blk = DEC_SEQ // tm

    def rope_idx(i):
        return (jnp.where(i * tm < N_CTX, n_rope_blk, (i - N_CTX // tm) % n_rope_blk), 0)

    full = lambda shape: pl.BlockSpec(shape, lambda i: (0,) * len(shape))
    out512 = pl.BlockSpec((tm, 512), lambda i: (i, 0))
    out128 = pl.BlockSpec((tm, LANES), lambda i: (i, 0))
    return pl.pallas_call(
        _even_proj_kernel,
        grid=(nt,),
        in_specs=[
            pl.BlockSpec((tm, D_MODEL), lambda i: (i, 0)),
            pl.BlockSpec((None, 1, 6 * D_MODEL), lambda i: (_cond_row(i, tm), 0, 0)),
            full((1, D_MODEL)),
            full((D_MODEL, EVEN_PROJ)),
            full((1, LANES)),
            full((1, LANES)),
            pl.BlockSpec((tm, LANES), rope_idx),
            pl.BlockSpec((tm, LANES), rope_idx),
            pl.BlockSpec((tm, LANES), rope_idx),
            full((LANES, LANES)),
        ],
        out_specs=[out512, out512, out512, out512, out128, out128],
        out_shape=[jax.ShapeDtypeStruct((N_TOK, 512), F32)] * 4 + [jax.ShapeDtypeStruct((N_TOK, LANES), F32)] * 2,
        compiler_params=_params(("arbitrary",)),
        name="even_proj",
    )(x, mod_l, g, w, qg, kg, cos, s1, s2, gm)


def _softmax_pv(q, ks, vs):
    ss = [_dot_nt(q, k) for k in ks]
    m = functools.reduce(jnp.maximum, [jnp.max(s, axis=-1, keepdims=True) for s in ss])
    ps = [jnp.exp(s - m) for s in ss]
    l = functools.reduce(jnp.add, [jnp.sum(p, axis=-1, keepdims=True) for p in ps])
    acc = functools.reduce(jnp.add, [_dot(p.astype(BF16), v) for p, v in zip(ps, vs)])
    return acc / l


def _attn_kernel(*refs, has_cache, lam_init):
    if has_cache:
        (qa_ref, qb_ref, ka_ref, va_ref, kb_ref, vb_ref, cak_ref, cav_ref, cbk_ref, cbv_ref,
         al_ref, sg_ref, oa_ref, ob_ref) = refs
    else:
        qa_ref, qb_ref, ka_ref, va_ref, kb_ref, vb_ref, al_ref, sg_ref, oa_ref, ob_ref = refs
    al = al_ref[...]
    lam = (jnp.exp(jnp.sum(al[0:1] * al[1:2], axis=-1, keepdims=True))
           - jnp.exp(jnp.sum(al[2:3] * al[3:4], axis=-1, keepdims=True)) + lam_init)
    lo = lax.broadcasted_iota(jnp.int32, (1, LANES), 1) < DH_A

    for h in range(HA):
        sl = slice(h * LANES, (h + 1) * LANES)
        q = qa_ref[:, sl]
        ks = [ka_ref[:, sl].astype(BF16)]
        vs = [va_ref[:, sl].astype(BF16)]
        if has_cache:
            ks.insert(0, cak_ref[h].astype(BF16))
            vs.insert(0, cav_ref[h].astype(BF16))
        a1 = _softmax_pv(jnp.where(lo, q, 0.0).astype(BF16), ks, vs)
        a2 = _softmax_pv(jnp.where(lo, 0.0, q).astype(BF16), ks, vs)
        dlt = a1 - lam * a2
        ms = jnp.mean(dlt * dlt, axis=-1, keepdims=True)
        oa_ref[:, sl] = dlt * lax.rsqrt(ms + EPS) * sg_ref[...] * (1.0 - lam_init)

    ks = [kb_ref[...].astype(BF16)]
    vs = [vb_ref[...].astype(BF16)]
    if has_cache:
        ks.insert(0, cbk_ref[...].astype(BF16))
        vs.insert(0, cbv_ref[...].astype(BF16))
    for j in range(HB // 2):
        sl = slice(j * LANES, (j + 1) * LANES)
        q = qb_ref[:, sl]
        o0 = _softmax_pv(jnp.where(lo, q, 0.0).astype(BF16), ks, vs)
        o1 = _softmax_pv(jnp.where(lo, 0.0, q).astype(BF16), ks, vs)
        ob_ref[:, sl] = jnp.where(lo, o0, o1)


def _attn_call(qa, qb, ka, va, kb, vb, cache, a_lambda, subln_g, lam_init, *, n_seq, seq, row0):
    tq = 256
    nq = seq // tq
    qblk0 = row0 // tq
    sblk0 = row0 // seq
    has_cache = cache is not None
    qspec = pl.BlockSpec((tq, 512), lambda b, i: (qblk0 + b * nq + i, 0))
    own512 = pl.BlockSpec((seq, 512), lambda b, i: (sblk0 + b, 0))
    own128 = pl.BlockSpec((seq, LANES), lambda b, i: (sblk0 + b, 0))
    in_specs = [qspec, qspec, own512, own512, own128, own128]
    args = [qa, qb, ka, va, kb, vb]
    if has_cache:
        in_specs += [
            pl.BlockSpec((None, HA, PAST_LEN, LANES), lambda b, i: (b, 0, 0, 0)),
            pl.BlockSpec((None, HA, PAST_LEN, LANES), lambda b, i: (b, 0, 0, 0)),
            pl.BlockSpec((None, PAST_LEN, LANES), lambda b, i: (b, 0, 0)),
            pl.BlockSpec((None, PAST_LEN, LANES), lambda b, i: (b, 0, 0)),
        ]
        args += list(cache)
    in_specs += [pl.BlockSpec((4, DH_A), lambda b, i: (0, 0)), pl.BlockSpec((1, LANES), lambda b, i: (0, 0))]
    args += [a_lambda, subln_g]
    ospec = pl.BlockSpec((tq, 512), lambda b, i: (b * nq + i, 0))
    return pl.pallas_call(
        functools.partial(_attn_kernel, has_cache=has_cache, lam_init=lam_init),
        grid=(n_seq, nq),
        in_specs=in_specs,
        out_specs=[ospec, ospec],
        out_shape=[jax.ShapeDtypeStruct((n_seq * seq, 512), F32)] * 2,
        compiler_params=_params(("arbitrary", "arbitrary")),
        name="attn_dec" if has_cache else "attn_ctx",
    )(*args)


def _out_proj_kernel(x_ref, a_ref, b_ref, mod_ref, w_ref, o_ref):
    d = D_MODEL
    half = a_ref.shape[1]
    mix = _dot(a_ref[...].astype(BF16), w_ref[0:half, :]) + _dot(b_ref[...].astype(BF16), w_ref[half:2 * half, :])
    o_ref[...] = x_ref[...] + mod_ref[:, 2 * d:3 * d] * mix


def _out_proj_call(x, a, b, mod_l, w):
    tm = 512
    return pl.pallas_call(
        _out_proj_kernel,
        grid=(N_TOK // tm,),
        in_specs=[
            pl.BlockSpec((tm, D_MODEL), lambda i: (i, 0)),
            pl.BlockSpec((tm, 512), lambda i: (i, 0)),
            pl.BlockSpec((tm, 512), lambda i: (i, 0)),
            pl.BlockSpec((None, 1, 6 * D_MODEL), lambda i: (_cond_row(i, tm), 0, 0)),
            pl.BlockSpec((D_MODEL, D_MODEL), lambda i: (0, 0)),
        ],
        out_specs=pl.BlockSpec((tm, D_MODEL), lambda i: (i, 0)),
        out_shape=jax.ShapeDtypeStruct((N_TOK, D_MODEL), F32),
        compiler_params=_params(("arbitrary",)),
        name="out_proj",
    )(x, a, b, mod_l, w)


def _ffn_kernel(x_ref, mod_ref, g_ref, wv_ref, wg_ref, cwv_ref, cwg_ref, cbv_ref, cbg_ref, wd_ref, fg_ref,
                o_ref, h_scr, acc_scr, *, tm, final):
    d = D_MODEL
    i, j = pl.program_id(0), pl.program_id(1)

    @pl.when(j == 0)
    def _():
        h_scr[...] = _norm_mod(x_ref[...], g_ref[...], mod_ref[:, 4 * d:5 * d], mod_ref[:, 3 * d:4 * d]).astype(BF16)
        acc_scr[...] = jnp.zeros_like(acc_scr)

    seq_len = _seq_len(i, tm)
    h = h_scr[...]

    def conv(w_ref, cw_ref, cb_ref):
        u = _dot(h, w_ref[...])
        prev, nxt = _shifted(u, seq_len)
        return cw_ref[0:1, :] * prev + cw_ref[1:2, :] * u + cw_ref[2:3, :] * nxt + cb_ref[...]

    val = conv(wv_ref, cwv_ref, cbv_ref)
    gate = conv(wg_ref, cwg_ref, cbg_ref)
    act = (_silu(gate) * val).astype(BF16)
    acc_scr[...] += _dot(act, wd_ref[...])

    @pl.when(j == pl.num_programs(1) - 1)
    def _():
        y = x_ref[...] + mod_ref[:, 5 * d:6 * d] * acc_scr[...]
        if final:
            ms = jnp.mean(y * y, axis=-1, keepdims=True)
            y = y * lax.rsqrt(ms + EPS) * fg_ref[...]
        o_ref[...] = y


def _ffn_call(x, mod_l, g, w_up, conv_w, conv_b, w_down, final_g, *, final):
    tm = DEC_SEQ
    nf = D_FF // FF_TILE
    return pl.pallas_call(
        functools.partial(_ffn_kernel, tm=tm, final=final),
        grid=(N_TOK // tm, nf),
        in_specs=[
            pl.BlockSpec((tm, D_MODEL), lambda i, j: (i, 0)),
            pl.BlockSpec((None, 1, 6 * D_MODEL), lambda i, j: (_cond_row(i, tm), 0, 0)),
            pl.BlockSpec((1, D_MODEL), lambda i, j: (0, 0)),
            pl.BlockSpec((D_MODEL, FF_TILE), lambda i, j: (0, j)),
            pl.BlockSpec((D_MODEL, FF_TILE), lambda i, j: (0, nf + j)),
            pl.BlockSpec((3, FF_TILE), lambda i, j: (0, j)),
            pl.BlockSpec((3, FF_TILE), lambda i, j: (0, nf + j)),
            pl.BlockSpec((1, FF_TILE), lambda i, j: (0, j)),
            pl.BlockSpec((1, FF_TILE), lambda i, j: (0, nf + j)),
            pl.BlockSpec((FF_TILE, D_MODEL), lambda i, j: (j, 0)),
            pl.BlockSpec((1, D_MODEL), lambda i, j: (0, 0)),
        ],
        out_specs=pl.BlockSpec((tm, D_MODEL), lambda i, j: (i, 0)),
        out_shape=jax.ShapeDtypeStruct((N_TOK, D_MODEL), F32),
        scratch_shapes=[pltpu.VMEM((tm, D_MODEL), BF16), pltpu.VMEM((tm, D_MODEL), F32)],
        compiler_params=_params(("arbitrary", "arbitrary")),
        name="conv_ffn",
    )(x, mod_l, g, w_up, w_up, conv_w, conv_w, conv_b, conv_b, w_down, final_g)


def _odd_proj_kernel(x_ref, mod_ref, g_ref, w_ref, mu_ref, o_ref, h_scr, *, tm, shift):
    d = D_MODEL
    i, j = pl.program_id(0), pl.program_id(1)

    @pl.when(j == 0)
    def _():
        h_scr[...] = _norm_mod(x_ref[...], g_ref[...], mod_ref[:, d:2 * d], mod_ref[:, 0:d]).astype(BF16)

    p = _dot(h_scr[...], w_ref[...])
    if shift:
        prev, nxt = _shifted(p, _seq_len(i, tm))
        p = p + mu_ref[...] * (0.5 * (prev + nxt) - p)
    o_ref[...] = p


def _odd_proj_call(x, mod_l, g, w, mu, *, shift):
    tm = DEC_SEQ
    tn = 640
    n = w.shape[1]
    return pl.pallas_call(
        functools.partial(_odd_proj_kernel, tm=tm, shift=shift),
        grid=(N_TOK // tm, n // tn),
        in_specs=[
            pl.BlockSpec((tm, D_MODEL), lambda i, j: (i, 0)),
            pl.BlockSpec((None, 1, 6 * D_MODEL), lambda i, j: (_cond_row(i, tm), 0, 0)),
            pl.BlockSpec((1, D_MODEL), lambda i, j: (0, 0)),
            pl.BlockSpec((D_MODEL, tn), lambda i, j: (0, j)),
            pl.BlockSpec((1, tn), lambda i, j: (0, j)),
        ],
        out_specs=pl.BlockSpec((tm, tn), lambda i, j: (i, j)),
        out_shape=jax.ShapeDtypeStruct((N_TOK, n), F32),
        scratch_shapes=[pltpu.VMEM((tm, D_MODEL), BF16)],
        compiler_params=_params(("arbitrary", "arbitrary")),
        name="odd_proj_rwkv" if shift else "odd_proj_hgrn",
    )(x, mod_l, g, w, mu)


def _hgrn_kernel(q_ref, ff_ref, fb_ref, v_ref, gc_ref, lbl_ref, ng_ref, s0_ref, o_ref, sfin_ref, oacc,
                 *, seq, lidx):
    L = HGRN_L
    nc = seq // L
    nb = L // SUBLANES

    def lower_bound(drc):
        lg = lbl_ref[drc]
        e = jnp.exp(lg - jnp.max(lg, axis=0, keepdims=True))
        sm = e / jnp.sum(e, axis=0, keepdims=True)
        return functools.reduce(jnp.add, [sm[i:i + 1, :] for i in range(1, lidx + 1)])

    rowi = lax.broadcasted_iota(jnp.int32, (L, L), 0)
    coli = lax.broadcasted_iota(jnp.int32, (L, L), 1)
    rows8 = lax.broadcasted_iota(jnp.int32, (SUBLANES, L), 0)
    cols8 = lax.broadcasted_iota(jnp.int32, (SUBLANES, L), 1)

    for drc in range(2):
        rev = drc == 1
        f_ref = fb_ref if rev else ff_ref
        lb = lower_bound(drc)
        tri = jnp.where((coli >= rowi) if rev else (coli <= rowi), 1.0, 0.0).astype(F32)

        def chunk(ci, st, rev=rev, f_ref=f_ref, lb=lb, tri=tri):
            c = (nc - 1 - ci) if rev else ci
            r0 = pl.multiple_of(c * L, L)
            q = _silu(q_ref[pl.ds(r0, L), :])
            f = lb + (1.0 - lb) * _sigmoid(f_ref[pl.ds(r0, L), :])
            k = 1.0 - f
            lf = jnp.log(f)
            v = v_ref[pl.ds(r0, L), :]
            cum = _dot_hi(tri, lf)
            cum_ex = cum - lf
            o = _dot_nt(q * jnp.exp(cum), st, HI)

            blocks = []
            for tb in range(nb):
                rs = slice(tb * SUBLANES, (tb + 1) * SUBLANES)
                edge = (tb + 1) * SUBLANES - 1 if rev else tb * SUBLANES
                anchor = cum_ex[edge:edge + 1, :]
                qe = q[rs] * jnp.exp(cum[rs] - anchor)
                ke = k * jnp.exp(jnp.minimum(anchor - cum, 0.0))
                sc = _dot_nt(qe, ke, HI)
                outside = (cols8 > edge) if rev else (cols8 < edge)
                sc = jnp.where(outside, sc, 0.0)
                for s_l in range(SUBLANES):
                    s = tb * SUBLANES + s_l
                    ed = jnp.exp(jnp.minimum(cum[rs] - cum[s:s + 1, :], 0.0))
                    col = jnp.sum(q[rs] * k[s:s + 1, :] * ed, axis=-1, keepdims=True)
                    ok = (rows8 <= s_l) if rev else (rows8 >= s_l)
                    sc = jnp.where((cols8 == s) & ok, col, sc)
                blocks.append(sc)
            scores = jnp.concatenate(blocks, axis=0)
            o = o + _dot_hi(scores, v)

            end = 0 if rev else L - 1
            cend = cum[end:end + 1, :]
            kd = k * jnp.exp(cend - cum)
            st = st * jnp.exp(cend) + _dot_tn(v, kd, HI)
            if rev:
                oacc[pl.ds(r0, L), :] += o
            else:
                oacc[pl.ds(r0, L), :] = o
            return st

        st = lax.fori_loop(0, nc, chunk, s0_ref[drc].T)
        sfin_ref[drc] = st.T

    o = oacc[...]
    ms = jnp.mean(o * o, axis=-1, keepdims=True)
    o_ref[...] = o * lax.rsqrt(ms + EPS) * ng_ref[...] * _silu(gc_ref[...])


def _hgrn_call(proj_h, lb_logits, norm_g, s0, *, n_seq, seq, row0, lidx):
    sblk0 = row0 // seq

    def col(section):
        return pl.BlockSpec((seq, LANES), lambda b, h: (sblk0 + b, section * HC + h))

    st_spec = pl.BlockSpec((None, 2, None, DK_C, DV_C), lambda b, h: (b, 0, h, 0, 0))
    return pl.pallas_call(
        functools.partial(_hgrn_kernel, seq=seq, lidx=lidx),
        grid=(n_seq, HC),
        in_specs=[
            col(0), col(1), col(2), col(3), col(4),
            pl.BlockSpec((2, DEPTH, LANES), lambda b, h: (0, 0, h)),
            pl.BlockSpec((1, LANES), lambda b, h: (0, 0)),
            st_spec,
        ],
        out_specs=[pl.BlockSpec((seq, LANES), lambda b, h: (b, h)), st_spec],
        out_shape=[jax.ShapeDtypeStruct((n_seq * seq, D_C), F32),
                   jax.ShapeDtypeStruct((n_seq, 2, HC, DK_C, DV_C), F32)],
        scratch_shapes=[pltpu.VMEM((seq, LANES), F32)],
        compiler_params=_params(("arbitrary", "arbitrary")),
        name="hgrn_dec" if row0 else "hgrn_ctx",
    )(proj_h, proj_h, proj_h, proj_h, proj_h, lb_logits, norm_g, s0)


def _rwkv_prep_kernel(p_ref, a0_ref, aup_ref, gup_ref, kk_ref, ka_ref, rk_ref, w0_ref, wup0_ref, wup1_ref,
                      gs_ref, w0o_ref, w1o_ref, nkk_ref, bb_ref, kt_ref, g_ref, rkv_ref):
    r = p_ref[:, 0:512]
    k = p_ref[:, 512:1024]
    v = p_ref[:, 1024:1536]
    wd = p_ref[:, 1536:1664]
    gd = p_ref[:, 1664:1792]
    ad = p_ref[:, 1792:1920]
    gs = gs_ref[...]

    def group_sum(t):
        return jnp.concatenate([_dot_hi(t[:, j * LANES:(j + 1) * LANES], gs) for j in range(4)], axis=1)

    a = _sigmoid(a0_ref[...] + _dot_hi(ad, aup_ref[...]))
    g_ref[...] = _dot_hi(_sigmoid(gd), gup_ref[...])
    kkr = k * kk_ref[...]
    kk = kkr / jnp.maximum(jnp.sqrt(group_sum(kkr * kkr)), 1e-12)
    kt = k * (1.0 + (a - 1.0) * ka_ref[...])
    th = jnp.tanh(wd)
    decay = math.exp(-0.5)
    w0o_ref[...] = jnp.exp(-decay * _sigmoid(w0_ref[0:1, :] + _dot_hi(th, wup0_ref[...])))
    w1o_ref[...] = jnp.exp(-decay * _sigmoid(w0_ref[1:2, :] + _dot_hi(th, wup1_ref[...])))
    nkk_ref[...] = -kk
    bb_ref[...] = kk * a
    kt_ref[...] = kt
    rkv_ref[...] = group_sum(r * kt * rk_ref[...]) * v


def _rwkv_prep_call(proj_r, a0, aup, gup, kk_k, k_a, r_k, w0, wup0, wup1, gs):
    tm = 512
    full = lambda shape: pl.BlockSpec(shape, lambda i: (0,) * len(shape))
    ospec = pl.BlockSpec((tm, D_D), lambda i: (i, 0))
    return pl.pallas_call(
        _rwkv_prep_kernel,
        grid=(N_TOK // tm,),
        in_specs=[
            pl.BlockSpec((tm, RWKV_PAD), lambda i: (i, 0)),
            full((1, D_D)), full((LANES, D_D)), full((LANES, D_D)), full((1, D_D)), full((1, D_D)), full((1, D_D)),
            full((2, D_D)), full((LANES, D_D)), full((LANES, D_D)), full((LANES, LANES)),
        ],
        out_specs=[ospec] * 7,
        out_shape=[jax.ShapeDtypeStruct((N_TOK, D_D), F32)] * 7,
        compiler_params=_params(("arbitrary",)),
        name="rwkv_prep",
    )(proj_r, a0, aup, gup, kk_k, k_a, r_k, w0, wup0, wup1, gs)


def _rwkv_scan_kernel(rf_ref, wf_ref, nf_ref, bf_ref, kf_ref, vf_ref, rb_ref, wb_ref, nb_ref, bb_ref, kb_ref,
                      vb_ref, s0_ref, of_ref, ob_ref, sfin_ref, st, vt, ot):
    tb = pl.program_id(1)
    ngrp = D_D // LANES
    steps = RWKV_TB

    @pl.when(tb == 0)
    def _():
        st[...] = s0_ref[...]

    for g in range(ngrp):
        sl = slice(g * LANES, (g + 1) * LANES)
        vt[0, g] = vf_ref[:, sl].T
        vt[1, g] = vb_ref[:, sl].T
    ot[...] = jnp.zeros_like(ot)

    lane = lax.broadcasted_iota(jnp.int32, (1, LANES), 1)
    lo = lane < DH_D
    dirs = ((rf_ref, wf_ref, nf_ref, bf_ref, kf_ref), (rb_ref, wb_ref, nb_ref, bb_ref, kb_ref))

    def half_sums(t):
        s_lo = jnp.sum(jnp.where(lo, t, 0.0), axis=1, keepdims=True)
        s_hi = jnp.sum(jnp.where(lo, 0.0, t), axis=1, keepdims=True)
        return s_lo, s_hi

    def step_group(t8, carry):
        bases = (pl.multiple_of(t8 * SUBLANES, SUBLANES), pl.multiple_of(steps - (t8 + 1) * SUBLANES, SUBLANES))
        rows = [[ref[pl.ds(bases[drc], SUBLANES), :] for ref in dirs[drc]] for drc in range(2)]
        for j in range(SUBLANES):
            for drc in range(2):
                loc = (SUBLANES - 1 - j) if drc else j
                r_blk, w_blk, n_blk, b_blk, k_blk = rows[drc]
                here = lane == bases[drc] + loc
                for g in range(ngrp):
                    sl = slice(g * LANES, (g + 1) * LANES)
                    s = st[drc, g]
                    sa_lo, sa_hi = half_sums(s * n_blk[loc:loc + 1, sl])
                    sa = jnp.where(lo, sa_lo, sa_hi)
                    vc = jnp.sum(jnp.where(here, vt[drc, g], 0.0), axis=1, keepdims=True)
                    vcol = jnp.where(lo, vc[0:DH_D], vc[DH_D:2 * DH_D])
                    s = s * w_blk[loc:loc + 1, sl] + sa * b_blk[loc:loc + 1, sl] + vcol * k_blk[loc:loc + 1, sl]
                    st[drc, g] = s
                    o_lo, o_hi = half_sums(s * r_blk[loc:loc + 1, sl])
                    ot[drc, g, 0:DH_D, :] = jnp.where(here, o_lo, ot[drc, g, 0:DH_D, :])
                    ot[drc, g, DH_D:2 * DH_D, :] = jnp.where(here, o_hi, ot[drc, g, DH_D:2 * DH_D, :])
        return carry

    lax.fori_loop(0, steps // SUBLANES, step_group, 0)

    for g in range(ngrp):
        sl = slice(g * LANES, (g + 1) * LANES)
        of_ref[:, sl] = ot[0, g].T
        ob_ref[:, sl] = ot[1, g].T

    @pl.when(tb == pl.num_programs(1) - 1)
    def _():
        sfin_ref[...] = st[...]


def _rwkv_scan_call(r_src, w0, w1, nkk, bb, kt, s0, *, n_seq, seq, row0):
    ntb = seq // RWKV_TB
    blk0 = row0 // RWKV_TB
    ngrp = D_D // LANES

    def fwd(cb=0):
        return pl.BlockSpec((RWKV_TB, D_D), lambda b, t: (blk0 + b * ntb + t, cb))

    def bwd(cb=0):
        return pl.BlockSpec((RWKV_TB, D_D), lambda b, t: (blk0 + b * ntb + ntb - 1 - t, cb))

    st_spec = pl.BlockSpec((None, 2, ngrp, DH_D, LANES), lambda b, t: (b, 0, 0, 0, 0))
    return pl.pallas_call(
        _rwkv_scan_kernel,
        grid=(n_seq, ntb),
        in_specs=[fwd(0), fwd(), fwd(), fwd(), fwd(), fwd(2), bwd(0), bwd(), bwd(), bwd(), bwd(), bwd(2), st_spec],
        out_specs=[
            pl.BlockSpec((RWKV_TB, D_D), lambda b, t: (b * ntb + t, 0)),
            pl.BlockSpec((RWKV_TB, D_D), lambda b, t: (b * ntb + ntb - 1 - t, 0)),
            st_spec,
        ],
        out_shape=[jax.ShapeDtypeStruct((n_seq * seq, D_D), F32)] * 2
        + [jax.ShapeDtypeStruct((n_seq, 2, ngrp, DH_D, LANES), F32)],
        scratch_shapes=[pltpu.VMEM((2, ngrp, DH_D, LANES), F32), pltpu.VMEM((2, ngrp, LANES, LANES), F32),
                        pltpu.VMEM((2, ngrp, LANES, LANES), F32)],
        compiler_params=_params(("arbitrary", "arbitrary")),
        name="rwkv_dec" if row0 else "rwkv_ctx",
    )(r_src, w0, nkk, bb, kt, r_src, r_src, w1, nkk, bb, kt, r_src, s0)


def _rwkv_post_kernel(of_ref, ob_ref, rkv_ref, g_ref, lng_ref, lnb_ref, gm_ref, o_ref):
    gm = gm_ref[...]
    for j in range(D_D // LANES):
        sl = slice(j * LANES, (j + 1) * LANES)
        o = of_ref[:, sl] + ob_ref[:, sl]
        dlt = o - _dot_hi(o, gm)
        var = _dot_hi(dlt * dlt, gm)
        y = dlt * lax.rsqrt(var + RWKV_GN_EPS) * lng_ref[:, sl] + lnb_ref[:, sl]
        o_ref[:, sl] = (y + rkv_ref[:, sl]) * g_ref[:, sl]


def _rwkv_post_call(o_f, o_b, rkv, g, ln_g, ln_b, gm):
    tm = 512
    tok = pl.BlockSpec((tm, D_D), lambda i: (i, 0))
    row = pl.BlockSpec((1, D_D), lambda i: (0, 0))
    return pl.pallas_call(
        _rwkv_post_kernel,
        grid=(N_TOK // tm,),
        in_specs=[tok, tok, tok, tok, row, row, pl.BlockSpec((LANES, LANES), lambda i: (0, 0))],
        out_specs=tok,
        out_shape=jax.ShapeDtypeStruct((N_TOK, D_D), F32),
        compiler_params=_params(("arbitrary",)),
        name="rwkv_post",
    )(o_f, o_b, rkv, g, ln_g, ln_b, gm)


def _rope_tables():
    pos = np.arange(DEC_SEQ)
    pr, pc = pos // GRID_W, pos % GRID_W
    lane = np.arange(LANES)
    dd = lane % DH_A
    use_col = (dd // 32) == 1
    j = dd % 16
    is_lo = (dd % 32) < 16
    freq = ROPE_THETA ** (-(j.astype(np.float64)) / 16.0)
    p = np.where(use_col[None, :], pc[:, None], pr[:, None]).astype(np.float64)
    ang = (p.astype(np.float32) * freq.astype(np.float32)[None, :]).astype(np.float32)
    cos = np.cos(ang).astype(np.float32)
    sin = np.sin(ang).astype(np.float32)
    s1 = np.where(is_lo[None, :], -sin, 0.0).astype(np.float32)
    s2 = np.where(is_lo[None, :], 0.0, sin).astype(np.float32)
    ident = 512
    cos = np.concatenate([cos, np.ones((ident, LANES), np.float32)], 0)
    s1 = np.concatenate([s1, np.zeros((ident, LANES), np.float32)], 0)
    s2 = np.concatenate([s2, np.zeros((ident, LANES), np.float32)], 0)
    return jnp.asarray(cos), jnp.asarray(s1), jnp.asarray(s2)


def _block_diag(value):
    m = np.zeros((LANES, LANES), np.float32)
    half = LANES // 2
    m[:half, :half] = value
    m[half:, half:] = value
    return jnp.asarray(m)


def _qb_perm():
    idx = np.zeros(D_B, np.int32)
    for j in range(HB // 2):
        for hh in range(2):
            for dch in range(DH_B):
                idx[j * LANES + hh * DH_B + dch] = (hh * (HB // 2) + j) * DH_B + dch
    return idx


def kernel(x_prompt, x_sample, cache_a_k, cache_a_v, cache_b_k, cache_b_v, state_hgrn, state_rwkv, c, c_ctx, ada_w, ada_b, norm_mix_g, norm_ffn_g, final_norm_g, ev_w_in, ev_w_out, a_lambda, a_subln_g, b_q_norm_g, b_k_norm_g, od_w_in, od_w_out, hgrn_lb_logits, hgrn_norm_g, rwkv_mu, rwkv_w0, rwkv_w_up, rwkv_a0, rwkv_a_up, rwkv_g_up, rwkv_k_k, rwkv_k_a, rwkv_r_k, rwkv_ln_g, rwkv_ln_b, ffn_w_up, ffn_conv_w, ffn_conv_b, ffn_w_down):
    d = D_MODEL
    x = jnp.concatenate([x_prompt.reshape(N_CTX, d), x_sample.reshape(N_DEC, d)], axis=0)
    cond = jnp.concatenate([c_ctx[None, :], c, jnp.zeros((COND_ROWS - 1 - DEC_BATCH, d), F32)], axis=0)
    mod = _mod_call(cond, ada_w, ada_b).reshape(DEPTH, COND_ROWS, 1, 6 * d)

    cos, s1, s2 = _rope_tables()
    g_mean = _block_diag(1.0 / DH_B)
    g_sum = _block_diag(1.0)
    perm = _qb_perm()
    row = lambda t: t.reshape(1, -1)

    new_ctx = None
    for l in range(DEPTH):
        i = l // 2
        mod_l = mod[l]
        if l % 2 == 0:
            w_in = ev_w_in[i]
            w_in = jnp.concatenate([w_in[:, :1536], w_in[:, 1536:2048][:, perm], w_in[:, 2048:]], axis=1).astype(BF16)
            w_out = ev_w_out[i]
            w_out = jnp.concatenate([w_out[:D_A], w_out[D_A:][perm]], axis=0).astype(BF16)
            qg = row(jnp.tile(b_q_norm_g[i], 2))
            kg = row(jnp.tile(b_k_norm_g[i], 2))
            qa, ka, va, qb, kb, vb = _even_proj_call(x, mod_l, row(norm_mix_g[l]), w_in, qg, kg, cos, s1, s2, g_mean)
            lam_init = 0.8 - 0.6 * math.exp(-0.3 * l)
            sub_g = row(a_subln_g[i])
            oa_c, ob_c = _attn_call(qa, qb, ka, va, kb, vb, None, a_lambda[i], sub_g, lam_init,
                                    n_seq=BATCH, seq=SEQ, row0=0)
            cbk = jnp.transpose(cache_b_k[:, i], (0, 2, 1, 3)).reshape(DEC_BATCH, PAST_LEN, HKV_B * DH_B)
            cbv = jnp.transpose(cache_b_v[:, i], (0, 2, 1, 3)).reshape(DEC_BATCH, PAST_LEN, HKV_B * DH_B)
            cache = (cache_a_k[:, i], cache_a_v[:, i], cbk, cbv)
            oa_s, ob_s = _attn_call(qa, qb, ka, va, kb, vb, cache, a_lambda[i], sub_g, lam_init,
                                    n_seq=DEC_BATCH, seq=DEC_SEQ, row0=N_CTX)
            mix_a = jnp.concatenate([oa_c, oa_s], axis=0)
            mix_b = jnp.concatenate([ob_c, ob_s], axis=0)

            def heads_first(t, nh):
                t = t[:N_CTX].reshape(BATCH, SEQ, nh, -1)
                return jnp.transpose(t, (0, 2, 1, 3))[:, None]

            even_ctx = (heads_first(ka, HA), heads_first(va, HA), heads_first(kb, HKV_B), heads_first(vb, HKV_B))
        else:
            w = od_w_in[i]
            w_h = w[:, :HGRN_PROJ].astype(BF16)
            wr = w[:, HGRN_PROJ:]
            mu = rwkv_mu[i]

            def rwkv_cols(t):
                z = jnp.zeros(t.shape[:-1] + (RWKV_PAD - RWKV_PROJ,), t.dtype)
                return jnp.concatenate([t[..., :1664], t[..., 1728:1856], t[..., 1664:1728], z], axis=-1)

            w_r = rwkv_cols(wr).astype(BF16)
            mu_r = row(rwkv_cols(mu))
            g_l = row(norm_mix_g[l])
            proj_h = _odd_proj_call(x, mod_l, g_l, w_h, jnp.zeros((1, HGRN_PROJ), F32), shift=False)
            proj_r = _odd_proj_call(x, mod_l, g_l, w_r, mu_r, shift=True)

            ng = row(hgrn_norm_g[i])
            zero_h = jnp.zeros((BATCH, 2, HC, DK_C, DV_C), F32)
            oc_c, sh_c = _hgrn_call(proj_h, hgrn_lb_logits, ng, zero_h, n_seq=BATCH, seq=SEQ, row0=0, lidx=l)
            oc_s, _ = _hgrn_call(proj_h, hgrn_lb_logits, ng, state_hgrn[:, i], n_seq=DEC_BATCH, seq=DEC_SEQ,
                                 row0=N_CTX, lidx=l)

            pad_rows = lambda t: jnp.concatenate([t, jnp.zeros((LANES - t.shape[0], t.shape[1]), F32)], axis=0)
            wup0 = pad_rows(rwkv_w_up[i, 0])
            wup1 = jnp.concatenate([jnp.zeros((W_LORA, D_D), F32), rwkv_w_up[i, 1]], axis=0)
            w0o, w1o, nkk, bb, kt, gg, rkv = _rwkv_prep_call(
                proj_r, row(rwkv_a0[i]), pad_rows(rwkv_a_up[i]), rwkv_g_up[i], row(rwkv_k_k[i]), row(rwkv_k_a[i]),
                row(rwkv_r_k[i]), rwkv_w0[i], wup0, wup1, g_sum)

            def to_tiles(s):
                b = s.shape[0]
                s = s.reshape(b, 2, HD // 2, 2, DH_D, DH_D)
                return jnp.transpose(s, (0, 1, 2, 4, 3, 5)).reshape(b, 2, HD // 2, DH_D, LANES)

            def from_tiles(s):
                b = s.shape[0]
                s = s.reshape(b, 2, HD // 2, DH_D, 2, DH_D)
                return jnp.transpose(s, (0, 1, 2, 4, 3, 5)).reshape(b, 2, HD, DH_D, DH_D)

            zero_r = jnp.zeros((BATCH, 2, HD // 2, DH_D, LANES), F32)
            of_c, ob_c, sr_c = _rwkv_scan_call(proj_r, w0o, w1o, nkk, bb, kt, zero_r, n_seq=BATCH, seq=SEQ, row0=0)
            of_s, ob_s, _ = _rwkv_scan_call(proj_r, w0o, w1o, nkk, bb, kt, to_tiles(state_rwkv[:, i]),
                                            n_seq=DEC_BATCH, seq=DEC_SEQ, row0=N_CTX)
            o_f = jnp.concatenate([of_c, of_s], axis=0)
            o_b = jnp.concatenate([ob_c, ob_s], axis=0)
            mix_b = _rwkv_post_call(o_f, o_b, rkv, gg, row(rwkv_ln_g[i]), row(rwkv_ln_b[i]), g_mean)
            mix_a = jnp.concatenate([oc_c, oc_s], axis=0)
            w_out = od_w_out[i].astype(BF16)
            odd_ctx = (sh_c[:, None], from_tiles(sr_c)[:, None])

        x = _out_proj_call(x, mix_a, mix_b, mod_l, w_out)
        x = _ffn_call(x, mod_l, row(norm_ffn_g[l]), ffn_w_up[l].astype(BF16), ffn_conv_w[l], row(ffn_conv_b[l]),
                      ffn_w_down[l].astype(BF16), row(final_norm_g), final=(l == DEPTH - 1))

    y_prompt = x[:N_CTX].reshape(BATCH, SEQ, d)
    y_sample = x[N_CTX:].reshape(DEC_BATCH, DEC_SEQ, d)
    return (y_prompt, y_sample) + even_ctx + odd_ctx
```

```python
import functools
import math

import jax
import jax.numpy as jnp
import numpy as np
from jax import lax
from jax.experimental import pallas as pl
from jax.experimental.pallas import tpu as pltpu

D_MODEL = 1024
BATCH = 16
SEQ = 256
DEPTH = 2
DEC_BATCH = 4
DEC_SEQ = 1024
PAST_LEN = 512
GRID_W = 64
ROPE_THETA = 10000.0
EPS = 1e-6
RWKV_GN_EPS = 64e-5
HA = 4
DH_A = 64
DV_A = 2 * DH_A
HB = 8
HKV_B = 2
DH_B = 64
HC = 4
DK_C = 128
DV_C = 128
HD = 8
DH_D = 64
W_LORA = 64
A_LORA = 64
G_LORA = 128
D_FF = 2816

D_A = HA * DV_A
D_B = HB * DH_B
D_C = HC * DV_C
D_D = HD * DH_D
EVEN_PROJ = 2304
HGRN_PROJ = 2560
RWKV_PROJ = 1856
RWKV_PAD = 1920

N_CTX = BATCH * SEQ
N_DEC = DEC_BATCH * DEC_SEQ
N_TOK = N_CTX + N_DEC
COND_ROWS = 8

LANES = 128
SUBLANES = 8
VMEM_LIMIT = 56 * 1024 * 1024

F32 = jnp.float32
BF16 = jnp.bfloat16
HI = lax.Precision.HIGHEST

FF_TILE = 256
HGRN_L = 64
RWKV_TB = 128
RWKV_NS = 2


def _dot(a, b):
    return jnp.dot(a, b, preferred_element_type=F32)


def _dot_hi(a, b):
    return jnp.dot(a, b, preferred_element_type=F32, precision=HI)


def _dot_nt(a, b, precision=None):
    return lax.dot_general(a, b, (((1,), (1,)), ((), ())), preferred_element_type=F32, precision=precision)


def _dot_tn(a, b, precision=None):
    return lax.dot_general(a, b, (((0,), (0,)), ((), ())), preferred_element_type=F32, precision=precision)


def _sigmoid(x):
    return 1.0 / (1.0 + jnp.exp(-x))


def _silu(x):
    return x * _sigmoid(x)


def _norm_mod(x, g, sc, sh):
    ms = jnp.mean(x * x, axis=-1, keepdims=True)
    return (x * lax.rsqrt(ms + EPS) * g) * (1.0 + sc) + sh


def _params(sem):
    return pltpu.CompilerParams(dimension_semantics=sem, vmem_limit_bytes=VMEM_LIMIT)


def _cond_row(i, tm):
    r0 = i * tm
    return jnp.where(r0 < N_CTX, 0, 1 + (r0 - N_CTX) // DEC_SEQ)


def _seq_len(i, tm):
    return jnp.where(i * tm < N_CTX, SEQ, DEC_SEQ)


def _shifted(u, seq_len):
    m = u.shape[0]
    row = lax.broadcasted_iota(jnp.int32, (m, 1), 0)
    pos = row & (seq_len - 1)
    prev = jnp.where(pos == 0, 0.0, pltpu.roll(u, 1, 0))
    nxt = jnp.where(pos == seq_len - 1, 0.0, pltpu.roll(u, m - 1, 0))
    return prev, nxt


def _mod_kernel(c_ref, w_ref, b_ref, o_ref):
    s = _silu(c_ref[...]).astype(BF16)
    o_ref[...] = _dot(s, w_ref[...].astype(BF16)) + b_ref[...]


def _mod_call(cond, ada_w, ada_b):
    tn = 1536
    n = 6 * D_MODEL
    return pl.pallas_call(
        _mod_kernel,
        grid=(DEPTH, n // tn),
        in_specs=[
            pl.BlockSpec((COND_ROWS, D_MODEL), lambda l, j: (0, 0)),
            pl.BlockSpec((None, D_MODEL, tn), lambda l, j: (l, 0, j)),
            pl.BlockSpec((None, 1, tn), lambda l, j: (l, 0, j)),
        ],
        out_specs=pl.BlockSpec((None, COND_ROWS, tn), lambda l, j: (l, 0, j)),
        out_shape=jax.ShapeDtypeStruct((DEPTH, COND_ROWS, n), F32),
        compiler_params=_params(("arbitrary", "arbitrary")),
        name="ada_mod",
    )(cond, ada_w, ada_b.reshape(DEPTH, 1, n))


def _even_proj_kernel(x_ref, mod_ref, g_ref, w_ref, qg_ref, kg_ref, cos_ref, s1_ref, s2_ref, gm_ref,
                      qa_ref, ka_ref, va_ref, qb_ref, kb_ref, vb_ref):
    d = D_MODEL
    h = _norm_mod(x_ref[...], g_ref[...], mod_ref[:, d:2 * d], mod_ref[:, 0:d]).astype(BF16)
    proj = _dot(h, w_ref[...])
    cos, s1, s2, gm = cos_ref[...], s1_ref[...], s2_ref[...], gm_ref[...]
    scale = DH_A ** -0.5

    def rope(t):
        return t * cos + pltpu.roll(t, LANES - 16, 1) * s1 + pltpu.roll(t, 16, 1) * s2

    def head_norm(t, g):
        return t * lax.rsqrt(_dot_hi(t * t, gm) + EPS) * g

    for j in range(4):
        sl = slice(j * LANES, (j + 1) * LANES)
        qa_ref[:, sl] = rope(proj[:, j * LANES:(j + 1) * LANES]) * scale
        ka_ref[:, sl] = rope(proj[:, 512 + j * LANES:512 + (j + 1) * LANES])
        va_ref[:, sl] = proj[:, 1024 + j * LANES:1024 + (j + 1) * LANES]
        qb_ref[:, sl] = rope(head_norm(proj[:, 1536 + j * LANES:1536 + (j + 1) * LANES], qg_ref[...])) * scale
    kb_ref[...] = rope(head_norm(proj[:, 2048:2176], kg_ref[...]))
    vb_ref[...] = proj[:, 2176:2304]


def _even_proj_call(x, mod_l, g, w, qg, kg, cos, s1, s2, gm):
    tm = 512
    nt = N_TOK // tm
    n_rope_blk = DEC_SEQ // tm

    def rope_idx(i):
        return (jnp.where(i * tm < N_CTX, n_rope_blk, (i - N_CTX // tm) % n_rope_blk), 0)

    full = lambda shape: pl.BlockSpec(shape, lambda i: (0,) * len(shape))
    out512 = pl.BlockSpec((tm, 512), lambda i: (i, 0))
    out128 = pl.BlockSpec((tm, LANES), lambda i: (i, 0))
    return pl.pallas_call(
        _even_proj_kernel,
        grid=(nt,),
        in_specs=[
            pl.BlockSpec((tm, D_MODEL), lambda i: (i, 0)),
            pl.BlockSpec((None, 1, 6 * D_MODEL), lambda i: (_cond_row(i, tm), 0, 0)),
            full((1, D_MODEL)),
            full((D_MODEL, EVEN_PROJ)),
            full((1, LANES)),
            full((1, LANES)),
            pl.BlockSpec((tm, LANES), rope_idx),
            pl.BlockSpec((tm, LANES), rope_idx),
            pl.BlockSpec((tm, LANES), rope_idx),
            full((LANES, LANES)),
        ],
        out_specs=[out512, out512, out512, out512, out128, out128],
        out_shape=[jax.ShapeDtypeStruct((N_TOK, 512), F32)] * 4 + [jax.ShapeDtypeStruct((N_TOK, LANES), F32)] * 2,
        compiler_params=_params(("arbitrary",)),
        name="even_proj",
    )(x, mod_l, g, w, qg, kg, cos, s1, s2, gm)


def _softmax_pv(q, ks, vs):
    ss = [_dot_nt(q, k) for k in ks]
    m = functools.reduce(jnp.maximum, [jnp.max(s, axis=-1, keepdims=True) for s in ss])
    ps = [jnp.exp(s - m) for s in ss]
    l = functools.reduce(jnp.add, [jnp.sum(p, axis=-1, keepdims=True) for p in ps])
    acc = functools.reduce(jnp.add, [_dot(p.astype(BF16), v) for p, v in zip(ps, vs)])
    return acc / l


def _attn_kernel(*refs, has_cache, lam_init):
    if has_cache:
        (qa_ref, qb_ref, ka_ref, va_ref, kb_ref, vb_ref, cak_ref, cav_ref, cbk_ref, cbv_ref,
         al_ref, sg_ref, oa_ref, ob_ref) = refs
    else:
        qa_ref, qb_ref, ka_ref, va_ref, kb_ref, vb_ref, al_ref, sg_ref, oa_ref, ob_ref = refs
    al = al_ref[...]
    lam = (jnp.exp(jnp.sum(al[0:1] * al[1:2], axis=-1, keepdims=True))
           - jnp.exp(jnp.sum(al[2:3] * al[3:4], axis=-1, keepdims=True)) + lam_init)
    lo = lax.broadcasted_iota(jnp.int32, (1, LANES), 1) < DH_A

    for h in range(HA):
        sl = slice(h * LANES, (h + 1) * LANES)
        q = qa_ref[:, sl]
        ks = [ka_ref[:, sl].astype(BF16)]
        vs = [va_ref[:, sl].astype(BF16)]
        if has_cache:
            ks.insert(0, cak_ref[h].astype(BF16))
            vs.insert(0, cav_ref[h].astype(BF16))
        a1 = _softmax_pv(jnp.where(lo, q, 0.0).astype(BF16), ks, vs)
        a2 = _softmax_pv(jnp.where(lo, 0.0, q).astype(BF16), ks, vs)
        dlt = a1 - lam * a2
        ms = jnp.mean(dlt * dlt, axis=-1, keepdims=True)
        oa_ref[:, sl] = dlt * lax.rsqrt(ms + EPS) * sg_ref[...] * (1.0 - lam_init)

    ks = [kb_ref[...].astype(BF16)]
    vs = [vb_ref[...].astype(BF16)]
    if has_cache:
        ks.insert(0, cbk_ref[...].astype(BF16))
        vs.insert(0, cbv_ref[...].astype(BF16))
    for j in range(HB // 2):
        sl = slice(j * LANES, (j + 1) * LANES)
        q = qb_ref[:, sl]
        o0 = _softmax_pv(jnp.where(lo, q, 0.0).astype(BF16), ks, vs)
        o1 = _softmax_pv(jnp.where(lo, 0.0, q).astype(BF16), ks, vs)
        ob_ref[:, sl] = jnp.where(lo, o0, o1)


def _attn_call(qa, qb, ka, va, kb, vb, cache, a_lambda, subln_g, lam_init, *, n_seq, seq, row0):
    tq = 256
    nq = seq // tq
    qblk0 = row0 // tq
    sblk0 = row0 // seq
    has_cache = cache is not None
    qspec = pl.BlockSpec((tq, 512), lambda b, i: (qblk0 + b * nq + i, 0))
    own512 = pl.BlockSpec((seq, 512), lambda b, i: (sblk0 + b, 0))
    own128 = pl.BlockSpec((seq, LANES), lambda b, i: (sblk0 + b, 0))
    in_specs = [qspec, qspec, own512, own512, own128, own128]
    args = [qa, qb, ka, va, kb, vb]
    if has_cache:
        in_specs += [
            pl.BlockSpec((None, HA, PAST_LEN, LANES), lambda b, i: (b, 0, 0, 0)),
            pl.BlockSpec((None, HA, PAST_LEN, LANES), lambda b, i: (b, 0, 0, 0)),
            pl.BlockSpec((None, PAST_LEN, LANES), lambda b, i: (b, 0, 0)),
            pl.BlockSpec((None, PAST_LEN, LANES), lambda b, i: (b, 0, 0)),
        ]
        args += list(cache)
    in_specs += [pl.BlockSpec((4, DH_A), lambda b, i: (0, 0)), pl.BlockSpec((1, LANES), lambda b, i: (0, 0))]
    args += [a_lambda, subln_g]
    ospec = pl.BlockSpec((tq, 512), lambda b, i: (b * nq + i, 0))
    return pl.pallas_call(
        functools.partial(_attn_kernel, has_cache=has_cache, lam_init=lam_init),
        grid=(n_seq, nq),
        in_specs=in_specs,
        out_specs=[ospec, ospec],
        out_shape=[jax.ShapeDtypeStruct((n_seq * seq, 512), F32)] * 2,
        compiler_params=_params(("arbitrary", "arbitrary")),
        name="attn_dec" if has_cache else "attn_ctx",
    )(*args)


def _out_proj_kernel(x_ref, a_ref, b_ref, mod_ref, w_ref, o_ref):
    d = D_MODEL
    half = a_ref.shape[1]
    mix = _dot(a_ref[...].astype(BF16), w_ref[0:half, :]) + _dot(b_ref[...].astype(BF16), w_ref[half:2 * half, :])
    o_ref[...] = x_ref[...] + mod_ref[:, 2 * d:3 * d] * mix


def _out_proj_call(x, a, b, mod_l, w):
    tm = 512
    return pl.pallas_call(
        _out_proj_kernel,
        grid=(N_TOK // tm,),
        in_specs=[
            pl.BlockSpec((tm, D_MODEL), lambda i: (i, 0)),
            pl.BlockSpec((tm, 512), lambda i: (i, 0)),
            pl.BlockSpec((tm, 512), lambda i: (i, 0)),
            pl.BlockSpec((None, 1, 6 * D_MODEL), lambda i: (_cond_row(i, tm), 0, 0)),
            pl.BlockSpec((D_MODEL, D_MODEL), lambda i: (0, 0)),
        ],
        out_specs=pl.BlockSpec((tm, D_MODEL), lambda i: (i, 0)),
        out_shape=jax.ShapeDtypeStruct((N_TOK, D_MODEL), F32),
        compiler_params=_params(("arbitrary",)),
        name="out_proj",
    )(x, a, b, mod_l, w)


def _ffn_kernel(x_ref, mod_ref, g_ref, wv_ref, wg_ref, cwv_ref, cwg_ref, cbv_ref, cbg_ref, wd_ref, fg_ref,
                o_ref, h_scr, acc_scr, *, tm, final):
    d = D_MODEL
    i, j = pl.program_id(0), pl.program_id(1)

    @pl.when(j == 0)
    def _():
        h_scr[...] = _norm_mod(x_ref[...], g_ref[...], mod_ref[:, 4 * d:5 * d], mod_ref[:, 3 * d:4 * d]).astype(BF16)
        acc_scr[...] = jnp.zeros_like(acc_scr)

    seq_len = _seq_len(i, tm)
    h = h_scr[...]

    def conv(w_ref, cw_ref, cb_ref):
        u = _dot(h, w_ref[...])
        prev, nxt = _shifted(u, seq_len)
        return cw_ref[0:1, :] * prev + cw_ref[1:2, :] * u + cw_ref[2:3, :] * nxt + cb_ref[...]

    val = conv(wv_ref, cwv_ref, cbv_ref)
    gate = conv(wg_ref, cwg_ref, cbg_ref)
    act = (_silu(gate) * val).astype(BF16)
    acc_scr[...] += _dot(act, wd_ref[...])

    @pl.when(j == pl.num_programs(1) - 1)
    def _():
        y = x_ref[...] + mod_ref[:, 5 * d:6 * d] * acc_scr[...]
        if final:
            ms = jnp.mean(y * y, axis=-1, keepdims=True)
            y = y * lax.rsqrt(ms + EPS) * fg_ref[...]
        o_ref[...] = y


def _ffn_call(x, mod_l, g, w_up, conv_w, conv_b, w_down, final_g, *, final):
    tm = DEC_SEQ
    nf = D_FF // FF_TILE
    return pl.pallas_call(
        functools.partial(_ffn_kernel, tm=tm, final=final),
        grid=(N_TOK // tm, nf),
        in_specs=[
            pl.BlockSpec((tm, D_MODEL), lambda i, j: (i, 0)),
            pl.BlockSpec((None, 1, 6 * D_MODEL), lambda i, j: (_cond_row(i, tm), 0, 0)),
            pl.BlockSpec((1, D_MODEL), lambda i, j: (0, 0)),
            pl.BlockSpec((D_MODEL, FF_TILE), lambda i, j: (0, j)),
            pl.BlockSpec((D_MODEL, FF_TILE), lambda i, j: (0, nf + j)),
            pl.BlockSpec((3, FF_TILE), lambda i, j: (0, j)),
            pl.BlockSpec((3, FF_TILE), lambda i, j: (0, nf + j)),
            pl.BlockSpec((1, FF_TILE), lambda i, j: (0, j)),
            pl.BlockSpec((1, FF_TILE), lambda i, j: (0, nf + j)),
            pl.BlockSpec((FF_TILE, D_MODEL), lambda i, j: (j, 0)),
            pl.BlockSpec((1, D_MODEL), lambda i, j: (0, 0)),
        ],
        out_specs=pl.BlockSpec((tm, D_MODEL), lambda i, j: (i, 0)),
        out_shape=jax.ShapeDtypeStruct((N_TOK, D_MODEL), F32),
        scratch_shapes=[pltpu.VMEM((tm, D_MODEL), BF16), pltpu.VMEM((tm, D_MODEL), F32)],
        compiler_params=_params(("arbitrary", "arbitrary")),
        name="conv_ffn",
    )(x, mod_l, g, w_up, w_up, conv_w, conv_w, conv_b, conv_b, w_down, final_g)


def _odd_proj_kernel(x_ref, mod_ref, g_ref, w_ref, mu_ref, o_ref, h_scr, *, tm, shift):
    d = D_MODEL
    i, j = pl.program_id(0), pl.program_id(1)

    @pl.when(j == 0)
    def _():
        h_scr[...] = _norm_mod(x_ref[...], g_ref[...], mod_ref[:, d:2 * d], mod_ref[:, 0:d]).astype(BF16)

    p = _dot(h_scr[...], w_ref[...])
    if shift:
        prev, nxt = _shifted(p, _seq_len(i, tm))
        p = p + mu_ref[...] * (0.5 * (prev + nxt) - p)
    o_ref[...] = p


def _odd_proj_call(x, mod_l, g, w, mu, *, shift):
    tm = DEC_SEQ
    tn = 640
    n = w.shape[1]
    return pl.pallas_call(
        functools.partial(_odd_proj_kernel, tm=tm, shift=shift),
        grid=(N_TOK // tm, n // tn),
        in_specs=[
            pl.BlockSpec((tm, D_MODEL), lambda i, j: (i, 0)),
            pl.BlockSpec((None, 1, 6 * D_MODEL), lambda i, j: (_cond_row(i, tm), 0, 0)),
            pl.BlockSpec((1, D_MODEL), lambda i, j: (0, 0)),
            pl.BlockSpec((D_MODEL, tn), lambda i, j: (0, j)),
            pl.BlockSpec((1, tn), lambda i, j: (0, j)),
        ],
        out_specs=pl.BlockSpec((tm, tn), lambda i, j: (i, j)),
        out_shape=jax.ShapeDtypeStruct((N_TOK, n), F32),
        scratch_shapes=[pltpu.VMEM((tm, D_MODEL), BF16)],
        compiler_params=_params(("arbitrary", "arbitrary")),
        name="odd_proj_rwkv" if shift else "odd_proj_hgrn",
    )(x, mod_l, g, w, mu)


def _hgrn_kernel(q_ref, ff_ref, fb_ref, v_ref, gc_ref, lbl_ref, ng_ref, s0_ref, ones_ref, o_ref, sfin_ref,
                 oacc_f, oacc_b, *, seq, lidx):
    L = HGRN_L
    nc = seq // L
    nb = L // SUBLANES
    ones = ones_ref[...]
    rowi = lax.broadcasted_iota(jnp.int32, (L, L), 0)
    coli = lax.broadcasted_iota(jnp.int32, (L, L), 1)
    rowv = lax.broadcasted_iota(jnp.int32, (L, 1), 0)
    same_block = (rowi // SUBLANES) == (coli // SUBLANES)

    def lower_bound(drc):
        lg = lbl_ref[drc]
        e = jnp.exp(lg - jnp.max(lg, axis=0, keepdims=True))
        sm = e / jnp.sum(e, axis=0, keepdims=True)
        return functools.reduce(jnp.add, [sm[i:i + 1, :] for i in range(1, lidx + 1)])

    def spread(t, s_l):
        return jnp.concatenate(
            [jnp.broadcast_to(t[b * SUBLANES + s_l:b * SUBLANES + s_l + 1, :], (SUBLANES, LANES)) for b in range(nb)],
            axis=0)

    def chunk(rev, r0, st, f_ref, lb, tri):
        q = _silu(q_ref[pl.ds(r0, L), :])
        f = lb + (1.0 - lb) * _sigmoid(f_ref[pl.ds(r0, L), :])
        k = 1.0 - f
        lf = jnp.log(f)
        v = v_ref[pl.ds(r0, L), :].astype(BF16)
        cum = _dot_hi(tri, lf)
        cum_ex = cum - lf
        o = _dot_nt((q * jnp.exp(cum)).astype(BF16), st.astype(BF16))

        scores = jnp.zeros((L, L), F32)
        h = L // 2
        while h >= SUBLANES:
            pieces = []
            for a in range(0, L, 2 * h):
                edge = a + h - 1 if rev else a + h
                pieces.append(jnp.broadcast_to(cum_ex[edge:edge + 1, :], (2 * h, LANES)))
            anchor = jnp.concatenate(pieces, axis=0) if len(pieces) > 1 else pieces[0]
            is_q = ((rowv // h) % 2) == (0 if rev else 1)
            d = cum - anchor
            e = jnp.exp(jnp.where(is_q, d, -d))
            qe = jnp.where(is_q, q * e, 0.0).astype(BF16)
            ke = jnp.where(is_q, 0.0, k * e).astype(BF16)
            same_pair = (rowi // (2 * h)) == (coli // (2 * h))
            scores = scores + jnp.where(same_pair, _dot_nt(qe, ke), 0.0)
            h //= 2

        prods = []
        for s_l in range(SUBLANES):
            e = jnp.exp(jnp.minimum(cum - spread(cum, s_l), 0.0))
            prods.append(q * spread(k, s_l) * e)
        diag = _dot(jnp.concatenate(prods, axis=0).astype(BF16), ones)
        for s_l in range(SUBLANES):
            ok = ((rowi % SUBLANES) <= s_l) if rev else ((rowi % SUBLANES) >= s_l)
            take = same_block & ((coli % SUBLANES) == s_l) & ok
            scores = jnp.where(take, diag[s_l * L:(s_l + 1) * L, 0:L], scores)
        o = o + _dot(scores.astype(BF16), v)

        end = 0 if rev else L - 1
        cend = cum[end:end + 1, :]
        kd = (k * jnp.exp(cend - cum)).astype(BF16)
        return o, st * jnp.exp(cend) + _dot_tn(v, kd)

    lb_f, lb_b = lower_bound(0), lower_bound(1)
    tri_f = jnp.where(coli <= rowi, 1.0, 0.0).astype(F32)
    tri_b = jnp.where(coli >= rowi, 1.0, 0.0).astype(F32)

    def body(ci, carry):
        st_f, st_b = carry
        r0f = pl.multiple_of(ci * L, L)
        r0b = pl.multiple_of((nc - 1 - ci) * L, L)
        o_f, st_f = chunk(False, r0f, st_f, ff_ref, lb_f, tri_f)
        o_b, st_b = chunk(True, r0b, st_b, fb_ref, lb_b, tri_b)
        oacc_f[pl.ds(r0f, L), :] = o_f
        oacc_b[pl.ds(r0b, L), :] = o_b
        return st_f, st_b

    st_f, st_b = lax.fori_loop(0, nc, body, (s0_ref[0].T, s0_ref[1].T))
    sfin_ref[0] = st_f.T
    sfin_ref[1] = st_b.T
    o = oacc_f[...] + oacc_b[...]
    ms = jnp.mean(o * o, axis=-1, keepdims=True)
    o_ref[...] = o * lax.rsqrt(ms + EPS) * ng_ref[...] * _silu(gc_ref[...])


def _hgrn_call(proj_h, lb_logits, norm_g, s0, ones, *, n_seq, seq, row0, lidx):
    sblk0 = row0 // seq

    def col(section):
        return pl.BlockSpec((seq, LANES), lambda b, h: (sblk0 + b, section * HC + h))

    st_spec = pl.BlockSpec((None, 2, None, DK_C, DV_C), lambda b, h: (b, 0, h, 0, 0))
    return pl.pallas_call(
        functools.partial(_hgrn_kernel, seq=seq, lidx=lidx),
        grid=(n_seq, HC),
        in_specs=[
            col(0), col(1), col(2), col(3), col(4),
            pl.BlockSpec((2, DEPTH, LANES), lambda b, h: (0, 0, h)),
            pl.BlockSpec((1, LANES), lambda b, h: (0, 0)),
            st_spec,
            pl.BlockSpec((LANES, LANES), lambda b, h: (0, 0)),
        ],
        out_specs=[pl.BlockSpec((seq, LANES), lambda b, h: (b, h)), st_spec],
        out_shape=[jax.ShapeDtypeStruct((n_seq * seq, D_C), F32),
                   jax.ShapeDtypeStruct((n_seq, 2, HC, DK_C, DV_C), F32)],
        scratch_shapes=[pltpu.VMEM((seq, LANES), F32), pltpu.VMEM((seq, LANES), F32)],
        compiler_params=_params(("arbitrary", "arbitrary")),
        name="hgrn_dec" if row0 else "hgrn_ctx",
    )(proj_h, proj_h, proj_h, proj_h, proj_h, lb_logits, norm_g, s0, ones)


def _rwkv_prep_kernel(p_ref, a0_ref, aup_ref, gup_ref, kk_ref, ka_ref, rk_ref, w0_ref, wup0_ref, wup1_ref,
                      gs_ref, w0o_ref, w1o_ref, nkk_ref, bb_ref, kt_ref, g_ref, rkv_ref):
    r = p_ref[:, 0:512]
    k = p_ref[:, 512:1024]
    v = p_ref[:, 1024:1536]
    wd = p_ref[:, 1536:1664]
    gd = p_ref[:, 1664:1792]
    ad = p_ref[:, 1792:1920]
    gs = gs_ref[...]

    def group_sum(t):
        return jnp.concatenate([_dot_hi(t[:, j * LANES:(j + 1) * LANES], gs) for j in range(4)], axis=1)

    a = _sigmoid(a0_ref[...] + _dot_hi(ad, aup_ref[...]))
    g_ref[...] = _dot_hi(_sigmoid(gd), gup_ref[...])
    kkr = k * kk_ref[...]
    kk = kkr / jnp.maximum(jnp.sqrt(group_sum(kkr * kkr)), 1e-12)
    kt = k * (1.0 + (a - 1.0) * ka_ref[...])
    th = jnp.tanh(wd)
    decay = math.exp(-0.5)
    w0o_ref[...] = jnp.exp(-decay * _sigmoid(w0_ref[0:1, :] + _dot_hi(th, wup0_ref[...])))
    w1o_ref[...] = jnp.exp(-decay * _sigmoid(w0_ref[1:2, :] + _dot_hi(th, wup1_ref[...])))
    nkk_ref[...] = -kk
    bb_ref[...] = kk * a
    kt_ref[...] = kt
    rkv_ref[...] = group_sum(r * kt * rk_ref[...]) * v


def _rwkv_prep_call(proj_r, a0, aup, gup, kk_k, k_a, r_k, w0, wup0, wup1, gs):
    tm = 512
    full = lambda shape: pl.BlockSpec(shape, lambda i: (0,) * len(shape))
    ospec = pl.BlockSpec((tm, D_D), lambda i: (i, 0))
    return pl.pallas_call(
        _rwkv_prep_kernel,
        grid=(N_TOK // tm,),
        in_specs=[
            pl.BlockSpec((tm, RWKV_PAD), lambda i: (i, 0)),
            full((1, D_D)), full((LANES, D_D)), full((LANES, D_D)), full((1, D_D)), full((1, D_D)), full((1, D_D)),
            full((2, D_D)), full((LANES, D_D)), full((LANES, D_D)), full((LANES, LANES)),
        ],
        out_specs=[ospec] * 7,
        out_shape=[jax.ShapeDtypeStruct((N_TOK, D_D), F32)] * 7,
        compiler_params=_params(("arbitrary",)),
        name="rwkv_prep",
    )(proj_r, a0, aup, gup, kk_k, k_a, r_k, w0, wup0, wup1, gs)


def _rwkv_scan_kernel(rf_ref, wf_ref, nf_ref, bf_ref, kf_ref, vf_ref, rb_ref, wb_ref, nb_ref, bb_ref, kb_ref,
                      vb_ref, s0_ref, qq_ref, qo_ref, of_ref, ob_ref, sfin_ref, st, ot):
    tb = pl.program_id(1)
    ns = RWKV_NS
    ngrp = D_D // LANES
    steps = RWKV_TB
    chains = [(s, drc, g) for s in range(ns) for drc in range(2) for g in range(ngrp)]
    dirs = ((rf_ref, wf_ref, nf_ref, bf_ref, kf_ref, vf_ref), (rb_ref, wb_ref, nb_ref, bb_ref, kb_ref, vb_ref))
    o_refs = (of_ref, ob_ref)

    def rows(c, n=DH_D):
        return slice(c * n, (c + 1) * n)

    @pl.when(tb == 0)
    def _():
        for c, (s, drc, g) in enumerate(chains):
            st[rows(c), :] = s0_ref[s, drc, g]

    ot[...] = jnp.zeros_like(ot)
    lane = lax.broadcasted_iota(jnp.int32, (1, LANES), 1)
    diag = (lax.broadcasted_iota(jnp.int32, (DH_D, LANES), 1) & (DH_D - 1)) == lax.broadcasted_iota(
        jnp.int32, (DH_D, LANES), 0)
    qq = qq_ref[...]
    q1 = qq_ref[0:LANES, :]
    qo = qo_ref[...]

    def step_group(t8, carry):
        bases = (pl.multiple_of(t8 * SUBLANES, SUBLANES), pl.multiple_of(steps - (t8 + 1) * SUBLANES, SUBLANES))
        blk = [[[ref[s, pl.ds(bases[drc], SUBLANES), :] for ref in dirs[drc]] for drc in range(2)] for s in range(ns)]
        for j in range(SUBLANES):
            locs = (j, SUBLANES - 1 - j)
            states, prods, vdiag = [], [], []
            for c, (s, drc, g) in enumerate(chains):
                loc, sl = locs[drc], slice(g * LANES, (g + 1) * LANES)
                sv = st[rows(c), :]
                states.append(sv)
                prods.append(sv * blk[s][drc][2][loc:loc + 1, sl])
                vdiag.append(jnp.where(diag, blk[s][drc][5][loc:loc + 1, sl], 0.0))
            p = jnp.concatenate(prods, axis=0)
            p_hi = p.astype(BF16)
            p_mid = (p - p_hi.astype(F32)).astype(BF16)
            sa_all = _dot(jnp.concatenate([p_hi, p_mid], axis=1), qq)
            vc_all = _dot(jnp.concatenate(vdiag, axis=0).astype(BF16), q1)
            outs = []
            for c, (s, drc, g) in enumerate(chains):
                loc, sl = locs[drc], slice(g * LANES, (g + 1) * LANES)
                r_b, w_b, _, b_b, k_b, _ = blk[s][drc]
                sv = (states[c] * w_b[loc:loc + 1, sl] + sa_all[rows(c)] * b_b[loc:loc + 1, sl]
                      + vc_all[rows(c)] * k_b[loc:loc + 1, sl])
                st[rows(c), :] = sv
                outs.append(sv * r_b[loc:loc + 1, sl])
            o_all = _dot(jnp.concatenate(outs, axis=0).astype(BF16), qo)
            for c, (s, drc, g) in enumerate(chains):
                here = lane == bases[drc] + locs[drc]
                top, bot = slice(c * LANES, c * LANES + DH_D), slice(c * LANES + DH_D, (c + 1) * LANES)
                ot[top, :] = jnp.where(here, o_all[rows(c), 0:LANES], ot[top, :])
                ot[bot, :] = jnp.where(here, o_all[rows(c), LANES:2 * LANES], ot[bot, :])
        return carry

    lax.fori_loop(0, steps // SUBLANES, step_group, 0)

    for c, (s, drc, g) in enumerate(chains):
        o_refs[drc][s, :, g * LANES:(g + 1) * LANES] = ot[rows(c, LANES), :].T

    @pl.when(tb == pl.num_programs(1) - 1)
    def _():
        for c, (s, drc, g) in enumerate(chains):
            sfin_ref[s, drc, g] = st[rows(c), :]


def _rwkv_scan_call(r_src, w0, w1, nkk, bb, kt, s0, qq, qo, *, n_seq, seq, row0):
    ns = RWKV_NS
    ntb = seq // RWKV_TB
    ngrp = D_D // LANES
    grp0 = row0 // (seq * ns)

    def view(t):
        return t.reshape(N_TOK // (seq * ns), ns, ntb, RWKV_TB, t.shape[-1])

    def tok(rev, cb=0):
        if rev:
            return pl.BlockSpec((None, ns, None, RWKV_TB, D_D), lambda b, t: (grp0 + b, 0, ntb - 1 - t, 0, cb))
        return pl.BlockSpec((None, ns, None, RWKV_TB, D_D), lambda b, t: (grp0 + b, 0, t, 0, cb))

    def out(rev):
        if rev:
            return pl.BlockSpec((None, ns, None, RWKV_TB, D_D), lambda b, t: (b, 0, ntb - 1 - t, 0, 0))
        return pl.BlockSpec((None, ns, None, RWKV_TB, D_D), lambda b, t: (b, 0, t, 0, 0))

    st_spec = pl.BlockSpec((ns, 2, ngrp, DH_D, LANES), lambda b, t: (b, 0, 0, 0, 0))
    in_specs = []
    for rev in (False, True):
        in_specs += [tok(rev, 0), tok(rev), tok(rev), tok(rev), tok(rev), tok(rev, 2)]
    in_specs += [st_spec, pl.BlockSpec((2 * LANES, LANES), lambda b, t: (0, 0)),
                 pl.BlockSpec((LANES, 2 * LANES), lambda b, t: (0, 0))]
    rv, w0v, w1v, nv, bv, kv = [view(t) for t in (r_src, w0, w1, nkk, bb, kt)]
    o_shape = jax.ShapeDtypeStruct((n_seq // ns, ns, ntb, RWKV_TB, D_D), F32)
    o_f, o_b, s_fin = pl.pallas_call(
        _rwkv_scan_kernel,
        grid=(n_seq // ns, ntb),
        in_specs=in_specs,
        out_specs=[out(False), out(True), st_spec],
        out_shape=[o_shape, o_shape, jax.ShapeDtypeStruct((n_seq, 2, ngrp, DH_D, LANES), F32)],
        scratch_shapes=[pltpu.VMEM((ns * 2 * ngrp * DH_D, LANES), F32), pltpu.VMEM((ns * 2 * ngrp * LANES, LANES), F32)],
        compiler_params=_params(("arbitrary", "arbitrary")),
        name="rwkv_dec" if row0 else "rwkv_ctx",
    )(rv, w0v, nv, bv, kv, rv, rv, w1v, nv, bv, kv, rv, s0, qq, qo)
    return o_f.reshape(n_seq * seq, D_D), o_b.reshape(n_seq * seq, D_D), s_fin


def _rwkv_post_kernel(of_ref, ob_ref, rkv_ref, g_ref, lng_ref, lnb_ref, gm_ref, o_ref):
    gm = gm_ref[...]
    for j in range(D_D // LANES):
        sl = slice(j * LANES, (j + 1) * LANES)
        o = of_ref[:, sl] + ob_ref[:, sl]
        dlt = o - _dot_hi(o, gm)
        var = _dot_hi(dlt * dlt, gm)
        y = dlt * lax.rsqrt(var + RWKV_GN_EPS) * lng_ref[:, sl] + lnb_ref[:, sl]
        o_ref[:, sl] = (y + rkv_ref[:, sl]) * g_ref[:, sl]


def _rwkv_post_call(o_f, o_b, rkv, g, ln_g, ln_b, gm):
    tm = 512
    tok = pl.BlockSpec((tm, D_D), lambda i: (i, 0))
    row = pl.BlockSpec((1, D_D), lambda i: (0, 0))
    return pl.pallas_call(
        _rwkv_post_kernel,
        grid=(N_TOK // tm,),
        in_specs=[tok, tok, tok, tok, row, row, pl.BlockSpec((LANES, LANES), lambda i: (0, 0))],
        out_specs=tok,
        out_shape=jax.ShapeDtypeStruct((N_TOK, D_D), F32),
        compiler_params=_params(("arbitrary",)),
        name="rwkv_post",
    )(o_f, o_b, rkv, g, ln_g, ln_b, gm)


def _rope_tables():
    pos = np.arange(DEC_SEQ)
    pr, pc = pos // GRID_W, pos % GRID_W
    lane = np.arange(LANES)
    dd = lane % DH_A
    use_col = (dd // 32) == 1
    j = dd % 16
    is_lo = (dd % 32) < 16
    freq = ROPE_THETA ** (-(j.astype(np.float64)) / 16.0)
    p = np.where(use_col[None, :], pc[:, None], pr[:, None]).astype(np.float64)
    ang = (p.astype(np.float32) * freq.astype(np.float32)[None, :]).astype(np.float32)
    cos = np.cos(ang).astype(np.float32)
    sin = np.sin(ang).astype(np.float32)
    s1 = np.where(is_lo[None, :], -sin, 0.0).astype(np.float32)
    s2 = np.where(is_lo[None, :], 0.0, sin).astype(np.float32)
    ident = 512
    cos = np.concatenate([cos, np.ones((ident, LANES), np.float32)], 0)
    s1 = np.concatenate([s1, np.zeros((ident, LANES), np.float32)], 0)
    s2 = np.concatenate([s2, np.zeros((ident, LANES), np.float32)], 0)
    return jnp.asarray(cos), jnp.asarray(s1), jnp.asarray(s2)


def _block_diag(value):
    m = np.zeros((LANES, LANES), np.float32)
    half = LANES // 2
    m[:half, :half] = value
    m[half:, half:] = value
    return jnp.asarray(m)


def _head_spread():
    m = np.zeros((LANES, 2 * LANES), np.float32)
    m[:LANES // 2, :LANES] = 1.0
    m[LANES // 2:, LANES:] = 1.0
    return jnp.asarray(m)


def _qb_perm():
    idx = np.zeros(D_B, np.int32)
    for j in range(HB // 2):
        for hh in range(2):
            for dch in range(DH_B):
                idx[j * LANES + hh * DH_B + dch] = (hh * (HB // 2) + j) * DH_B + dch
    return idx


def kernel(x_prompt, x_sample, cache_a_k, cache_a_v, cache_b_k, cache_b_v, state_hgrn, state_rwkv, c, c_ctx, ada_w, ada_b, norm_mix_g, norm_ffn_g, final_norm_g, ev_w_in, ev_w_out, a_lambda, a_subln_g, b_q_norm_g, b_k_norm_g, od_w_in, od_w_out, hgrn_lb_logits, hgrn_norm_g, rwkv_mu, rwkv_w0, rwkv_w_up, rwkv_a0, rwkv_a_up, rwkv_g_up, rwkv_k_k, rwkv_k_a, rwkv_r_k, rwkv_ln_g, rwkv_ln_b, ffn_w_up, ffn_conv_w, ffn_conv_b, ffn_w_down):
    d = D_MODEL
    x = jnp.concatenate([x_prompt.reshape(N_CTX, d), x_sample.reshape(N_DEC, d)], axis=0)
    cond = jnp.concatenate([c_ctx[None, :], c, jnp.zeros((COND_ROWS - 1 - DEC_BATCH, d), F32)], axis=0)
    mod = _mod_call(cond, ada_w, ada_b).reshape(DEPTH, COND_ROWS, 1, 6 * d)

    cos, s1, s2 = _rope_tables()
    g_mean = _block_diag(1.0 / DH_B)
    g_sum = _block_diag(1.0)
    perm = _qb_perm()
    row = lambda t: t.reshape(1, -1)

    new_ctx = None
    for l in range(DEPTH):
        i = l // 2
        mod_l = mod[l]
        if l % 2 == 0:
            w_in = ev_w_in[i]
            w_in = jnp.concatenate([w_in[:, :1536], w_in[:, 1536:2048][:, perm], w_in[:, 2048:]], axis=1).astype(BF16)
            w_out = ev_w_out[i]
            w_out = jnp.concatenate([w_out[:D_A], w_out[D_A:][perm]], axis=0).astype(BF16)
            qg = row(jnp.tile(b_q_norm_g[i], 2))
            kg = row(jnp.tile(b_k_norm_g[i], 2))
            qa, ka, va, qb, kb, vb = _even_proj_call(x, mod_l, row(norm_mix_g[l]), w_in, qg, kg, cos, s1, s2, g_mean)
            lam_init = 0.8 - 0.6 * math.exp(-0.3 * l)
            sub_g = row(a_subln_g[i])
            oa_c, ob_c = _attn_call(qa, qb, ka, va, kb, vb, None, a_lambda[i], sub_g, lam_init,
                                    n_seq=BATCH, seq=SEQ, row0=0)
            cbk = jnp.transpose(cache_b_k[:, i], (0, 2, 1, 3)).reshape(DEC_BATCH, PAST_LEN, HKV_B * DH_B)
            cbv = jnp.transpose(cache_b_v[:, i], (0, 2, 1, 3)).reshape(DEC_BATCH, PAST_LEN, HKV_B * DH_B)
            cache = (cache_a_k[:, i], cache_a_v[:, i], cbk, cbv)
            oa_s, ob_s = _attn_call(qa, qb, ka, va, kb, vb, cache, a_lambda[i], sub_g, lam_init,
                                    n_seq=DEC_BATCH, seq=DEC_SEQ, row0=N_CTX)
            mix_a = jnp.concatenate([oa_c, oa_s], axis=0)
            mix_b = jnp.concatenate([ob_c, ob_s], axis=0)

            def heads_first(t, nh):
                t = t[:N_CTX].reshape(BATCH, SEQ, nh, -1)
                return jnp.transpose(t, (0, 2, 1, 3))[:, None]

            even_ctx = (heads_first(ka, HA), heads_first(va, HA), heads_first(kb, HKV_B), heads_first(vb, HKV_B))
        else:
            w = od_w_in[i]
            w_h = w[:, :HGRN_PROJ].astype(BF16)
            wr = w[:, HGRN_PROJ:]
            mu = rwkv_mu[i]

            def rwkv_cols(t):
                z = jnp.zeros(t.shape[:-1] + (RWKV_PAD - RWKV_PROJ,), t.dtype)
                return jnp.concatenate([t[..., :1664], t[..., 1728:1856], t[..., 1664:1728], z], axis=-1)

            w_r = rwkv_cols(wr).astype(BF16)
            mu_r = row(rwkv_cols(mu))
            g_l = row(norm_mix_g[l])
            proj_h = _odd_proj_call(x, mod_l, g_l, w_h, jnp.zeros((1, HGRN_PROJ), F32), shift=False)
            proj_r = _odd_proj_call(x, mod_l, g_l, w_r, mu_r, shift=True)

            ng = row(hgrn_norm_g[i])
            zero_h = jnp.zeros((BATCH, 2, HC, DK_C, DV_C), F32)
            ones = jnp.ones((LANES, LANES), BF16)
            oc_c, sh_c = _hgrn_call(proj_h, hgrn_lb_logits, ng, zero_h, ones, n_seq=BATCH, seq=SEQ, row0=0, lidx=l)
            oc_s, _ = _hgrn_call(proj_h, hgrn_lb_logits, ng, state_hgrn[:, i], ones, n_seq=DEC_BATCH, seq=DEC_SEQ,
                                 row0=N_CTX, lidx=l)

            pad_rows = lambda t: jnp.concatenate([t, jnp.zeros((LANES - t.shape[0], t.shape[1]), F32)], axis=0)
            wup0 = pad_rows(rwkv_w_up[i, 0])
            wup1 = jnp.concatenate([jnp.zeros((W_LORA, D_D), F32), rwkv_w_up[i, 1]], axis=0)
            w0o, w1o, nkk, bb, kt, gg, rkv = _rwkv_prep_call(
                proj_r, row(rwkv_a0[i]), pad_rows(rwkv_a_up[i]), rwkv_g_up[i], row(rwkv_k_k[i]), row(rwkv_k_a[i]),
                row(rwkv_r_k[i]), rwkv_w0[i], wup0, wup1, g_sum)

            def to_tiles(s):
                b = s.shape[0]
                s = s.reshape(b, 2, HD // 2, 2, DH_D, DH_D)
                return jnp.transpose(s, (0, 1, 2, 4, 3, 5)).reshape(b, 2, HD // 2, DH_D, LANES)

            def from_tiles(s):
                b = s.shape[0]
                s = s.reshape(b, 2, HD // 2, DH_D, 2, DH_D)
                return jnp.transpose(s, (0, 1, 2, 4, 3, 5)).reshape(b, 2, HD, DH_D, DH_D)

            zero_r = jnp.zeros((BATCH, 2, HD // 2, DH_D, LANES), F32)
            qq = jnp.concatenate([g_sum, g_sum], axis=0).astype(BF16)
            qo = _head_spread().astype(BF16)
            of_c, ob_c, sr_c = _rwkv_scan_call(proj_r, w0o, w1o, nkk, bb, kt, zero_r, qq, qo,
                                               n_seq=BATCH, seq=SEQ, row0=0)
            of_s, ob_s, _ = _rwkv_scan_call(proj_r, w0o, w1o, nkk, bb, kt, to_tiles(state_rwkv[:, i]), qq, qo,
                                            n_seq=DEC_BATCH, seq=DEC_SEQ, row0=N_CTX)
            o_f = jnp.concatenate([of_c, of_s], axis=0)
            o_b = jnp.concatenate([ob_c, ob_s], axis=0)
            mix_b = _rwkv_post_call(o_f, o_b, rkv, gg, row(rwkv_ln_g[i]), row(rwkv_ln_b[i]), g_mean)
            mix_a = jnp.concatenate([oc_c, oc_s], axis=0)
            w_out = od_w_out[i].astype(BF16)
            odd_ctx = (sh_c[:, None], from_tiles(sr_c)[:, None])

        x = _out_proj_call(x, mix_a, mix_b, mod_l, w_out)
        x = _ffn_call(x, mod_l, row(norm_ffn_g[l]), ffn_w_up[l].astype(BF16), ffn_conv_w[l], row(ffn_conv_b[l]),
                      ffn_w_down[l].astype(BF16), row(final_norm_g), final=(l == DEPTH - 1))

    y_prompt = x[:N_CTX].reshape(BATCH, SEQ, d)
    y_sample = x[N_CTX:].reshape(DEC_BATCH, DEC_SEQ, d)
    return (y_prompt, y_sample) + even_ctx + odd_ctx
```

```python
import functools
import math

import jax
import jax.numpy as jnp
import numpy as np
from jax import lax
from jax.experimental import pallas as pl
from jax.experimental.pallas import tpu as pltpu

D_MODEL = 1024
BATCH = 16
SEQ = 256
DEPTH = 2
DEC_BATCH = 4
DEC_SEQ = 1024
PAST_LEN = 512
GRID_W = 64
ROPE_THETA = 10000.0
EPS = 1e-6
RWKV_GN_EPS = 64e-5
HA = 4
DH_A = 64
DV_A = 2 * DH_A
HB = 8
HKV_B = 2
DH_B = 64
HC = 4
DK_C = 128
DV_C = 128
HD = 8
DH_D = 64
W_LORA = 64
A_LORA = 64
G_LORA = 128
D_FF = 2816

D_A = HA * DV_A
D_B = HB * DH_B
D_C = HC * DV_C
D_D = HD * DH_D
EVEN_PROJ = 2304
HGRN_PROJ = 2560
RWKV_PROJ = 1856
RWKV_PAD = 1920

N_CTX = BATCH * SEQ
N_DEC = DEC_BATCH * DEC_SEQ
N_TOK = N_CTX + N_DEC
COND_ROWS = 8

LANES = 128
SUBLANES = 8
VMEM_LIMIT = 56 * 1024 * 1024

F32 = jnp.float32
BF16 = jnp.bfloat16
HI = lax.Precision.HIGHEST

FF_TILE = 256
HGRN_L = 64
HGRN_HP = 2
RWKV_TB = 128
RWKV_NS = 2


def _dot(a, b):
    return jnp.dot(a, b, preferred_element_type=F32)


def _dot_hi(a, b):
    return jnp.dot(a, b, preferred_element_type=F32, precision=HI)


def _dot_nt(a, b, precision=None):
    return lax.dot_general(a, b, (((1,), (1,)), ((), ())), preferred_element_type=F32, precision=precision)


def _dot_tn(a, b, precision=None):
    return lax.dot_general(a, b, (((0,), (0,)), ((), ())), preferred_element_type=F32, precision=precision)


def _sigmoid(x):
    return 1.0 / (1.0 + jnp.exp(-x))


def _silu(x):
    return x * _sigmoid(x)


def _norm_mod(x, g, sc, sh):
    ms = jnp.mean(x * x, axis=-1, keepdims=True)
    return (x * lax.rsqrt(ms + EPS) * g) * (1.0 + sc) + sh


def _params(sem):
    return pltpu.CompilerParams(dimension_semantics=sem, vmem_limit_bytes=VMEM_LIMIT)


def _cond_row(i, tm):
    r0 = i * tm
    return jnp.where(r0 < N_CTX, 0, 1 + (r0 - N_CTX) // DEC_SEQ)


def _seq_len(i, tm):
    return jnp.where(i * tm < N_CTX, SEQ, DEC_SEQ)


def _shifted(u, seq_len):
    m = u.shape[0]
    row = lax.broadcasted_iota(jnp.int32, (m, 1), 0)
    pos = row & (seq_len - 1)
    prev = jnp.where(pos == 0, 0.0, pltpu.roll(u, 1, 0))
    nxt = jnp.where(pos == seq_len - 1, 0.0, pltpu.roll(u, m - 1, 0))
    return prev, nxt


def _pick(tm, lo_ref, hi_ref):
    return jnp.where(pl.program_id(0) * tm < N_CTX, lo_ref[...], hi_ref[...])


def _pair_specs(tm, width, pair):
    nct = N_CTX // tm
    hi0 = pair[2] // tm
    return [pl.BlockSpec((tm, width), lambda i: (jnp.minimum(i, nct - 1), 0)),
            pl.BlockSpec((tm, width), lambda i: (hi0 + jnp.maximum(i - nct, 0), 0))]


def _whole(t):
    return (t, t, N_CTX)


def _mod_kernel(c_ref, w_ref, b_ref, o_ref):
    s = _silu(c_ref[...]).astype(BF16)
    o_ref[...] = _dot(s, w_ref[...].astype(BF16)) + b_ref[...]


def _mod_call(cond, ada_w, ada_b):
    tn = 1536
    n = 6 * D_MODEL
    return pl.pallas_call(
        _mod_kernel,
        grid=(DEPTH, n // tn),
        in_specs=[
            pl.BlockSpec((COND_ROWS, D_MODEL), lambda l, j: (0, 0)),
            pl.BlockSpec((None, D_MODEL, tn), lambda l, j: (l, 0, j)),
            pl.BlockSpec((None, 1, tn), lambda l, j: (l, 0, j)),
        ],
        out_specs=pl.BlockSpec((None, COND_ROWS, tn), lambda l, j: (l, 0, j)),
        out_shape=jax.ShapeDtypeStruct((DEPTH, COND_ROWS, n), F32),
        compiler_params=_params(("arbitrary", "arbitrary")),
        name="ada_mod",
    )(cond, ada_w, ada_b.reshape(DEPTH, 1, n))


def _even_proj_kernel(xl_ref, xh_ref, mod_ref, g_ref, w_ref, qg_ref, kg_ref, cos_ref, s1_ref, s2_ref, gm_ref,
                      qa_ref, ka_ref, va_ref, qb_ref, kb_ref, vb_ref, *, tm):
    d = D_MODEL
    h = _norm_mod(_pick(tm, xl_ref, xh_ref), g_ref[...], mod_ref[:, d:2 * d], mod_ref[:, 0:d]).astype(BF16)
    proj = _dot(h, w_ref[...])
    cos, s1, s2, gm = cos_ref[...], s1_ref[...], s2_ref[...], gm_ref[...]
    scale = DH_A ** -0.5

    def rope(t):
        return t * cos + pltpu.roll(t, LANES - 16, 1) * s1 + pltpu.roll(t, 16, 1) * s2

    def head_norm(t, g):
        return t * lax.rsqrt(_dot_hi(t * t, gm) + EPS) * g

    for j in range(4):
        sl = slice(j * LANES, (j + 1) * LANES)
        qa_ref[:, sl] = rope(proj[:, j * LANES:(j + 1) * LANES]) * scale
        ka_ref[:, sl] = rope(proj[:, 512 + j * LANES:512 + (j + 1) * LANES])
        va_ref[:, sl] = proj[:, 1024 + j * LANES:1024 + (j + 1) * LANES]
        qb_ref[:, sl] = rope(head_norm(proj[:, 1536 + j * LANES:1536 + (j + 1) * LANES], qg_ref[...])) * scale
    kb_ref[...] = rope(head_norm(proj[:, 2048:2176], kg_ref[...]))
    vb_ref[...] = proj[:, 2176:2304]


def _even_proj_call(x, mod_l, g, w, qg, kg, cos, s1, s2, gm):
    tm = 512
    nt = N_TOK // tm
    n_rope_blk = DEC_SEQ // tm

    def rope_idx(i):
        return (jnp.where(i * tm < N_CTX, n_rope_blk, (i - N_CTX // tm) % n_rope_blk), 0)

    full = lambda shape: pl.BlockSpec(shape, lambda i: (0,) * len(shape))
    out512 = pl.BlockSpec((tm, 512), lambda i: (i, 0))
    out128 = pl.BlockSpec((tm, LANES), lambda i: (i, 0))
    return pl.pallas_call(
        functools.partial(_even_proj_kernel, tm=tm),
        grid=(nt,),
        in_specs=_pair_specs(tm, D_MODEL, x) + [
            pl.BlockSpec((None, 1, 6 * D_MODEL), lambda i: (_cond_row(i, tm), 0, 0)),
            full((1, D_MODEL)),
            full((D_MODEL, EVEN_PROJ)),
            full((1, LANES)),
            full((1, LANES)),
            pl.BlockSpec((tm, LANES), rope_idx),
            pl.BlockSpec((tm, LANES), rope_idx),
            pl.BlockSpec((tm, LANES), rope_idx),
            full((LANES, LANES)),
        ],
        out_specs=[out512, out512, out512, out512, out128, out128],
        out_shape=[jax.ShapeDtypeStruct((N_TOK, 512), F32)] * 4 + [jax.ShapeDtypeStruct((N_TOK, LANES), F32)] * 2,
        compiler_params=_params(("arbitrary",)),
        name="even_proj",
    )(x[0], x[1], mod_l, g, w, qg, kg, cos, s1, s2, gm)


def _softmax_pv(q, ks, vs):
    ss = [_dot_nt(q, k) for k in ks]
    m = functools.reduce(jnp.maximum, [jnp.max(s, axis=-1, keepdims=True) for s in ss])
    ps = [jnp.exp(s - m) for s in ss]
    l = functools.reduce(jnp.add, [jnp.sum(p, axis=-1, keepdims=True) for p in ps])
    acc = functools.reduce(jnp.add, [_dot(p.astype(BF16), v) for p, v in zip(ps, vs)])
    return acc / l


def _attn_kernel(*refs, has_cache, lam_init):
    if has_cache:
        (qa_ref, qb_ref, ka_ref, va_ref, kb_ref, vb_ref, cak_ref, cav_ref, cbk_ref, cbv_ref,
         al_ref, sg_ref, oa_ref, ob_ref) = refs
    else:
        qa_ref, qb_ref, ka_ref, va_ref, kb_ref, vb_ref, al_ref, sg_ref, oa_ref, ob_ref = refs
    al = al_ref[...]
    lam = (jnp.exp(jnp.sum(al[0:1] * al[1:2], axis=-1, keepdims=True))
           - jnp.exp(jnp.sum(al[2:3] * al[3:4], axis=-1, keepdims=True)) + lam_init)
    lo = lax.broadcasted_iota(jnp.int32, (1, LANES), 1) < DH_A

    for h in range(HA):
        sl = slice(h * LANES, (h + 1) * LANES)
        q = qa_ref[:, sl]
        ks = [ka_ref[:, sl].astype(BF16)]
        vs = [va_ref[:, sl].astype(BF16)]
        if has_cache:
            ks.insert(0, cak_ref[h].astype(BF16))
            vs.insert(0, cav_ref[h].astype(BF16))
        a1 = _softmax_pv(jnp.where(lo, q, 0.0).astype(BF16), ks, vs)
        a2 = _softmax_pv(jnp.where(lo, 0.0, q).astype(BF16), ks, vs)
        dlt = a1 - lam * a2
        ms = jnp.mean(dlt * dlt, axis=-1, keepdims=True)
        oa_ref[:, sl] = dlt * lax.rsqrt(ms + EPS) * sg_ref[...] * (1.0 - lam_init)

    ks = [kb_ref[...].astype(BF16)]
    vs = [vb_ref[...].astype(BF16)]
    if has_cache:
        ks.insert(0, cbk_ref[...].astype(BF16))
        vs.insert(0, cbv_ref[...].astype(BF16))
    for j in range(HB // 2):
        sl = slice(j * LANES, (j + 1) * LANES)
        q = qb_ref[:, sl]
        o0 = _softmax_pv(jnp.where(lo, q, 0.0).astype(BF16), ks, vs)
        o1 = _softmax_pv(jnp.where(lo, 0.0, q).astype(BF16), ks, vs)
        ob_ref[:, sl] = jnp.where(lo, o0, o1)


def _attn_call(qa, qb, ka, va, kb, vb, cache, a_lambda, subln_g, lam_init, *, n_seq, seq, row0):
    tq = 256
    nq = seq // tq
    qblk0 = row0 // tq
    sblk0 = row0 // seq
    has_cache = cache is not None
    qspec = pl.BlockSpec((tq, 512), lambda b, i: (qblk0 + b * nq + i, 0))
    own512 = pl.BlockSpec((seq, 512), lambda b, i: (sblk0 + b, 0))
    own128 = pl.BlockSpec((seq, LANES), lambda b, i: (sblk0 + b, 0))
    in_specs = [qspec, qspec, own512, own512, own128, own128]
    args = [qa, qb, ka, va, kb, vb]
    if has_cache:
        in_specs += [
            pl.BlockSpec((None, HA, PAST_LEN, LANES), lambda b, i: (b, 0, 0, 0)),
            pl.BlockSpec((None, HA, PAST_LEN, LANES), lambda b, i: (b, 0, 0, 0)),
            pl.BlockSpec((None, PAST_LEN, LANES), lambda b, i: (b, 0, 0)),
            pl.BlockSpec((None, PAST_LEN, LANES), lambda b, i: (b, 0, 0)),
        ]
        args += list(cache)
    in_specs += [pl.BlockSpec((4, DH_A), lambda b, i: (0, 0)), pl.BlockSpec((1, LANES), lambda b, i: (0, 0))]
    args += [a_lambda, subln_g]
    ospec = pl.BlockSpec((tq, 512), lambda b, i: (b * nq + i, 0))
    return pl.pallas_call(
        functools.partial(_attn_kernel, has_cache=has_cache, lam_init=lam_init),
        grid=(n_seq, nq),
        in_specs=in_specs,
        out_specs=[ospec, ospec],
        out_shape=[jax.ShapeDtypeStruct((n_seq * seq, 512), F32)] * 2,
        compiler_params=_params(("arbitrary", "arbitrary")),
        name="attn_dec" if has_cache else "attn_ctx",
    )(*args)


def _out_proj_kernel(xl_ref, xh_ref, al_ref, ah_ref, bl_ref, bh_ref, mod_ref, w_ref, o_ref, *, tm):
    d = D_MODEL
    half = al_ref.shape[1]
    a = _pick(tm, al_ref, ah_ref).astype(BF16)
    b = _pick(tm, bl_ref, bh_ref).astype(BF16)
    mix = _dot(a, w_ref[0:half, :]) + _dot(b, w_ref[half:2 * half, :])
    o_ref[...] = _pick(tm, xl_ref, xh_ref) + mod_ref[:, 2 * d:3 * d] * mix


def _out_proj_call(x, a, b, mod_l, w):
    tm = 512
    return pl.pallas_call(
        functools.partial(_out_proj_kernel, tm=tm),
        grid=(N_TOK // tm,),
        in_specs=_pair_specs(tm, D_MODEL, x) + _pair_specs(tm, 512, a) + _pair_specs(tm, 512, b) + [
            pl.BlockSpec((None, 1, 6 * D_MODEL), lambda i: (_cond_row(i, tm), 0, 0)),
            pl.BlockSpec((D_MODEL, D_MODEL), lambda i: (0, 0)),
        ],
        out_specs=pl.BlockSpec((tm, D_MODEL), lambda i: (i, 0)),
        out_shape=jax.ShapeDtypeStruct((N_TOK, D_MODEL), F32),
        compiler_params=_params(("arbitrary",)),
        name="out_proj",
    )(x[0], x[1], a[0], a[1], b[0], b[1], mod_l, w)


def _ffn_kernel(x_ref, mod_ref, g_ref, wv_ref, wg_ref, cwv_ref, cwg_ref, cbv_ref, cbg_ref, wd_ref, fg_ref,
                o_ref, h_scr, acc_scr, *, tm, final):
    d = D_MODEL
    i, j = pl.program_id(0), pl.program_id(1)

    @pl.when(j == 0)
    def _():
        h_scr[...] = _norm_mod(x_ref[...], g_ref[...], mod_ref[:, 4 * d:5 * d], mod_ref[:, 3 * d:4 * d]).astype(BF16)
        acc_scr[...] = jnp.zeros_like(acc_scr)

    seq_len = _seq_len(i, tm)
    h = h_scr[...]

    def conv(w_ref, cw_ref, cb_ref):
        u = _dot(h, w_ref[...])
        prev, nxt = _shifted(u, seq_len)
        return cw_ref[0:1, :] * prev + cw_ref[1:2, :] * u + cw_ref[2:3, :] * nxt + cb_ref[...]

    val = conv(wv_ref, cwv_ref, cbv_ref)
    gate = conv(wg_ref, cwg_ref, cbg_ref)
    act = (_silu(gate) * val).astype(BF16)
    acc_scr[...] += _dot(act, wd_ref[...])

    @pl.when(j == pl.num_programs(1) - 1)
    def _():
        y = x_ref[...] + mod_ref[:, 5 * d:6 * d] * acc_scr[...]
        if final:
            ms = jnp.mean(y * y, axis=-1, keepdims=True)
            y = y * lax.rsqrt(ms + EPS) * fg_ref[...]
        o_ref[...] = y


def _ffn_call(x, mod_l, g, w_up, conv_w, conv_b, w_down, final_g, *, final):
    tm = DEC_SEQ
    nf = D_FF // FF_TILE
    return pl.pallas_call(
        functools.partial(_ffn_kernel, tm=tm, final=final),
        grid=(N_TOK // tm, nf),
        in_specs=[
            pl.BlockSpec((tm, D_MODEL), lambda i, j: (i, 0)),
            pl.BlockSpec((None, 1, 6 * D_MODEL), lambda i, j: (_cond_row(i, tm), 0, 0)),
            pl.BlockSpec((1, D_MODEL), lambda i, j: (0, 0)),
            pl.BlockSpec((D_MODEL, FF_TILE), lambda i, j: (0, j)),
            pl.BlockSpec((D_MODEL, FF_TILE), lambda i, j: (0, nf + j)),
            pl.BlockSpec((3, FF_TILE), lambda i, j: (0, j)),
            pl.BlockSpec((3, FF_TILE), lambda i, j: (0, nf + j)),
            pl.BlockSpec((1, FF_TILE), lambda i, j: (0, j)),
            pl.BlockSpec((1, FF_TILE), lambda i, j: (0, nf + j)),
            pl.BlockSpec((FF_TILE, D_MODEL), lambda i, j: (j, 0)),
            pl.BlockSpec((1, D_MODEL), lambda i, j: (0, 0)),
        ],
        out_specs=pl.BlockSpec((tm, D_MODEL), lambda i, j: (i, 0)),
        out_shape=jax.ShapeDtypeStruct((N_TOK, D_MODEL), F32),
        scratch_shapes=[pltpu.VMEM((tm, D_MODEL), BF16), pltpu.VMEM((tm, D_MODEL), F32)],
        compiler_params=_params(("arbitrary", "arbitrary")),
        name="conv_ffn",
    )(x, mod_l, g, w_up, w_up, conv_w, conv_w, conv_b, conv_b, w_down, final_g)


def _odd_proj_kernel(x_ref, mod_ref, g_ref, w_ref, mu_ref, o_ref, h_scr, *, tm, shift):
    d = D_MODEL
    i, j = pl.program_id(0), pl.program_id(1)

    @pl.when(j == 0)
    def _():
        h_scr[...] = _norm_mod(x_ref[...], g_ref[...], mod_ref[:, d:2 * d], mod_ref[:, 0:d]).astype(BF16)

    p = _dot(h_scr[...], w_ref[...])
    if shift:
        prev, nxt = _shifted(p, _seq_len(i, tm))
        p = p + mu_ref[...] * (0.5 * (prev + nxt) - p)
    o_ref[...] = p


def _odd_proj_call(x, mod_l, g, w, mu, *, shift):
    tm = DEC_SEQ
    tn = 640
    n = w.shape[1]
    return pl.pallas_call(
        functools.partial(_odd_proj_kernel, tm=tm, shift=shift),
        grid=(N_TOK // tm, n // tn),
        in_specs=[
            pl.BlockSpec((tm, D_MODEL), lambda i, j: (i, 0)),
            pl.BlockSpec((None, 1, 6 * D_MODEL), lambda i, j: (_cond_row(i, tm), 0, 0)),
            pl.BlockSpec((1, D_MODEL), lambda i, j: (0, 0)),
            pl.BlockSpec((D_MODEL, tn), lambda i, j: (0, j)),
            pl.BlockSpec((1, tn), lambda i, j: (0, j)),
        ],
        out_specs=pl.BlockSpec((tm, tn), lambda i, j: (i, j)),
        out_shape=jax.ShapeDtypeStruct((N_TOK, n), F32),
        scratch_shapes=[pltpu.VMEM((tm, D_MODEL), BF16)],
        compiler_params=_params(("arbitrary", "arbitrary")),
        name="odd_proj_rwkv" if shift else "odd_proj_hgrn",
    )(x, mod_l, g, w, mu)


def _hgrn_kernel(q_ref, ff_ref, fb_ref, v_ref, gc_ref, lbl_ref, ng_ref, s0_ref, ones_ref, o_ref, sfin_ref,
                 oacc_f, oacc_b, st_scr, *, seq, lidx):
    L = HGRN_L
    nc = seq // L
    nb = L // SUBLANES
    ones = ones_ref[...]
    rowi = lax.broadcasted_iota(jnp.int32, (L, L), 0)
    coli = lax.broadcasted_iota(jnp.int32, (L, L), 1)
    rowv = lax.broadcasted_iota(jnp.int32, (L, 1), 0)
    same_block = (rowi // SUBLANES) == (coli // SUBLANES)

    def lower_bound(drc):
        lg = lbl_ref[drc]
        e = jnp.exp(lg - jnp.max(lg, axis=0, keepdims=True))
        sm = e / jnp.sum(e, axis=0, keepdims=True)
        return functools.reduce(jnp.add, [sm[i:i + 1, :] for i in range(1, lidx + 1)])

    def spread(t, s_l):
        return jnp.concatenate(
            [jnp.broadcast_to(t[b * SUBLANES + s_l:b * SUBLANES + s_l + 1, :], (SUBLANES, LANES)) for b in range(nb)],
            axis=0)

    def chunk(rev, r0, st, f_ref, lb, tri, hs):
        q = _silu(q_ref[pl.ds(r0, L), hs])
        f = lb + (1.0 - lb) * _sigmoid(f_ref[pl.ds(r0, L), hs])
        k = 1.0 - f
        lf = jnp.log(f)
        v = v_ref[pl.ds(r0, L), hs].astype(BF16)
        cum = _dot_hi(tri, lf)
        cum_ex = cum - lf
        o = _dot_nt((q * jnp.exp(cum)).astype(BF16), st.astype(BF16))

        scores = jnp.zeros((L, L), F32)
        h = L // 2
        while h >= SUBLANES:
            pieces = []
            for a in range(0, L, 2 * h):
                edge = a + h - 1 if rev else a + h
                pieces.append(jnp.broadcast_to(cum_ex[edge:edge + 1, :], (2 * h, LANES)))
            anchor = jnp.concatenate(pieces, axis=0) if len(pieces) > 1 else pieces[0]
            is_q = ((rowv // h) % 2) == (0 if rev else 1)
            d = cum - anchor
            e = jnp.exp(jnp.where(is_q, d, -d))
            qe = jnp.where(is_q, q * e, 0.0).astype(BF16)
            ke = jnp.where(is_q, 0.0, k * e).astype(BF16)
            same_pair = (rowi // (2 * h)) == (coli // (2 * h))
            scores = scores + jnp.where(same_pair, _dot_nt(qe, ke), 0.0)
            h //= 2

        prods = []
        for s_l in range(SUBLANES):
            e = jnp.exp(jnp.minimum(cum - spread(cum, s_l), 0.0))
            prods.append(q * spread(k, s_l) * e)
        diag = _dot(jnp.concatenate(prods, axis=0).astype(BF16), ones)
        for s_l in range(SUBLANES):
            ok = ((rowi % SUBLANES) <= s_l) if rev else ((rowi % SUBLANES) >= s_l)
            take = same_block & ((coli % SUBLANES) == s_l) & ok
            scores = jnp.where(take, diag[s_l * L:(s_l + 1) * L, 0:L], scores)
        o = o + _dot(scores.astype(BF16), v)

        end = 0 if rev else L - 1
        cend = cum[end:end + 1, :]
        kd = (k * jnp.exp(cend - cum)).astype(BF16)
        return o, st * jnp.exp(cend) + _dot_tn(v, kd)

    lb_f, lb_b = lower_bound(0), lower_bound(1)
    tri_f = jnp.where(coli <= rowi, 1.0, 0.0).astype(F32)
    tri_b = jnp.where(coli >= rowi, 1.0, 0.0).astype(F32)
    heads = [slice(hh * LANES, (hh + 1) * LANES) for hh in range(HGRN_HP)]

    for hh, hs in enumerate(heads):
        st_scr[2 * hh] = s0_ref[0, hh].T
        st_scr[2 * hh + 1] = s0_ref[1, hh].T

    def body(ci, carry):
        r0f = pl.multiple_of(ci * L, L)
        r0b = pl.multiple_of((nc - 1 - ci) * L, L)
        for hh, hs in enumerate(heads):
            o_f, st_f = chunk(False, r0f, st_scr[2 * hh], ff_ref, lb_f[:, hs], tri_f, hs)
            o_b, st_b = chunk(True, r0b, st_scr[2 * hh + 1], fb_ref, lb_b[:, hs], tri_b, hs)
            st_scr[2 * hh] = st_f
            st_scr[2 * hh + 1] = st_b
            oacc_f[pl.ds(r0f, L), hs] = o_f
            oacc_b[pl.ds(r0b, L), hs] = o_b
        return carry

    lax.fori_loop(0, nc, body, 0)
    for hh, hs in enumerate(heads):
        sfin_ref[0, hh] = st_scr[2 * hh].T
        sfin_ref[1, hh] = st_scr[2 * hh + 1].T
        o = oacc_f[:, hs] + oacc_b[:, hs]
        ms = jnp.mean(o * o, axis=-1, keepdims=True)
        o_ref[:, hs] = o * lax.rsqrt(ms + EPS) * ng_ref[...] * _silu(gc_ref[:, hs])


def _hgrn_call(proj_h, lb_logits, norm_g, s0, ones, *, n_seq, seq, row0, lidx):
    sblk0 = row0 // seq

    hp = HGRN_HP
    width = hp * LANES

    def col(section):
        return pl.BlockSpec((seq, width), lambda b, h: (sblk0 + b, section * (HC // hp) + h))

    st_spec = pl.BlockSpec((None, 2, hp, DK_C, DV_C), lambda b, h: (b, 0, h, 0, 0))
    return pl.pallas_call(
        functools.partial(_hgrn_kernel, seq=seq, lidx=lidx),
        grid=(n_seq, HC // hp),
        in_specs=[
            col(0), col(1), col(2), col(3), col(4),
            pl.BlockSpec((2, DEPTH, width), lambda b, h: (0, 0, h)),
            pl.BlockSpec((1, LANES), lambda b, h: (0, 0)),
            st_spec,
            pl.BlockSpec((LANES, LANES), lambda b, h: (0, 0)),
        ],
        out_specs=[pl.BlockSpec((seq, width), lambda b, h: (b, h)), st_spec],
        out_shape=[jax.ShapeDtypeStruct((n_seq * seq, D_C), F32),
                   jax.ShapeDtypeStruct((n_seq, 2, HC, DK_C, DV_C), F32)],
        scratch_shapes=[pltpu.VMEM((seq, width), F32), pltpu.VMEM((seq, width), F32),
                        pltpu.VMEM((2 * hp, DV_C, DK_C), F32)],
        compiler_params=_params(("arbitrary", "arbitrary")),
        name="hgrn_dec" if row0 else "hgrn_ctx",
    )(proj_h, proj_h, proj_h, proj_h, proj_h, lb_logits, norm_g, s0, ones)


def _rwkv_prep_kernel(p_ref, a0_ref, aup_ref, gup_ref, kk_ref, ka_ref, rk_ref, w0_ref, wup0_ref, wup1_ref,
                      gs_ref, w0o_ref, w1o_ref, nkk_ref, bb_ref, kt_ref, g_ref, rkv_ref):
    r = p_ref[:, 0:512]
    k = p_ref[:, 512:1024]
    v = p_ref[:, 1024:1536]
    wd = p_ref[:, 1536:1664]
    gd = p_ref[:, 1664:1792]
    ad = p_ref[:, 1792:1920]
    gs = gs_ref[...]

    def group_sum(t):
        return jnp.concatenate([_dot_hi(t[:, j * LANES:(j + 1) * LANES], gs) for j in range(4)], axis=1)

    a = _sigmoid(a0_ref[...] + _dot_hi(ad, aup_ref[...]))
    g_ref[...] = _dot_hi(_sigmoid(gd), gup_ref[...])
    kkr = k * kk_ref[...]
    kk = kkr / jnp.maximum(jnp.sqrt(group_sum(kkr * kkr)), 1e-12)
    kt = k * (1.0 + (a - 1.0) * ka_ref[...])
    th = jnp.tanh(wd)
    decay = math.exp(-0.5)
    w0o_ref[...] = jnp.exp(-decay * _sigmoid(w0_ref[0:1, :] + _dot_hi(th, wup0_ref[...])))
    w1o_ref[...] = jnp.exp(-decay * _sigmoid(w0_ref[1:2, :] + _dot_hi(th, wup1_ref[...])))
    nkk_ref[...] = -kk
    bb_ref[...] = kk * a
    kt_ref[...] = kt
    rkv_ref[...] = group_sum(r * kt * rk_ref[...]) * v


def _rwkv_prep_call(proj_r, a0, aup, gup, kk_k, k_a, r_k, w0, wup0, wup1, gs):
    tm = 512
    full = lambda shape: pl.BlockSpec(shape, lambda i: (0,) * len(shape))
    ospec = pl.BlockSpec((tm, D_D), lambda i: (i, 0))
    return pl.pallas_call(
        _rwkv_prep_kernel,
        grid=(N_TOK // tm,),
        in_specs=[
            pl.BlockSpec((tm, RWKV_PAD), lambda i: (i, 0)),
            full((1, D_D)), full((LANES, D_D)), full((LANES, D_D)), full((1, D_D)), full((1, D_D)), full((1, D_D)),
            full((2, D_D)), full((LANES, D_D)), full((LANES, D_D)), full((LANES, LANES)),
        ],
        out_specs=[ospec] * 7,
        out_shape=[jax.ShapeDtypeStruct((N_TOK, D_D), F32)] * 7,
        compiler_params=_params(("arbitrary",)),
        name="rwkv_prep",
    )(proj_r, a0, aup, gup, kk_k, k_a, r_k, w0, wup0, wup1, gs)


def _rwkv_scan_kernel(rf_ref, wf_ref, nf_ref, bf_ref, kf_ref, vf_ref, rb_ref, wb_ref, nb_ref, bb_ref, kb_ref,
                      vb_ref, s0_ref, q1_ref, wsp_ref, of_ref, ob_ref, sfin_ref, *scratch):
    tb = pl.program_id(1)
    ns = RWKV_NS
    ngrp = D_D // LANES
    steps = RWKV_TB
    groups = [(s, drc) for s in range(ns) for drc in range(2)]
    st, ot, pabuf, obuf = (scratch[i * len(groups):(i + 1) * len(groups)] for i in range(4))
    m_rows = ngrp * DH_D
    dirs = ((rf_ref, wf_ref, nf_ref, bf_ref, kf_ref, vf_ref), (rb_ref, wb_ref, nb_ref, bb_ref, kb_ref, vb_ref))
    o_refs = (of_ref, ob_ref)

    def rows(g, n=DH_D, base=0):
        return slice(base + g * n, base + (g + 1) * n)

    @pl.when(tb == 0)
    def _():
        for q, (s, drc) in enumerate(groups):
            for g in range(ngrp):
                st[q][rows(g), :] = s0_ref[s, drc, g]

    for q in range(len(groups)):
        ot[q][...] = jnp.zeros_like(ot[q])
    lane = lax.broadcasted_iota(jnp.int32, (1, LANES), 1)
    diag = (lax.broadcasted_iota(jnp.int32, (DH_D, LANES), 1) & (DH_D - 1)) == lax.broadcasted_iota(
        jnp.int32, (DH_D, LANES), 0)

    def step_group(t8, carry):
        bases = (pl.multiple_of(t8 * SUBLANES, SUBLANES), pl.multiple_of(steps - (t8 + 1) * SUBLANES, SUBLANES))
        blk = [[ref[s, pl.ds(bases[drc], SUBLANES), :] for ref in dirs[drc]] for s, drc in groups]
        for j in range(SUBLANES):
            sums = []
            for q, (s, drc) in enumerate(groups):
                loc = (SUBLANES - 1 - j) if drc else j
                n_b, v_b = blk[q][2], blk[q][5]
                for g in range(ngrp):
                    sl = slice(g * LANES, (g + 1) * LANES)
                    pabuf[q][rows(g), :] = (st[q][rows(g), :] * n_b[loc:loc + 1, sl]).astype(BF16)
                    pabuf[q][rows(g, base=m_rows), :] = jnp.where(diag, v_b[loc:loc + 1, sl], 0.0).astype(BF16)
                sums.append(_dot(pabuf[q][...], q1_ref[...]))
            for q, (s, drc) in enumerate(groups):
                loc = (SUBLANES - 1 - j) if drc else j
                r_b, w_b, _, b_b, k_b, _ = blk[q]
                for g in range(ngrp):
                    sl = slice(g * LANES, (g + 1) * LANES)
                    sv = (st[q][rows(g), :] * w_b[loc:loc + 1, sl] + sums[q][rows(g)] * b_b[loc:loc + 1, sl]
                          + sums[q][rows(g, base=m_rows)] * k_b[loc:loc + 1, sl])
                    st[q][rows(g), :] = sv
                    obuf[q][rows(g), loc * LANES:(loc + 1) * LANES] = (sv * r_b[loc:loc + 1, sl]).astype(BF16)
        for q, (s, drc) in enumerate(groups):
            o_all = _dot(obuf[q][...], wsp_ref[...])
            fresh = (lane >= bases[drc]) & (lane < bases[drc] + SUBLANES)
            for g in range(ngrp):
                top, bot = slice(g * LANES, g * LANES + DH_D), slice(g * LANES + DH_D, (g + 1) * LANES)
                ot[q][top, :] = jnp.where(fresh, o_all[rows(g), 0:LANES], ot[q][top, :])
                ot[q][bot, :] = jnp.where(fresh, o_all[rows(g), LANES:2 * LANES], ot[q][bot, :])
        return carry

    lax.fori_loop(0, steps // SUBLANES, step_group, 0)

    for q, (s, drc) in enumerate(groups):
        for g in range(ngrp):
            o_refs[drc][s, :, g * LANES:(g + 1) * LANES] = ot[q][rows(g, LANES), :].T

    @pl.when(tb == pl.num_programs(1) - 1)
    def _():
        for q, (s, drc) in enumerate(groups):
            for g in range(ngrp):
                sfin_ref[s, drc, g] = st[q][rows(g), :]


def _rwkv_scan_call(r_src, w0, w1, nkk, bb, kt, s0, q1, wsp, *, n_seq, seq, row0):
    ns = RWKV_NS
    ntb = seq // RWKV_TB
    ngrp = D_D // LANES
    grp0 = row0 // (seq * ns)

    def view(t):
        return t.reshape(N_TOK // (seq * ns), ns, ntb, RWKV_TB, t.shape[-1])

    def tok(rev, cb=0):
        if rev:
            return pl.BlockSpec((None, ns, None, RWKV_TB, D_D), lambda b, t: (grp0 + b, 0, ntb - 1 - t, 0, cb))
        return pl.BlockSpec((None, ns, None, RWKV_TB, D_D), lambda b, t: (grp0 + b, 0, t, 0, cb))

    def out(rev):
        if rev:
            return pl.BlockSpec((None, ns, None, RWKV_TB, D_D), lambda b, t: (b, 0, ntb - 1 - t, 0, 0))
        return pl.BlockSpec((None, ns, None, RWKV_TB, D_D), lambda b, t: (b, 0, t, 0, 0))

    st_spec = pl.BlockSpec((ns, 2, ngrp, DH_D, LANES), lambda b, t: (b, 0, 0, 0, 0))
    in_specs = []
    for rev in (False, True):
        in_specs += [tok(rev, 0), tok(rev), tok(rev), tok(rev), tok(rev), tok(rev, 2)]
    in_specs += [st_spec, pl.BlockSpec((LANES, LANES), lambda b, t: (0, 0)),
                 pl.BlockSpec((SUBLANES * LANES, 2 * LANES), lambda b, t: (0, 0))]
    rv, w0v, w1v, nv, bv, kv = [view(t) for t in (r_src, w0, w1, nkk, bb, kt)]
    o_shape = jax.ShapeDtypeStruct((n_seq // ns, ns, ntb, RWKV_TB, D_D), F32)
    m_rows = ngrp * DH_D
    n_groups = ns * 2
    scratch = ([pltpu.VMEM((m_rows, LANES), F32)] * n_groups + [pltpu.VMEM((2 * m_rows, LANES), F32)] * n_groups
               + [pltpu.VMEM((2 * m_rows, LANES), BF16)] * n_groups
               + [pltpu.VMEM((m_rows, SUBLANES * LANES), BF16)] * n_groups)
    o_f, o_b, s_fin = pl.pallas_call(
        _rwkv_scan_kernel,
        grid=(n_seq // ns, ntb),
        in_specs=in_specs,
        out_specs=[out(False), out(True), st_spec],
        out_shape=[o_shape, o_shape, jax.ShapeDtypeStruct((n_seq, 2, ngrp, DH_D, LANES), F32)],
        scratch_shapes=scratch,
        compiler_params=_params(("arbitrary", "arbitrary")),
        name="rwkv_dec" if row0 else "rwkv_ctx",
    )(rv, w0v, nv, bv, kv, rv, rv, w1v, nv, bv, kv, rv, s0, q1, wsp)
    return o_f.reshape(n_seq * seq, D_D), o_b.reshape(n_seq * seq, D_D), s_fin


def _rwkv_post_kernel(ofl_ref, ofh_ref, obl_ref, obh_ref, rkv_ref, g_ref, lng_ref, lnb_ref, gm_ref, o_ref, *, tm):
    gm = gm_ref[...]
    o_all = _pick(tm, ofl_ref, ofh_ref) + _pick(tm, obl_ref, obh_ref)
    for j in range(D_D // LANES):
        sl = slice(j * LANES, (j + 1) * LANES)
        o = o_all[:, sl]
        dlt = o - _dot_hi(o, gm)
        var = _dot_hi(dlt * dlt, gm)
        y = dlt * lax.rsqrt(var + RWKV_GN_EPS) * lng_ref[:, sl] + lnb_ref[:, sl]
        o_ref[:, sl] = (y + rkv_ref[:, sl]) * g_ref[:, sl]


def _rwkv_post_call(o_f, o_b, rkv, g, ln_g, ln_b, gm):
    tm = 512
    tok = pl.BlockSpec((tm, D_D), lambda i: (i, 0))
    row = pl.BlockSpec((1, D_D), lambda i: (0, 0))
    return pl.pallas_call(
        functools.partial(_rwkv_post_kernel, tm=tm),
        grid=(N_TOK // tm,),
        in_specs=_pair_specs(tm, D_D, o_f) + _pair_specs(tm, D_D, o_b)
        + [tok, tok, row, row, pl.BlockSpec((LANES, LANES), lambda i: (0, 0))],
        out_specs=tok,
        out_shape=jax.ShapeDtypeStruct((N_TOK, D_D), F32),
        compiler_params=_params(("arbitrary",)),
        name="rwkv_post",
    )(o_f[0], o_f[1], o_b[0], o_b[1], rkv, g, ln_g, ln_b, gm)


def _rope_tables():
    pos = np.arange(DEC_SEQ)
    pr, pc = pos // GRID_W, pos % GRID_W
    lane = np.arange(LANES)
    dd = lane % DH_A
    use_col = (dd // 32) == 1
    j = dd % 16
    is_lo = (dd % 32) < 16
    freq = ROPE_THETA ** (-(j.astype(np.float64)) / 16.0)
    p = np.where(use_col[None, :], pc[:, None], pr[:, None]).astype(np.float64)
    ang = (p.astype(np.float32) * freq.astype(np.float32)[None, :]).astype(np.float32)
    cos = np.cos(ang).astype(np.float32)
    sin = np.sin(ang).astype(np.float32)
    s1 = np.where(is_lo[None, :], -sin, 0.0).astype(np.float32)
    s2 = np.where(is_lo[None, :], 0.0, sin).astype(np.float32)
    ident = 512
    cos = np.concatenate([cos, np.ones((ident, LANES), np.float32)], 0)
    s1 = np.concatenate([s1, np.zeros((ident, LANES), np.float32)], 0)
    s2 = np.concatenate([s2, np.zeros((ident, LANES), np.float32)], 0)
    return jnp.asarray(cos), jnp.asarray(s1), jnp.asarray(s2)


def _block_diag(value):
    m = np.zeros((LANES, LANES), np.float32)
    half = LANES // 2
    m[:half, :half] = value
    m[half:, half:] = value
    return jnp.asarray(m)


def _head_spread():
    step = np.arange(SUBLANES * LANES) // LANES
    head = (np.arange(SUBLANES * LANES) % LANES) // DH_D
    out_head = np.arange(2 * LANES) // LANES
    out_step = (np.arange(2 * LANES) % LANES) % SUBLANES
    m = (head[:, None] == out_head[None, :]) & (step[:, None] == out_step[None, :])
    return jnp.asarray(m.astype(np.float32))


def _qb_perm():
    idx = np.zeros(D_B, np.int32)
    for j in range(HB // 2):
        for hh in range(2):
            for dch in range(DH_B):
                idx[j * LANES + hh * DH_B + dch] = (hh * (HB // 2) + j) * DH_B + dch
    return idx


def kernel(x_prompt, x_sample, cache_a_k, cache_a_v, cache_b_k, cache_b_v, state_hgrn, state_rwkv, c, c_ctx, ada_w, ada_b, norm_mix_g, norm_ffn_g, final_norm_g, ev_w_in, ev_w_out, a_lambda, a_subln_g, b_q_norm_g, b_k_norm_g, od_w_in, od_w_out, hgrn_lb_logits, hgrn_norm_g, rwkv_mu, rwkv_w0, rwkv_w_up, rwkv_a0, rwkv_a_up, rwkv_g_up, rwkv_k_k, rwkv_k_a, rwkv_r_k, rwkv_ln_g, rwkv_ln_b, ffn_w_up, ffn_conv_w, ffn_conv_b, ffn_w_down):
    d = D_MODEL
    x = (x_prompt.reshape(N_CTX, d), x_sample.reshape(N_DEC, d), 0)
    cond =jnp.concatenate([c_ctx[None, :], c, jnp.zeros((COND_ROWS - 1 - DEC_BATCH, d), F32)], axis=0)
    mod = _mod_call(cond, ada_w, ada_b).reshape(DEPTH, COND_ROWS, 1, 6 * d)

    cos, s1, s2 = _rope_tables()
    g_mean = _block_diag(1.0 / DH_B)
    g_sum = _block_diag(1.0)
    perm = _qb_perm()
    row = lambda t: t.reshape(1, -1)

    new_ctx = None
    for l in range(DEPTH):
        i = l // 2
        mod_l = mod[l]
        if l % 2 == 0:
            w_in = ev_w_in[i]
            w_in = jnp.concatenate([w_in[:, :1536], w_in[:, 1536:2048][:, perm], w_in[:, 2048:]], axis=1).astype(BF16)
            w_out = ev_w_out[i]
            w_out = jnp.concatenate([w_out[:D_A], w_out[D_A:][perm]], axis=0).astype(BF16)
            qg = row(jnp.tile(b_q_norm_g[i], 2))
            kg = row(jnp.tile(b_k_norm_g[i], 2))
            qa, ka, va, qb, kb, vb = _even_proj_call(x, mod_l, row(norm_mix_g[l]), w_in, qg, kg, cos, s1, s2, g_mean)
            lam_init = 0.8 - 0.6 * math.exp(-0.3 * l)
            sub_g = row(a_subln_g[i])
            oa_c, ob_c = _attn_call(qa, qb, ka, va, kb, vb, None, a_lambda[i], sub_g, lam_init,
                                    n_seq=BATCH, seq=SEQ, row0=0)
            cbk = jnp.transpose(cache_b_k[:, i], (0, 2, 1, 3)).reshape(DEC_BATCH, PAST_LEN, HKV_B * DH_B)
            cbv = jnp.transpose(cache_b_v[:, i], (0, 2, 1, 3)).reshape(DEC_BATCH, PAST_LEN, HKV_B * DH_B)
            cache = (cache_a_k[:, i], cache_a_v[:, i], cbk, cbv)
            oa_s, ob_s = _attn_call(qa, qb, ka, va, kb, vb, cache, a_lambda[i], sub_g, lam_init,
                                    n_seq=DEC_BATCH, seq=DEC_SEQ, row0=N_CTX)
            mix_a = (oa_c, oa_s, 0)
            mix_b = (ob_c, ob_s, 0)

            def heads_first(t, nh):
                t = t[:N_CTX].reshape(BATCH, SEQ, nh, -1)
                return jnp.transpose(t, (0, 2, 1, 3))[:, None]

            even_ctx = (heads_first(ka, HA), heads_first(va, HA), heads_first(kb, HKV_B), heads_first(vb, HKV_B))
        else:
            w = od_w_in[i]
            w_h = w[:, :HGRN_PROJ].astype(BF16)
            wr = w[:, HGRN_PROJ:]
            mu = rwkv_mu[i]

            def rwkv_cols(t):
                z = jnp.zeros(t.shape[:-1] + (RWKV_PAD - RWKV_PROJ,), t.dtype)
                return jnp.concatenate([t[..., :1664], t[..., 1728:1856], t[..., 1664:1728], z], axis=-1)

            w_r = rwkv_cols(wr).astype(BF16)
            mu_r = row(rwkv_cols(mu))
            g_l = row(norm_mix_g[l])
            assert x[0] is x[1]
            proj_h = _odd_proj_call(x[0], mod_l, g_l, w_h, jnp.zeros((1, HGRN_PROJ), F32), shift=False)
            proj_r = _odd_proj_call(x[0], mod_l, g_l, w_r, mu_r, shift=True)

            ng = row(hgrn_norm_g[i])
            zero_h = jnp.zeros((BATCH, 2, HC, DK_C, DV_C), F32)
            ones = jnp.ones((LANES, LANES), BF16)
            oc_c, sh_c = _hgrn_call(proj_h, hgrn_lb_logits, ng, zero_h, ones, n_seq=BATCH, seq=SEQ, row0=0, lidx=l)
            oc_s, _ = _hgrn_call(proj_h, hgrn_lb_logits, ng, state_hgrn[:, i], ones, n_seq=DEC_BATCH, seq=DEC_SEQ,
                                 row0=N_CTX, lidx=l)

            pad_rows = lambda t: jnp.concatenate([t, jnp.zeros((LANES - t.shape[0], t.shape[1]), F32)], axis=0)
            wup0 = pad_rows(rwkv_w_up[i, 0])
            wup1 = jnp.concatenate([jnp.zeros((W_LORA, D_D), F32), rwkv_w_up[i, 1]], axis=0)
            w0o, w1o, nkk, bb, kt, gg, rkv = _rwkv_prep_call(
                proj_r, row(rwkv_a0[i]), pad_rows(rwkv_a_up[i]), rwkv_g_up[i], row(rwkv_k_k[i]), row(rwkv_k_a[i]),
                row(rwkv_r_k[i]), rwkv_w0[i], wup0, wup1, g_sum)

            def to_tiles(s):
                b = s.shape[0]
                s = s.reshape(b, 2, HD // 2, 2, DH_D, DH_D)
                return jnp.transpose(s, (0, 1, 2, 4, 3, 5)).reshape(b, 2, HD // 2, DH_D, LANES)

            def from_tiles(s):
                b = s.shape[0]
                s = s.reshape(b, 2, HD // 2, DH_D, 2, DH_D)
                return jnp.transpose(s, (0, 1, 2, 4, 3, 5)).reshape(b, 2, HD, DH_D, DH_D)

            zero_r = jnp.zeros((BATCH, 2, HD // 2, DH_D, LANES), F32)
            q1 = g_sum.astype(BF16)
            wsp = _head_spread().astype(BF16)
            of_c, ob_c, sr_c = _rwkv_scan_call(proj_r, w0o, w1o, nkk, bb, kt, zero_r, q1, wsp,
                                               n_seq=BATCH, seq=SEQ, row0=0)
            of_s, ob_s, _ = _rwkv_scan_call(proj_r, w0o, w1o, nkk, bb, kt, to_tiles(state_rwkv[:, i]), q1, wsp,
                                            n_seq=DEC_BATCH, seq=DEC_SEQ, row0=N_CTX)
            mix_b = _whole(_rwkv_post_call((of_c, of_s, 0), (ob_c, ob_s, 0), rkv, gg, row(rwkv_ln_g[i]),
                                           row(rwkv_ln_b[i]), g_mean))
            mix_a = (oc_c, oc_s, 0)
            w_out = od_w_out[i].astype(BF16)
            odd_ctx = (sh_c[:, None], from_tiles(sr_c)[:, None])

        x1 = _out_proj_call(x, mix_a, mix_b, mod_l, w_out)
        x2 = _ffn_call(x1, mod_l, row(norm_ffn_g[l]), ffn_w_up[l].astype(BF16), ffn_conv_w[l], row(ffn_conv_b[l]),
                       ffn_w_down[l].astype(BF16), row(final_norm_g), final=(l == DEPTH - 1))
        x = _whole(x2)

    y_prompt = x2[:N_CTX].reshape(BATCH, SEQ, d)
    y_sample = x2[N_CTX:].reshape(DEC_BATCH, DEC_SEQ, d)
    return (y_prompt, y_sample) + even_ctx + odd_ctx
```

```python
import functools
import math

import jax
import jax.numpy as jnp
import numpy as np
from jax import lax
from jax.experimental import pallas as pl
from jax.experimental.pallas import tpu as pltpu

D_MODEL = 1024
BATCH = 16
SEQ = 256
DEPTH = 2
DEC_BATCH = 4
DEC_SEQ = 1024
PAST_LEN = 512
GRID_W = 64
ROPE_THETA = 10000.0
EPS = 1e-6
RWKV_GN_EPS = 64e-5
HA = 4
DH_A = 64
DV_A = 2 * DH_A
HB = 8
HKV_B = 2
DH_B = 64
HC = 4
DK_C = 128
DV_C = 128
HD = 8
DH_D = 64
W_LORA = 64
A_LORA = 64
G_LORA = 128
D_FF = 2816

D_A = HA * DV_A
D_B = HB * DH_B
D_C = HC * DV_C
D_D = HD * DH_D
EVEN_PROJ = 2304
HGRN_PROJ = 2560
RWKV_PROJ = 1856
RWKV_PAD = 1920

N_CTX = BATCH * SEQ
N_DEC = DEC_BATCH * DEC_SEQ
N_TOK = N_CTX + N_DEC
COND_ROWS = 8

LANES = 128
SUBLANES = 8
VMEM_LIMIT = 56 * 1024 * 1024

F32 = jnp.float32
BF16 = jnp.bfloat16
HI = lax.Precision.HIGHEST

FF_TILE = 256
HGRN_L = 64
HGRN_HP = 2
RWKV_TB = 128
RWKV_NS = 4
RWKV_LAG = 3
assert RWKV_LAG < 2 * RWKV_NS


def _dot(a, b):
    return jnp.dot(a, b, preferred_element_type=F32)


def _dot_hi(a, b):
    return jnp.dot(a, b, preferred_element_type=F32, precision=HI)


def _dot_b16(a, b):
    return _dot(a.astype(BF16), b.astype(BF16))


def _dot_nt(a, b, precision=None):
    return lax.dot_general(a, b, (((1,), (1,)), ((), ())), preferred_element_type=F32, precision=precision)


def _dot_tn(a, b, precision=None):
    return lax.dot_general(a, b, (((0,), (0,)), ((), ())), preferred_element_type=F32, precision=precision)


def _sigmoid(x):
    return 1.0 / (1.0 + jnp.exp(-x))


def _silu(x):
    return x * _sigmoid(x)


def _norm_mod(x, g, sc, sh):
    ms = jnp.mean(x * x, axis=-1, keepdims=True)
    return (x * lax.rsqrt(ms + EPS) * g) * (1.0 + sc) + sh


def _params(sem):
    return pltpu.CompilerParams(dimension_semantics=sem, vmem_limit_bytes=VMEM_LIMIT)


def _cond_row(i, tm):
    r0 = i * tm
    return jnp.where(r0 < N_CTX, 0, 1 + (r0 - N_CTX) // DEC_SEQ)


def _seq_len(i, tm):
    return jnp.where(i * tm < N_CTX, SEQ, DEC_SEQ)


def _shifted(u, seq_len):
    m = u.shape[0]
    row = lax.broadcasted_iota(jnp.int32, (m, 1), 0)
    pos = row & (seq_len - 1)
    prev = jnp.where(pos == 0, 0.0, pltpu.roll(u, 1, 0))
    nxt = jnp.where(pos == seq_len - 1, 0.0, pltpu.roll(u, m - 1, 0))
    return prev, nxt


def _pick(tm, lo_ref, hi_ref):
    return jnp.where(pl.program_id(0) * tm < N_CTX, lo_ref[...], hi_ref[...])


def _pair_specs(tm, width, pair):
    nct = N_CTX // tm
    hi0 = pair[2] // tm
    return [pl.BlockSpec((tm, width), lambda i: (jnp.minimum(i, nct - 1), 0)),
            pl.BlockSpec((tm, width), lambda i: (hi0 + jnp.maximum(i - nct, 0), 0))]


def _whole(t):
    return (t, t, N_CTX)


def _mod_kernel(c_ref, w_ref, b_ref, o_ref):
    s = _silu(c_ref[...]).astype(BF16)
    o_ref[...] = _dot(s, w_ref[...].astype(BF16)) + b_ref[...]


def _mod_call(cond, ada_w, ada_b):
    tn = 1536
    n = 6 * D_MODEL
    return pl.pallas_call(
        _mod_kernel,
        grid=(DEPTH, n // tn),
        in_specs=[
            pl.BlockSpec((COND_ROWS, D_MODEL), lambda l, j: (0, 0)),
            pl.BlockSpec((None, D_MODEL, tn), lambda l, j: (l, 0, j)),
            pl.BlockSpec((None, 1, tn), lambda l, j: (l, 0, j)),
        ],
        out_specs=pl.BlockSpec((None, COND_ROWS, tn), lambda l, j: (l, 0, j)),
        out_shape=jax.ShapeDtypeStruct((DEPTH, COND_ROWS, n), F32),
        compiler_params=_params(("arbitrary", "arbitrary")),
        name="ada_mod",
    )(cond, ada_w, ada_b.reshape(DEPTH, 1, n))


def _even_proj_kernel(xl_ref, xh_ref, mod_ref, g_ref, w_ref, qg_ref, kg_ref, cos_ref, s1_ref, s2_ref, gm_ref,
                      qa_ref, ka_ref, va_ref, qb_ref, kb_ref, vb_ref, *, tm):
    d = D_MODEL
    h = _norm_mod(_pick(tm, xl_ref, xh_ref), g_ref[...], mod_ref[:, d:2 * d], mod_ref[:, 0:d]).astype(BF16)
    proj = _dot(h, w_ref[...])
    cos, s1, s2, gm = cos_ref[...], s1_ref[...], s2_ref[...], gm_ref[...]
    scale = DH_A ** -0.5

    def rope(t):
        return t * cos + pltpu.roll(t, LANES - 16, 1) * s1 + pltpu.roll(t, 16, 1) * s2

    def head_norm(t, g):
        return t * lax.rsqrt(_dot_b16(t * t, gm) + EPS) * g

    for j in range(4):
        sl = slice(j * LANES, (j + 1) * LANES)
        qa_ref[:, sl] = rope(proj[:, j * LANES:(j + 1) * LANES]) * scale
        ka_ref[:, sl] = rope(proj[:, 512 + j * LANES:512 + (j + 1) * LANES])
        va_ref[:, sl] = proj[:, 1024 + j * LANES:1024 + (j + 1) * LANES]
        qb_ref[:, sl] = rope(head_norm(proj[:, 1536 + j * LANES:1536 + (j + 1) * LANES], qg_ref[...])) * scale
    kb_ref[...] = rope(head_norm(proj[:, 2048:2176], kg_ref[...]))
    vb_ref[...] = proj[:, 2176:2304]


def _even_proj_call(x, mod_l, g, w, qg, kg, cos, s1, s2, gm):
    tm = 512
    nt = N_TOK // tm
    n_rope_blk = DEC_SEQ // tm

    def rope_idx(i):
        return (jnp.where(i * tm < N_CTX, n_rope_blk, (i - N_CTX // tm) % n_rope_blk), 0)

    full = lambda shape: pl.BlockSpec(shape, lambda i: (0,) * len(shape))
    out512 = pl.BlockSpec((tm, 512), lambda i: (i, 0))
    out128 = pl.BlockSpec((tm, LANES), lambda i: (i, 0))
    return pl.pallas_call(
        functools.partial(_even_proj_kernel, tm=tm),
        grid=(nt,),
        in_specs=_pair_specs(tm, D_MODEL, x) + [
            pl.BlockSpec((None, 1, 6 * D_MODEL), lambda i: (_cond_row(i, tm), 0, 0)),
            full((1, D_MODEL)),
            full((D_MODEL, EVEN_PROJ)),
            full((1, LANES)),
            full((1, LANES)),
            pl.BlockSpec((tm, LANES), rope_idx),
            pl.BlockSpec((tm, LANES), rope_idx),
            pl.BlockSpec((tm, LANES), rope_idx),
            full((LANES, LANES)),
        ],
        out_specs=[out512, out512, out512, out512, out128, out128],
        out_shape=[jax.ShapeDtypeStruct((N_TOK, 512), F32)] * 4 + [jax.ShapeDtypeStruct((N_TOK, LANES), F32)] * 2,
        compiler_params=_params(("arbitrary",)),
        name="even_proj",
    )(x[0], x[1], mod_l, g, w, qg, kg, cos, s1, s2, gm)


def _softmax_pv(q, ks, vs):
    ss = [_dot_nt(q, k) for k in ks]
    m = functools.reduce(jnp.maximum, [jnp.max(s, axis=-1, keepdims=True) for s in ss])
    ps = [jnp.exp(s - m) for s in ss]
    l = functools.reduce(jnp.add, [jnp.sum(p, axis=-1, keepdims=True) for p in ps])
    acc = functools.reduce(jnp.add, [_dot(p.astype(BF16), v) for p, v in zip(ps, vs)])
    return acc / l


def _attn_kernel(*refs, has_cache, lam_init):
    if has_cache:
        (qa_ref, qb_ref, ka_ref, va_ref, kb_ref, vb_ref, cak_ref, cav_ref, cbk_ref, cbv_ref,
         al_ref, sg_ref, oa_ref, ob_ref) = refs
    else:
        qa_ref, qb_ref, ka_ref, va_ref, kb_ref, vb_ref, al_ref, sg_ref, oa_ref, ob_ref = refs
    al = al_ref[...]
    lam = (jnp.exp(jnp.sum(al[0:1] * al[1:2], axis=-1, keepdims=True))
           - jnp.exp(jnp.sum(al[2:3] * al[3:4], axis=-1, keepdims=True)) + lam_init)
    lo = lax.broadcasted_iota(jnp.int32, (1, LANES), 1) < DH_A

    for h in range(HA):
        sl = slice(h * LANES, (h + 1) * LANES)
        q = qa_ref[:, sl]
        ks = [ka_ref[:, sl].astype(BF16)]
        vs = [va_ref[:, sl].astype(BF16)]
        if has_cache:
            ks.insert(0, cak_ref[h].astype(BF16))
            vs.insert(0, cav_ref[h].astype(BF16))
        a1 = _softmax_pv(jnp.where(lo, q, 0.0).astype(BF16), ks, vs)
        a2 = _softmax_pv(jnp.where(lo, 0.0, q).astype(BF16), ks, vs)
        dlt = a1 - lam * a2
        ms = jnp.mean(dlt * dlt, axis=-1, keepdims=True)
        oa_ref[:, sl] = dlt * lax.rsqrt(ms + EPS) * sg_ref[...] * (1.0 - lam_init)

    ks = [kb_ref[...].astype(BF16)]
    vs = [vb_ref[...].astype(BF16)]
    if has_cache:
        ks.insert(0, cbk_ref[...].astype(BF16))
        vs.insert(0, cbv_ref[...].astype(BF16))
    for j in range(HB // 2):
        sl = slice(j * LANES, (j + 1) * LANES)
        q = qb_ref[:, sl]
        o0 = _softmax_pv(jnp.where(lo, q, 0.0).astype(BF16), ks, vs)
        o1 = _softmax_pv(jnp.where(lo, 0.0, q).astype(BF16), ks, vs)
        ob_ref[:, sl] = jnp.where(lo, o0, o1)


def _attn_call(qa, qb, ka, va, kb, vb, cache, a_lambda, subln_g, lam_init, *, n_seq, seq, row0):
    tq = 256
    nq = seq // tq
    qblk0 = row0 // tq
    sblk0 = row0 // seq
    has_cache = cache is not None
    qspec = pl.BlockSpec((tq, 512), lambda b, i: (qblk0 + b * nq + i, 0))
    own512 = pl.BlockSpec((seq, 512), lambda b, i: (sblk0 + b, 0))
    own128 = pl.BlockSpec((seq, LANES), lambda b, i: (sblk0 + b, 0))
    in_specs = [qspec, qspec, own512, own512, own128, own128]
    args = [qa, qb, ka, va, kb, vb]
    if has_cache:
        in_specs += [
            pl.BlockSpec((None, HA, PAST_LEN, LANES), lambda b, i: (b, 0, 0, 0)),
            pl.BlockSpec((None, HA, PAST_LEN, LANES), lambda b, i: (b, 0, 0, 0)),
            pl.BlockSpec((None, PAST_LEN, LANES), lambda b, i: (b, 0, 0)),
            pl.BlockSpec((None, PAST_LEN, LANES), lambda b, i: (b, 0, 0)),
        ]
        args += list(cache)
    in_specs += [pl.BlockSpec((4, DH_A), lambda b, i: (0, 0)), pl.BlockSpec((1, LANES), lambda b, i: (0, 0))]
    args += [a_lambda, subln_g]
    ospec = pl.BlockSpec((tq, 512), lambda b, i: (b * nq + i, 0))
    return pl.pallas_call(
        functools.partial(_attn_kernel, has_cache=has_cache, lam_init=lam_init),
        grid=(n_seq, nq),
        in_specs=in_specs,
        out_specs=[ospec, ospec],
        out_shape=[jax.ShapeDtypeStruct((n_seq * seq, 512), F32)] * 2,
        compiler_params=_params(("arbitrary", "arbitrary")),
        name="attn_dec" if has_cache else "attn_ctx",
    )(*args)


def _out_proj_kernel(xl_ref, xh_ref, al_ref, ah_ref, bl_ref, bh_ref, mod_ref, w_ref, o_ref, *, tm):
    d = D_MODEL
    half = al_ref.shape[1]
    a = _pick(tm, al_ref, ah_ref).astype(BF16)
    b = _pick(tm, bl_ref, bh_ref).astype(BF16)
    mix = _dot(a, w_ref[0:half, :]) + _dot(b, w_ref[half:2 * half, :])
    o_ref[...] = _pick(tm, xl_ref, xh_ref) + mod_ref[:, 2 * d:3 * d] * mix


def _out_proj_call(x, a, b, mod_l, w):
    tm = 512
    return pl.pallas_call(
        functools.partial(_out_proj_kernel, tm=tm),
        grid=(N_TOK // tm,),
        in_specs=_pair_specs(tm, D_MODEL, x) + _pair_specs(tm, 512, a) + _pair_specs(tm, 512, b) + [
            pl.BlockSpec((None, 1, 6 * D_MODEL), lambda i: (_cond_row(i, tm), 0, 0)),
            pl.BlockSpec((D_MODEL, D_MODEL), lambda i: (0, 0)),
        ],
        out_specs=pl.BlockSpec((tm, D_MODEL), lambda i: (i, 0)),
        out_shape=jax.ShapeDtypeStruct((N_TOK, D_MODEL), F32),
        compiler_params=_params(("arbitrary",)),
        name="out_proj",
    )(x[0], x[1], a[0], a[1], b[0], b[1], mod_l, w)


def _ffn_kernel(x_ref, mod_ref, g_ref, wv_ref, wg_ref, cwv_ref, cwg_ref, cbv_ref, cbg_ref, wd_ref, fg_ref,
                o_ref, h_scr, acc_scr, u_scr, *, tm, final):
    d = D_MODEL
    i, j = pl.program_id(0), pl.program_id(1)
    pad = SUBLANES
    ch = SEQ

    @pl.when(j == 0)
    def _():
        h_scr[...] = _norm_mod(x_ref[...], g_ref[...], mod_ref[:, 4 * d:5 * d], mod_ref[:, 3 * d:4 * d]).astype(BF16)
        acc_scr[...] = jnp.zeros_like(acc_scr)
        for k in range(2):
            u_scr[k, 0:pad, :] = jnp.zeros((pad, FF_TILE), F32)
            u_scr[k, pad + tm:2 * pad + tm, :] = jnp.zeros((pad, FF_TILE), F32)

    seq_len = _seq_len(i, tm)
    row8 = lax.broadcasted_iota(jnp.int32, (SUBLANES, 1), 0)

    def project(c):
        rows = slice(c * ch, (c + 1) * ch)
        u_scr[0, pad + c * ch:pad + (c + 1) * ch, :] = _dot(h_scr[rows, :], wv_ref[...])
        u_scr[1, pad + c * ch:pad + (c + 1) * ch, :] = _dot(h_scr[rows, :], wg_ref[...])

    def conv(k, c, cw_ref, cb_ref):
        r0 = pad + c * ch
        at_start = (c * ch) % seq_len == 0
        at_end = ((c + 1) * ch) % seq_len == 0
        prev = u_scr[k, r0 - 1:r0 - 1 + ch, :]
        nxt = u_scr[k, r0 + 1:r0 + 1 + ch, :]
        prev = jnp.concatenate([jnp.where((row8 == 0) & at_start, 0.0, prev[0:SUBLANES]), prev[SUBLANES:]], axis=0)
        nxt = jnp.concatenate(
            [nxt[:ch - SUBLANES], jnp.where((row8 == SUBLANES - 1) & at_end, 0.0, nxt[ch - SUBLANES:])], axis=0)
        return cw_ref[0:1, :] * prev + cw_ref[1:2, :] * u_scr[k, r0:r0 + ch, :] + cw_ref[2:3, :] * nxt + cb_ref[...]

    def activate(c):
        val = conv(0, c, cwv_ref, cbv_ref)
        gate = conv(1, c, cwg_ref, cbg_ref)
        act = (_silu(gate) * val).astype(BF16)
        acc_scr[c * ch:(c + 1) * ch, :] += _dot(act, wd_ref[...])

    n_chunks = tm // ch
    project(0)
    for c in range(n_chunks):
        if c + 1 < n_chunks:
            project(c + 1)
        activate(c)

    @pl.when(j == pl.num_programs(1) - 1)
    def _():
        y = x_ref[...] + mod_ref[:, 5 * d:6 * d] * acc_scr[...]
        if final:
            ms = jnp.mean(y * y, axis=-1, keepdims=True)
            y = y * lax.rsqrt(ms + EPS) * fg_ref[...]
        o_ref[...] = y


def _ffn_call(x, mod_l, g, w_up, conv_w, conv_b, w_down, final_g, *, final):
    tm = DEC_SEQ
    nf = D_FF // FF_TILE
    up = lambda j: j
    down = lambda j: j
    return pl.pallas_call(
        functools.partial(_ffn_kernel, tm=tm, final=final),
        grid=(N_TOK // tm, nf),
        in_specs=[
            pl.BlockSpec((tm, D_MODEL), lambda i, j: (i, 0)),
            pl.BlockSpec((None, 1, 6 * D_MODEL), lambda i, j: (_cond_row(i, tm), 0, 0)),
            pl.BlockSpec((1, D_MODEL), lambda i, j: (0, 0)),
            pl.BlockSpec((D_MODEL, FF_TILE), lambda i, j: (0, up(j))),
            pl.BlockSpec((D_MODEL, FF_TILE), lambda i, j: (0, nf + up(j))),
            pl.BlockSpec((3, FF_TILE), lambda i, j: (0, up(j))),
            pl.BlockSpec((3, FF_TILE), lambda i, j: (0, nf + up(j))),
            pl.BlockSpec((1, FF_TILE), lambda i, j: (0, up(j))),
            pl.BlockSpec((1, FF_TILE), lambda i, j: (0, nf + up(j))),
            pl.BlockSpec((FF_TILE, D_MODEL), lambda i, j: (down(j), 0)),
            pl.BlockSpec((1, D_MODEL), lambda i, j: (0, 0)),
        ],
        out_specs=pl.BlockSpec((tm, D_MODEL), lambda i, j: (i, 0)),
        out_shape=jax.ShapeDtypeStruct((N_TOK, D_MODEL), F32),
        scratch_shapes=[pltpu.VMEM((tm, D_MODEL), BF16), pltpu.VMEM((tm, D_MODEL), F32),
                        pltpu.VMEM((2, tm + 2 * SUBLANES, FF_TILE), F32)],
        compiler_params=_params(("arbitrary", "arbitrary")),
        name="conv_ffn",
    )(x, mod_l, g, w_up, w_up, conv_w, conv_w, conv_b, conv_b, w_down, final_g)


def _odd_proj_kernel(x_ref, mod_ref, g_ref, w_ref, mu_ref, o_ref, h_scr, *, tm, shift):
    d = D_MODEL
    i, j = pl.program_id(0), pl.program_id(1)

    @pl.when(j == 0)
    def _():
        h_scr[...] = _norm_mod(x_ref[...], g_ref[...], mod_ref[:, d:2 * d], mod_ref[:, 0:d]).astype(BF16)

    p = _dot(h_scr[...], w_ref[...])
    if shift:
        prev, nxt = _shifted(p, _seq_len(i, tm))
        p = p + mu_ref[...] * (0.5 * (prev + nxt) - p)
    o_ref[...] = p


def _odd_proj_call(x, mod_l, g, w, mu, *, shift):
    tm = DEC_SEQ
    tn = 640
    n = w.shape[1]
    return pl.pallas_call(
        functools.partial(_odd_proj_kernel, tm=tm, shift=shift),
        grid=(N_TOK // tm, n // tn),
        in_specs=[
            pl.BlockSpec((tm, D_MODEL), lambda i, j: (i, 0)),
            pl.BlockSpec((None, 1, 6 * D_MODEL), lambda i, j: (_cond_row(i, tm), 0, 0)),
            pl.BlockSpec((1, D_MODEL), lambda i, j: (0, 0)),
            pl.BlockSpec((D_MODEL, tn), lambda i, j: (0, j)),
            pl.BlockSpec((1, tn), lambda i, j: (0, j)),
        ],
        out_specs=pl.BlockSpec((tm, tn), lambda i, j: (i, j)),
        out_shape=jax.ShapeDtypeStruct((N_TOK, n), F32),
        scratch_shapes=[pltpu.VMEM((tm, D_MODEL), BF16)],
        compiler_params=_params(("arbitrary", "arbitrary")),
        name="odd_proj_rwkv" if shift else "odd_proj_hgrn",
    )(x, mod_l, g, w, mu)


def _hgrn_kernel(q_ref, ff_ref, fb_ref, v_ref, gc_ref, lbl_ref, ng_ref, s0_ref, ones_ref, o_ref, sfin_ref,
                 oacc_f, oacc_b, st_scr, qd_scr, add_scr, dec_scr, start_scr, *, seq, lidx):
    L = HGRN_L
    nc = seq // L
    nb = L // SUBLANES
    ones = ones_ref[...]
    rowi = lax.broadcasted_iota(jnp.int32, (L, L), 0)
    coli = lax.broadcasted_iota(jnp.int32, (L, L), 1)
    rowv = lax.broadcasted_iota(jnp.int32, (L, 1), 0)
    same_block = (rowi // SUBLANES) == (coli // SUBLANES)

    def lower_bound(drc):
        lg = lbl_ref[drc]
        e = jnp.exp(lg - jnp.max(lg, axis=0, keepdims=True))
        sm = e / jnp.sum(e, axis=0, keepdims=True)
        return functools.reduce(jnp.add, [sm[i:i + 1, :] for i in range(1, lidx + 1)])

    def spread(t, s_l):
        return jnp.concatenate(
            [jnp.broadcast_to(t[b * SUBLANES + s_l:b * SUBLANES + s_l + 1, :], (SUBLANES, LANES)) for b in range(nb)],
            axis=0)

    def chunk(rev, r0, f_ref, lb, tri, hs):
        q = _silu(q_ref[pl.ds(r0, L), hs])
        f = lb + (1.0 - lb) * _sigmoid(f_ref[pl.ds(r0, L), hs])
        k = 1.0 - f
        lf = jnp.log(f)
        v = v_ref[pl.ds(r0, L), hs].astype(BF16)
        cum = _dot_hi(tri, lf)
        cum_ex = cum - lf
        q_dec = (q * jnp.exp(cum)).astype(BF16)

        scores = jnp.zeros((L, L), F32)
        h = L // 2
        while h >= SUBLANES:
            pieces = []
            for a in range(0, L, 2 * h):
                edge = a + h - 1 if rev else a + h
                pieces.append(jnp.broadcast_to(cum_ex[edge:edge + 1, :], (2 * h, LANES)))
            anchor = jnp.concatenate(pieces, axis=0) if len(pieces) > 1 else pieces[0]
            is_q = ((rowv // h) % 2) == (0 if rev else 1)
            d = cum - anchor
            e = jnp.exp(jnp.where(is_q, d, -d))
            qe = jnp.where(is_q, q * e, 0.0).astype(BF16)
            ke = jnp.where(is_q, 0.0, k * e).astype(BF16)
            same_pair = (rowi // (2 * h)) == (coli // (2 * h))
            scores = scores + jnp.where(same_pair, _dot_nt(qe, ke), 0.0)
            h //= 2

        prods = []
        for s_l in range(SUBLANES):
            e = jnp.exp(jnp.minimum(cum - spread(cum, s_l), 0.0))
            prods.append(q * spread(k, s_l) * e)
        diag = _dot(jnp.concatenate(prods, axis=0).astype(BF16), ones)
        for s_l in range(SUBLANES):
            ok = ((rowi % SUBLANES) <= s_l) if rev else ((rowi % SUBLANES) >= s_l)
            take = same_block & ((coli % SUBLANES) == s_l) & ok
            scores = jnp.where(take, diag[s_l * L:(s_l + 1) * L, 0:L], scores)
        o = _dot(scores.astype(BF16), v)

        end = 0 if rev else L - 1
        cend = cum[end:end + 1, :]
        kd = (k * jnp.exp(cend - cum)).astype(BF16)
        return q_dec, o, _dot_tn(v, kd), jnp.exp(cend)

    lb_f, lb_b = lower_bound(0), lower_bound(1)
    tri_f = jnp.where(coli <= rowi, 1.0, 0.0).astype(F32)
    tri_b = jnp.where(coli >= rowi, 1.0, 0.0).astype(F32)
    heads = [slice(hh * LANES, (hh + 1) * LANES) for hh in range(HGRN_HP)]
    chains = [(hh, hs, drc) for hh, hs in enumerate(heads) for drc in range(2)]
    oaccs = (oacc_f, oacc_b)

    def local(ci, carry):
        r0 = pl.multiple_of(ci * L, L)
        for c, (hh, hs, drc) in enumerate(chains):
            q_dec, o, add, dec = chunk(drc == 1, r0, fb_ref if drc else ff_ref, (lb_b if drc else lb_f)[:, hs],
                                       tri_b if drc else tri_f, hs)
            qd_scr[c, pl.ds(r0, L), :] = q_dec
            oaccs[drc][pl.ds(r0, L), hs] = o
            add_scr[c * nc + ci] = add
            dec_scr[c * nc + ci] = jnp.broadcast_to(dec, (SUBLANES, LANES))
        return carry

    lax.fori_loop(0, nc, local, 0, unroll=2)

    for c, (hh, hs, drc) in enumerate(chains):
        st_scr[c] = s0_ref[drc, hh].T

    def scan(n, carry):
        for c, (hh, hs, drc) in enumerate(chains):
            slot = c * nc + ((nc - 1 - n) if drc else n)
            st = st_scr[c]
            start_scr[slot] = st.astype(BF16)
            st_scr[c] = st * dec_scr[slot][0:1, :] + add_scr[slot]
        return carry

    lax.fori_loop(0, nc, scan, 0)

    def carried(ci, carry):
        r0 = pl.multiple_of(ci * L, L)
        for c, (hh, hs, drc) in enumerate(chains):
            oaccs[drc][pl.ds(r0, L), hs] += _dot_nt(qd_scr[c, pl.ds(r0, L), :], start_scr[c * nc + ci])
        return carry

    lax.fori_loop(0, nc, carried, 0)

    for hh, hs in enumerate(heads):
        sfin_ref[0, hh] = st_scr[2 * hh].T
        sfin_ref[1, hh] = st_scr[2 * hh + 1].T
        o = oacc_f[:, hs] + oacc_b[:, hs]
        ms = jnp.mean(o * o, axis=-1, keepdims=True)
        o_ref[:, hs] = o * lax.rsqrt(ms + EPS) * ng_ref[...] * _silu(gc_ref[:, hs])


def _hgrn_call(proj_h, lb_logits, norm_g, s0, ones, *, n_seq, seq, row0, lidx):
    sblk0 = row0 // seq

    hp = HGRN_HP
    width = hp * LANES

    def col(section):
        return pl.BlockSpec((seq, width), lambda b, h: (sblk0 + b, section * (HC // hp) + h))

    st_spec = pl.BlockSpec((None, 2, hp, DK_C, DV_C), lambda b, h: (b, 0, h, 0, 0))
    return pl.pallas_call(
        functools.partial(_hgrn_kernel, seq=seq, lidx=lidx),
        grid=(n_seq, HC // hp),
        in_specs=[
            col(0), col(1), col(2), col(3), col(4),
            pl.BlockSpec((2, DEPTH, width), lambda b, h: (0, 0, h)),
            pl.BlockSpec((1, LANES), lambda b, h: (0, 0)),
            st_spec,
            pl.BlockSpec((LANES, LANES), lambda b, h: (0, 0)),
        ],
        out_specs=[pl.BlockSpec((seq, width), lambda b, h: (b, h)), st_spec],
        out_shape=[jax.ShapeDtypeStruct((n_seq * seq, D_C), F32),
                   jax.ShapeDtypeStruct((n_seq, 2, HC, DK_C, DV_C), F32)],
        scratch_shapes=[pltpu.VMEM((seq, width), F32), pltpu.VMEM((seq, width), F32),
                        pltpu.VMEM((2 * hp, DV_C, DK_C), F32),
                        pltpu.VMEM((2 * hp, seq, DK_C), BF16),
                        pltpu.VMEM((2 * hp * (seq // HGRN_L), DV_C, DK_C), F32),
                        pltpu.VMEM((2 * hp * (seq // HGRN_L), SUBLANES, DK_C), F32),
                        pltpu.VMEM((2 * hp * (seq // HGRN_L), DV_C, DK_C), BF16)],
        compiler_params=_params(("arbitrary", "arbitrary")),
        name="hgrn_dec" if row0 else "hgrn_ctx",
    )(proj_h, proj_h, proj_h, proj_h, proj_h, lb_logits, norm_g, s0, ones)


def _rwkv_prep_kernel(p_ref, a0_ref, aup_ref, gup_ref, kk_ref, ka_ref, rk_ref, w0_ref, wup0_ref, wup1_ref,
                      gs_ref, w0o_ref, w1o_ref, nkk_ref, bb_ref, kt_ref, g_ref, rkv_ref):
    r = p_ref[:, 0:512]
    k = p_ref[:, 512:1024]
    v = p_ref[:, 1024:1536]
    wd = p_ref[:, 1536:1664]
    gd = p_ref[:, 1664:1792]
    ad = p_ref[:, 1792:1920]
    gs = gs_ref[...]

    def group_sum(t):
        return jnp.concatenate([_dot_b16(t[:, j * LANES:(j + 1) * LANES], gs) for j in range(4)], axis=1)

    a = _sigmoid(a0_ref[...] + _dot_b16(ad, aup_ref[...]))
    g_ref[...] = _dot_b16(_sigmoid(gd), gup_ref[...])
    kkr = k * kk_ref[...]
    kk = kkr / jnp.maximum(jnp.sqrt(group_sum(kkr * kkr)), 1e-12)
    kt = k * (1.0 + (a - 1.0) * ka_ref[...])
    th = jnp.tanh(wd)
    decay = math.exp(-0.5)
    w0o_ref[...] = jnp.exp(-decay * _sigmoid(w0_ref[0:1, :] + _dot_b16(th, wup0_ref[...])))
    w1o_ref[...] = jnp.exp(-decay * _sigmoid(w0_ref[1:2, :] + _dot_b16(th, wup1_ref[...])))
    nkk_ref[...] = -kk
    bb_ref[...] = kk * a
    kt_ref[...] = kt
    rkv_ref[...] = group_sum(r * kt * rk_ref[...]) * v


def _rwkv_prep_call(proj_r, a0, aup, gup, kk_k, k_a, r_k, w0, wup0, wup1, gs):
    tm = 512
    full = lambda shape: pl.BlockSpec(shape, lambda i: (0,) * len(shape))
    ospec = pl.BlockSpec((tm, D_D), lambda i: (i, 0))
    return pl.pallas_call(
        _rwkv_prep_kernel,
        grid=(N_TOK // tm,),
        in_specs=[
            pl.BlockSpec((tm, RWKV_PAD), lambda i: (i, 0)),
            full((1, D_D)), full((LANES, D_D)), full((LANES, D_D)), full((1, D_D)), full((1, D_D)), full((1, D_D)),
            full((2, D_D)), full((LANES, D_D)), full((LANES, D_D)), full((LANES, LANES)),
        ],
        out_specs=[ospec] * 7,
        out_shape=[jax.ShapeDtypeStruct((N_TOK, D_D), F32)] * 7,
        compiler_params=_params(("arbitrary",)),
        name="rwkv_prep",
    )(proj_r, a0, aup, gup, kk_k, k_a, r_k, w0, wup0, wup1, gs)


def _rwkv_scan_kernel(rf_ref, wf_ref, nf_ref, bf_ref, kf_ref, vf_ref, rb_ref, wb_ref, nb_ref, bb_ref, kb_ref,
                      vb_ref, s0_ref, q1_ref, wsp_ref, of_ref, ob_ref, sfin_ref, *scratch):
    tb = pl.program_id(1)
    ns = RWKV_NS
    ngrp = D_D // LANES
    steps = RWKV_TB
    groups = [(s, drc) for s in range(ns) for drc in range(2)]
    st, ot, pabuf, obuf = (scratch[i * len(groups):(i + 1) * len(groups)] for i in range(4))
    m_rows = ngrp * DH_D
    dirs = ((rf_ref, wf_ref, nf_ref, bf_ref, kf_ref, vf_ref), (rb_ref, wb_ref, nb_ref, bb_ref, kb_ref, vb_ref))
    o_refs = (of_ref, ob_ref)

    def rows(g, n=DH_D, base=0):
        return slice(base + g * n, base + (g + 1) * n)

    @pl.when(tb == 0)
    def _():
        for q, (s, drc) in enumerate(groups):
            for g in range(ngrp):
                st[q][rows(g), :] = s0_ref[s, drc, g]

    for q in range(len(groups)):
        ot[q][...] = jnp.zeros_like(ot[q])
    lane = lax.broadcasted_iota(jnp.int32, (1, LANES), 1)
    diag = (lax.broadcasted_iota(jnp.int32, (DH_D, LANES), 1) & (DH_D - 1)) == lax.broadcasted_iota(
        jnp.int32, (DH_D, LANES), 0)

    def step_group(t8, carry):
        bases = (pl.multiple_of(t8 * SUBLANES, SUBLANES), pl.multiple_of(steps - (t8 + 1) * SUBLANES, SUBLANES))
        blk = [[ref[s, pl.ds(bases[drc], SUBLANES), :] for ref in dirs[drc]] for s, drc in groups]

        def issue(j, q):
            loc = (SUBLANES - 1 - j) if groups[q][1] else j
            n_b, v_b = blk[q][2], blk[q][5]
            for g in range(ngrp):
                sl = slice(g * LANES, (g + 1) * LANES)
                pabuf[q][rows(g), :] = (st[q][rows(g), :] * n_b[loc:loc + 1, sl]).astype(BF16)
                pabuf[q][rows(g, base=m_rows), :] = jnp.where(diag, v_b[loc:loc + 1, sl], 0.0).astype(BF16)
            return _dot(pabuf[q][...], q1_ref[...])

        def retire(j, q, sums):
            loc = (SUBLANES - 1 - j) if groups[q][1] else j
            r_b, w_b, _, b_b, k_b, _ = blk[q]
            for g in range(ngrp):
                sl = slice(g * LANES, (g + 1) * LANES)
                sv = (st[q][rows(g), :] * w_b[loc:loc + 1, sl] + sums[rows(g)] * b_b[loc:loc + 1, sl]
                      + sums[rows(g, base=m_rows)] * k_b[loc:loc + 1, sl])
                st[q][rows(g), :] = sv
                obuf[q][rows(g), loc * LANES:(loc + 1) * LANES] = (sv * r_b[loc:loc + 1, sl]).astype(BF16)

        order = [(j, q) for j in range(SUBLANES) for q in range(len(groups))]
        inflight = []
        for slot, (j, q) in enumerate(order):
            inflight.append((j, q, issue(j, q)))
            if len(inflight) > RWKV_LAG:
                retire(*inflight.pop(0))
        for item in inflight:
            retire(*item)
        for q, (s, drc) in enumerate(groups):
            o_all = _dot(obuf[q][...], wsp_ref[...])
            fresh = (lane >= bases[drc]) & (lane < bases[drc] + SUBLANES)
            for g in range(ngrp):
                top, bot = slice(g * LANES, g * LANES + DH_D), slice(g * LANES + DH_D, (g + 1) * LANES)
                ot[q][top, :] = jnp.where(fresh, o_all[rows(g), 0:LANES], ot[q][top, :])
                ot[q][bot, :] = jnp.where(fresh, o_all[rows(g), LANES:2 * LANES], ot[q][bot, :])
        return carry

    lax.fori_loop(0, steps // SUBLANES, step_group, 0)

    for q, (s, drc) in enumerate(groups):
        for g in range(ngrp):
            o_refs[drc][s, :, g * LANES:(g + 1) * LANES] = ot[q][rows(g, LANES), :].T

    @pl.when(tb == pl.num_programs(1) - 1)
    def _():
        for q, (s, drc) in enumerate(groups):
            for g in range(ngrp):
                sfin_ref[s, drc, g] = st[q][rows(g), :]


def _rwkv_scan_call(r_src, w0, w1, nkk, bb, kt, s0, q1, wsp, *, n_seq, seq, row0):
    ns = RWKV_NS
    ntb = seq // RWKV_TB
    ngrp = D_D // LANES
    grp0 = row0 // (seq * ns)

    def view(t):
        return t.reshape(N_TOK // (seq * ns), ns, ntb, RWKV_TB, t.shape[-1])

    def tok(rev, cb=0):
        if rev:
            return pl.BlockSpec((None, ns, None, RWKV_TB, D_D), lambda b, t: (grp0 + b, 0, ntb - 1 - t, 0, cb))
        return pl.BlockSpec((None, ns, None, RWKV_TB, D_D), lambda b, t: (grp0 + b, 0, t, 0, cb))

    def out(rev):
        if rev:
            return pl.BlockSpec((None, ns, None, RWKV_TB, D_D), lambda b, t: (b, 0, ntb - 1 - t, 0, 0))
        return pl.BlockSpec((None, ns, None, RWKV_TB, D_D), lambda b, t: (b, 0, t, 0, 0))

    st_spec = pl.BlockSpec((ns, 2, ngrp, DH_D, LANES), lambda b, t: (b, 0, 0, 0, 0))
    in_specs = []
    for rev in (False, True):
        in_specs += [tok(rev, 0), tok(rev), tok(rev), tok(rev), tok(rev), tok(rev, 2)]
    in_specs += [st_spec, pl.BlockSpec((LANES, LANES), lambda b, t: (0, 0)),
                 pl.BlockSpec((SUBLANES * LANES, 2 * LANES), lambda b, t: (0, 0))]
    rv, w0v, w1v, nv, bv, kv = [view(t) for t in (r_src, w0, w1, nkk, bb, kt)]
    o_shape = jax.ShapeDtypeStruct((n_seq // ns, ns, ntb, RWKV_TB, D_D), F32)
    m_rows = ngrp * DH_D
    n_groups = ns * 2
    scratch = ([pltpu.VMEM((m_rows, LANES), F32)] * n_groups + [pltpu.VMEM((2 * m_rows, LANES), F32)] * n_groups
               + [pltpu.VMEM((2 * m_rows, LANES), BF16)] * n_groups
               + [pltpu.VMEM((m_rows, SUBLANES * LANES), BF16)] * n_groups)
    o_f, o_b, s_fin = pl.pallas_call(
        _rwkv_scan_kernel,
        grid=(n_seq // ns, ntb),
        in_specs=in_specs,
        out_specs=[out(False), out(True), st_spec],
        out_shape=[o_shape, o_shape, jax.ShapeDtypeStruct((n_seq, 2, ngrp, DH_D, LANES), F32)],
        scratch_shapes=scratch,
        compiler_params=_params(("arbitrary", "arbitrary")),
        name="rwkv_dec" if row0 else "rwkv_ctx",
    )(rv, w0v, nv, bv, kv, rv, rv, w1v, nv, bv, kv, rv, s0, q1, wsp)
    return o_f.reshape(n_seq * seq, D_D), o_b.reshape(n_seq * seq, D_D), s_fin


def _rwkv_post_kernel(ofl_ref, ofh_ref, obl_ref, obh_ref, rkv_ref, g_ref, lng_ref, lnb_ref, gm_ref, o_ref, *, tm):
    gm = gm_ref[...]
    o_all = _pick(tm, ofl_ref, ofh_ref) + _pick(tm, obl_ref, obh_ref)
    for j in range(D_D // LANES):
        sl = slice(j * LANES, (j + 1) * LANES)
        o = o_all[:, sl]
        dlt = o - _dot_b16(o, gm)
        var = _dot_b16(dlt * dlt, gm)
        y = dlt * lax.rsqrt(var + RWKV_GN_EPS) * lng_ref[:, sl] + lnb_ref[:, sl]
        o_ref[:, sl] = (y + rkv_ref[:, sl]) * g_ref[:, sl]


def _rwkv_post_call(o_f, o_b, rkv, g, ln_g, ln_b, gm):
    tm = 512
    tok = pl.BlockSpec((tm, D_D), lambda i: (i, 0))
    row = pl.BlockSpec((1, D_D), lambda i: (0, 0))
    return pl.pallas_call(
        functools.partial(_rwkv_post_kernel, tm=tm),
        grid=(N_TOK // tm,),
        in_specs=_pair_specs(tm, D_D, o_f) + _pair_specs(tm, D_D, o_b)
        + [tok, tok, row, row, pl.BlockSpec((LANES, LANES), lambda i: (0, 0))],
        out_specs=tok,
        out_shape=jax.ShapeDtypeStruct((N_TOK, D_D), F32),
        compiler_params=_params(("arbitrary",)),
        name="rwkv_post",
    )(o_f[0], o_f[1], o_b[0], o_b[1], rkv, g, ln_g, ln_b, gm)


def _rope_tables():
    pos = np.arange(DEC_SEQ)
    pr, pc = pos // GRID_W, pos % GRID_W
    lane = np.arange(LANES)
    dd = lane % DH_A
    use_col = (dd // 32) == 1
    j = dd % 16
    is_lo = (dd % 32) < 16
    freq = ROPE_THETA ** (-(j.astype(np.float64)) / 16.0)
    p = np.where(use_col[None, :], pc[:, None], pr[:, None]).astype(np.float64)
    ang = (p.astype(np.float32) * freq.astype(np.float32)[None, :]).astype(np.float32)
    cos = np.cos(ang).astype(np.float32)
    sin = np.sin(ang).astype(np.float32)
    s1 = np.where(is_lo[None, :], -sin, 0.0).astype(np.float32)
    s2 = np.where(is_lo[None, :], 0.0, sin).astype(np.float32)
    ident = 512
    cos = np.concatenate([cos, np.ones((ident, LANES), np.float32)], 0)
    s1 = np.concatenate([s1, np.zeros((ident, LANES), np.float32)], 0)
    s2 = np.concatenate([s2, np.zeros((ident, LANES), np.float32)], 0)
    return jnp.asarray(cos), jnp.asarray(s1), jnp.asarray(s2)


def _block_diag(value):
    m = np.zeros((LANES, LANES), np.float32)
    half = LANES // 2
    m[:half, :half] = value
    m[half:, half:] = value
    return jnp.asarray(m)


def _head_spread():
    step = np.arange(SUBLANES * LANES) // LANES
    head = (np.arange(SUBLANES * LANES) % LANES) // DH_D
    out_head = np.arange(2 * LANES) // LANES
    out_step = (np.arange(2 * LANES) % LANES) % SUBLANES
    m = (head[:, None] == out_head[None, :]) & (step[:, None] == out_step[None, :])
    return jnp.asarray(m.astype(np.float32))


def _qb_perm():
    idx = np.zeros(D_B, np.int32)
    for j in range(HB // 2):
        for hh in range(2):
            for dch in range(DH_B):
                idx[j * LANES + hh * DH_B + dch] = (hh * (HB // 2) + j) * DH_B + dch
    return idx


def kernel(x_prompt, x_sample, cache_a_k, cache_a_v, cache_b_k, cache_b_v, state_hgrn, state_rwkv, c, c_ctx, ada_w, ada_b, norm_mix_g, norm_ffn_g, final_norm_g, ev_w_in, ev_w_out, a_lambda, a_subln_g, b_q_norm_g, b_k_norm_g, od_w_in, od_w_out, hgrn_lb_logits, hgrn_norm_g, rwkv_mu, rwkv_w0, rwkv_w_up, rwkv_a0, rwkv_a_up, rwkv_g_up, rwkv_k_k, rwkv_k_a, rwkv_r_k, rwkv_ln_g, rwkv_ln_b, ffn_w_up, ffn_conv_w, ffn_conv_b, ffn_w_down):
    d = D_MODEL
    x = (x_prompt.reshape(N_CTX, d), x_sample.reshape(N_DEC, d), 0)
    cond =jnp.concatenate([c_ctx[None, :], c, jnp.zeros((COND_ROWS - 1 - DEC_BATCH, d), F32)], axis=0)
    mod = _mod_call(cond, ada_w, ada_b).reshape(DEPTH, COND_ROWS, 1, 6 * d)

    cos, s1, s2 = _rope_tables()
    g_mean = _block_diag(1.0 / DH_B)
    g_sum = _block_diag(1.0)
    perm = _qb_perm()
    row = lambda t: t.reshape(1, -1)

    new_ctx = None
    for l in range(DEPTH):
        i = l // 2
        mod_l = mod[l]
        if l % 2 == 0:
            w_in = ev_w_in[i]
            w_in = jnp.concatenate([w_in[:, :1536], w_in[:, 1536:2048][:, perm], w_in[:, 2048:]], axis=1).astype(BF16)
            w_out = ev_w_out[i]
            w_out = jnp.concatenate([w_out[:D_A], w_out[D_A:][perm]], axis=0).astype(BF16)
            qg = row(jnp.tile(b_q_norm_g[i], 2))
            kg = row(jnp.tile(b_k_norm_g[i], 2))
            qa, ka, va, qb, kb, vb = _even_proj_call(x, mod_l, row(norm_mix_g[l]), w_in, qg, kg, cos, s1, s2, g_mean)
            lam_init = 0.8 - 0.6 * math.exp(-0.3 * l)
            sub_g = row(a_subln_g[i])
            oa_c, ob_c = _attn_call(qa, qb, ka, va, kb, vb, None, a_lambda[i], sub_g, lam_init,
                                    n_seq=BATCH, seq=SEQ, row0=0)
            cbk = jnp.transpose(cache_b_k[:, i], (0, 2, 1, 3)).reshape(DEC_BATCH, PAST_LEN, HKV_B * DH_B)
            cbv = jnp.transpose(cache_b_v[:, i], (0, 2, 1, 3)).reshape(DEC_BATCH, PAST_LEN, HKV_B * DH_B)
            cache = (cache_a_k[:, i], cache_a_v[:, i], cbk, cbv)
            oa_s, ob_s = _attn_call(qa, qb, ka, va, kb, vb, cache, a_lambda[i], sub_g, lam_init,
                                    n_seq=DEC_BATCH, seq=DEC_SEQ, row0=N_CTX)
            mix_a = (oa_c, oa_s, 0)
            mix_b = (ob_c, ob_s, 0)

            def heads_first(t, nh):
                t = t[:N_CTX].reshape(BATCH, SEQ, nh, -1)
                return jnp.transpose(t, (0, 2, 1, 3))[:, None]

            even_ctx = (heads_first(ka, HA), heads_first(va, HA), heads_first(kb, HKV_B), heads_first(vb, HKV_B))
        else:
            w = od_w_in[i]
            w_h = w[:, :HGRN_PROJ].astype(BF16)
            wr = w[:, HGRN_PROJ:]
            mu = rwkv_mu[i]

            def rwkv_cols(t):
                z = jnp.zeros(t.shape[:-1] + (RWKV_PAD - RWKV_PROJ,), t.dtype)
                return jnp.concatenate([t[..., :1664], t[..., 1728:1856], t[..., 1664:1728], z], axis=-1)

            w_r = rwkv_cols(wr).astype(BF16)
            mu_r = row(rwkv_cols(mu))
            g_l = row(norm_mix_g[l])
            assert x[0] is x[1]
            proj_h = _odd_proj_call(x[0], mod_l, g_l, w_h, jnp.zeros((1, HGRN_PROJ), F32), shift=False)
            proj_r = _odd_proj_call(x[0], mod_l, g_l, w_r, mu_r, shift=True)

            ng = row(hgrn_norm_g[i])
            zero_h = jnp.zeros((BATCH, 2, HC, DK_C, DV_C), F32)
            ones = jnp.ones((LANES, LANES), BF16)
            oc_c, sh_c = _hgrn_call(proj_h, hgrn_lb_logits, ng, zero_h, ones, n_seq=BATCH, seq=SEQ, row0=0, lidx=l)
            oc_s, _ = _hgrn_call(proj_h, hgrn_lb_logits, ng, state_hgrn[:, i], ones, n_seq=DEC_BATCH, seq=DEC_SEQ,
                                 row0=N_CTX, lidx=l)

            pad_rows = lambda t: jnp.concatenate([t, jnp.zeros((LANES - t.shape[0], t.shape[1]), F32)], axis=0)
            wup0 = pad_rows(rwkv_w_up[i, 0])
            wup1 = jnp.concatenate([jnp.zeros((W_LORA, D_D), F32), rwkv_w_up[i, 1]], axis=0)
            w0o, w1o, nkk, bb, kt, gg, rkv = _rwkv_prep_call(
                proj_r, row(rwkv_a0[i]), pad_rows(rwkv_a_up[i]), rwkv_g_up[i], row(rwkv_k_k[i]), row(rwkv_k_a[i]),
                row(rwkv_r_k[i]), rwkv_w0[i], wup0, wup1, g_sum)

            def to_tiles(s):
                b = s.shape[0]
                s = s.reshape(b, 2, HD // 2, 2, DH_D, DH_D)
                return jnp.transpose(s, (0, 1, 2, 4, 3, 5)).reshape(b, 2, HD // 2, DH_D, LANES)

            def from_tiles(s):
                b = s.shape[0]
                s = s.reshape(b, 2, HD // 2, DH_D, 2, DH_D)
                return jnp.transpose(s, (0, 1, 2, 4, 3, 5)).reshape(b, 2, HD, DH_D, DH_D)

            zero_r = jnp.zeros((BATCH, 2, HD // 2, DH_D, LANES), F32)
            q1 = g_sum.astype(BF16)
            wsp = _head_spread().astype(BF16)
            of_c, ob_c, sr_c = _rwkv_scan_call(proj_r, w0o, w1o, nkk, bb, kt, zero_r, q1, wsp,
                                               n_seq=BATCH, seq=SEQ, row0=0)
            of_s, ob_s, _ = _rwkv_scan_call(proj_r, w0o, w1o, nkk, bb, kt, to_tiles(state_rwkv[:, i]), q1, wsp,
                                            n_seq=DEC_BATCH, seq=DEC_SEQ, row0=N_CTX)
            mix_b = _whole(_rwkv_post_call((of_c, of_s, 0), (ob_c, ob_s, 0), rkv, gg, row(rwkv_ln_g[i]),
                                           row(rwkv_ln_b[i]), g_mean))
            mix_a = (oc_c, oc_s, 0)
            w_out = od_w_out[i].astype(BF16)
            odd_ctx = (sh_c[:, None], from_tiles(sr_c)[:, None])

        x1 = _out_proj_call(x, mix_a, mix_b, mod_l, w_out)
        x2 = _ffn_call(x1, mod_l, row(norm_ffn_g[l]), ffn_w_up[l].astype(BF16), ffn_conv_w[l], row(ffn_conv_b[l]),
                       ffn_w_down[l].astype(BF16), row(final_norm_g), final=(l == DEPTH - 1))
        x = _whole(x2)

    y_prompt = x2[:N_CTX].reshape(BATCH, SEQ, d)
    y_sample = x2[N_CTX:].reshape(DEC_BATCH, DEC_SEQ, d)
    return (y_prompt, y_sample) + even_ctx + odd_ctx
```

```python
import functools
import math

import jax
import jax.numpy as jnp
import numpy as np
from jax import lax
from jax.experimental import pallas as pl
from jax.experimental.pallas import tpu as pltpu

D_MODEL = 1024
BATCH = 16
SEQ = 256
DEPTH = 2
DEC_BATCH = 4
DEC_SEQ = 1024
PAST_LEN = 512
GRID_W = 64
ROPE_THETA = 10000.0
EPS = 1e-6
RWKV_GN_EPS = 64e-5
HA = 4
DH_A = 64
DV_A = 2 * DH_A
HB = 8
HKV_B = 2
DH_B = 64
HC = 4
DK_C = 128
DV_C = 128
HD = 8
DH_D = 64
W_LORA = 64
A_LORA = 64
G_LORA = 128
D_FF = 2816

D_A = HA * DV_A
D_B = HB * DH_B
D_C = HC * DV_C
D_D = HD * DH_D
EVEN_PROJ = 2304
HGRN_PROJ = 2560
RWKV_PROJ = 1856
RWKV_PAD = 1920

N_CTX = BATCH * SEQ
N_DEC = DEC_BATCH * DEC_SEQ
N_TOK = N_CTX + N_DEC
COND_ROWS = 8

LANES = 128
SUBLANES = 8
VMEM_LIMIT = 56 * 1024 * 1024

F32 = jnp.float32
BF16 = jnp.bfloat16
HI = lax.Precision.HIGHEST

FF_TILE = 256
HGRN_L = 64
HGRN_HP = 2
RWKV_TB = 128
RWKV_NS = 4
RWKV_GROUP = 1
RWKV_LAG = 3
assert RWKV_LAG < 2 * RWKV_NS // RWKV_GROUP


def _dot(a, b):
    return jnp.dot(a, b, preferred_element_type=F32)


def _dot_hi(a, b):
    return jnp.dot(a, b, preferred_element_type=F32, precision=HI)


def _dot_b16(a, b):
    return _dot(a.astype(BF16), b.astype(BF16))


def _dot_nt(a, b, precision=None):
    return lax.dot_general(a, b, (((1,), (1,)), ((), ())), preferred_element_type=F32, precision=precision)


def _dot_tn(a, b, precision=None):
    return lax.dot_general(a, b, (((0,), (0,)), ((), ())), preferred_element_type=F32, precision=precision)


def _sigmoid(x):
    return 1.0 / (1.0 + jnp.exp(-x))


def _silu(x):
    return x * _sigmoid(x)


def _norm_mod(x, g, sc, sh):
    ms = jnp.mean(x * x, axis=-1, keepdims=True)
    return (x * lax.rsqrt(ms + EPS) * g) * (1.0 + sc) + sh


def _params(sem):
    return pltpu.CompilerParams(dimension_semantics=sem, vmem_limit_bytes=VMEM_LIMIT)


def _cond_row(i, tm):
    r0 = i * tm
    return jnp.where(r0 < N_CTX, 0, 1 + (r0 - N_CTX) // DEC_SEQ)


def _seq_len(i, tm):
    return jnp.where(i * tm < N_CTX, SEQ, DEC_SEQ)


ROW_PAD = SUBLANES


def _zero_row_pads(u_ref, tm):
    for k in range(u_ref.shape[0]):
        u_ref[k, 0:ROW_PAD, :] = jnp.zeros((ROW_PAD, u_ref.shape[2]), u_ref.dtype)
        u_ref[k, ROW_PAD + tm:2 * ROW_PAD + tm, :] = jnp.zeros((ROW_PAD, u_ref.shape[2]), u_ref.dtype)


def _neighbour_rows(u_ref, k, c, ch, seq_len):
    r0 = ROW_PAD + c * ch
    row8 = lax.broadcasted_iota(jnp.int32, (SUBLANES, 1), 0)
    at_start = (c * ch) % seq_len == 0
    at_end = ((c + 1) * ch) % seq_len == 0
    prev = u_ref[k, r0 - 1:r0 - 1 + ch, :]
    nxt = u_ref[k, r0 + 1:r0 + 1 + ch, :]
    prev = jnp.concatenate([jnp.where((row8 == 0) & at_start, 0.0, prev[0:SUBLANES]), prev[SUBLANES:]], axis=0)
    nxt = jnp.concatenate(
        [nxt[:ch - SUBLANES], jnp.where((row8 == SUBLANES - 1) & at_end, 0.0, nxt[ch - SUBLANES:])], axis=0)
    return prev, nxt


def _pick(tm, lo_ref, hi_ref, rows=slice(None)):
    return jnp.where(pl.program_id(0) * tm < N_CTX, lo_ref[rows, :], hi_ref[rows, :])


def _pair_specs(tm, width, pair):
    nct = N_CTX // tm
    hi0 = pair[2] // tm
    return [pl.BlockSpec((tm, width), lambda i: (jnp.minimum(i, nct - 1), 0)),
            pl.BlockSpec((tm, width), lambda i: (hi0 + jnp.maximum(i - nct, 0), 0))]


def _whole(t):
    return (t, t, N_CTX)


def _mod_kernel(c_ref, w_ref, b_ref, o_ref):
    s = _silu(c_ref[...]).astype(BF16)
    o_ref[...] = _dot(s, w_ref[...].astype(BF16)) + b_ref[...]


def _mod_call(cond, ada_w, ada_b):
    tn = 1536
    n = 6 * D_MODEL
    return pl.pallas_call(
        _mod_kernel,
        grid=(DEPTH, n // tn),
        in_specs=[
            pl.BlockSpec((COND_ROWS, D_MODEL), lambda l, j: (0, 0)),
            pl.BlockSpec((None, D_MODEL, tn), lambda l, j: (l, 0, j)),
            pl.BlockSpec((None, 1, tn), lambda l, j: (l, 0, j)),
        ],
        out_specs=pl.BlockSpec((None, COND_ROWS, tn), lambda l, j: (l, 0, j)),
        out_shape=jax.ShapeDtypeStruct((DEPTH, COND_ROWS, n), F32),
        compiler_params=_params(("arbitrary", "arbitrary")),
        name="ada_mod",
    )(cond, ada_w, ada_b.reshape(DEPTH, 1, n))


def _even_proj_kernel(xl_ref, xh_ref, mod_ref, g_ref, w_ref, qg_ref, kg_ref, cos_ref, s1_ref, s2_ref, gm_ref,
                      qa_ref, ka_ref, va_ref, qb_ref, kb_ref, vb_ref, *, tm):
    d = D_MODEL
    gm = gm_ref[...]
    scale = DH_A ** -0.5

    def head_norm(t, g):
        return t * lax.rsqrt(_dot_b16(t * t, gm) + EPS) * g

    for c in range(tm // SEQ):
        rows = slice(c * SEQ, (c + 1) * SEQ)
        h = _norm_mod(_pick(tm, xl_ref, xh_ref, rows), g_ref[...], mod_ref[:, d:2 * d], mod_ref[:, 0:d]).astype(BF16)
        proj = _dot(h, w_ref[...])
        cos, s1, s2 = cos_ref[rows, :], s1_ref[rows, :], s2_ref[rows, :]

        def rope(t):
            return t * cos + pltpu.roll(t, LANES - 16, 1) * s1 + pltpu.roll(t, 16, 1) * s2

        for j in range(4):
            sl = slice(j * LANES, (j + 1) * LANES)
            qa_ref[rows, sl] = rope(proj[:, j * LANES:(j + 1) * LANES]) * scale
            ka_ref[rows, sl] = rope(proj[:, 512 + j * LANES:512 + (j + 1) * LANES])
            va_ref[rows, sl] = proj[:, 1024 + j * LANES:1024 + (j + 1) * LANES]
            qb_ref[rows, sl] = rope(head_norm(proj[:, 1536 + j * LANES:1536 + (j + 1) * LANES], qg_ref[...])) * scale
        kb_ref[rows, :] = rope(head_norm(proj[:, 2048:2176], kg_ref[...]))
        vb_ref[rows, :] = proj[:, 2176:2304]


def _even_proj_call(x, mod_l, g, w, qg, kg, cos, s1, s2, gm):
    tm = 512
    nt = N_TOK // tm
    n_rope_blk = DEC_SEQ // tm

    def rope_idx(i):
        return (jnp.where(i * tm < N_CTX, n_rope_blk, (i - N_CTX // tm) % n_rope_blk), 0)

    full = lambda shape: pl.BlockSpec(shape, lambda i: (0,) * len(shape))
    out512 = pl.BlockSpec((tm, 512), lambda i: (i, 0))
    out128 = pl.BlockSpec((tm, LANES), lambda i: (i, 0))
    return pl.pallas_call(
        functools.partial(_even_proj_kernel, tm=tm),
        grid=(nt,),
        in_specs=_pair_specs(tm, D_MODEL, x) + [
            pl.BlockSpec((None, 1, 6 * D_MODEL), lambda i: (_cond_row(i, tm), 0, 0)),
            full((1, D_MODEL)),
            full((D_MODEL, EVEN_PROJ)),
            full((1, LANES)),
            full((1, LANES)),
            pl.BlockSpec((tm, LANES), rope_idx),
            pl.BlockSpec((tm, LANES), rope_idx),
            pl.BlockSpec((tm, LANES), rope_idx),
            full((LANES, LANES)),
        ],
        out_specs=[out512, out512, out512, out512, out128, out128],
        out_shape=[jax.ShapeDtypeStruct((N_TOK, 512), F32)] * 4 + [jax.ShapeDtypeStruct((N_TOK, LANES), F32)] * 2,
        compiler_params=_params(("arbitrary",)),
        name="even_proj",
    )(x[0], x[1], mod_l, g, w, qg, kg, cos, s1, s2, gm)


def _softmax_pv(q, ks, vs):
    ss = [_dot_nt(q, k) for k in ks]
    m = functools.reduce(jnp.maximum, [jnp.max(s, axis=-1, keepdims=True) for s in ss])
    ps = [jnp.exp(s - m) for s in ss]
    l = functools.reduce(jnp.add, [jnp.sum(p, axis=-1, keepdims=True) for p in ps])
    acc = functools.reduce(jnp.add, [_dot(p.astype(BF16), v) for p, v in zip(ps, vs)])
    return acc / l


def _attn_kernel(*refs, has_cache, lam_init):
    if has_cache:
        (qa_ref, qb_ref, ka_ref, va_ref, kb_ref, vb_ref, cak_ref, cav_ref, cbk_ref, cbv_ref,
         al_ref, sg_ref, oa_ref, ob_ref) = refs
    else:
        qa_ref, qb_ref, ka_ref, va_ref, kb_ref, vb_ref, al_ref, sg_ref, oa_ref, ob_ref = refs
    al = al_ref[...]
    lam = (jnp.exp(jnp.sum(al[0:1] * al[1:2], axis=-1, keepdims=True))
           - jnp.exp(jnp.sum(al[2:3] * al[3:4], axis=-1, keepdims=True)) + lam_init)
    lo = lax.broadcasted_iota(jnp.int32, (1, LANES), 1) < DH_A

    for h in range(HA):
        sl = slice(h * LANES, (h + 1) * LANES)
        q = qa_ref[:, sl]
        ks = [ka_ref[:, sl].astype(BF16)]
        vs = [va_ref[:, sl].astype(BF16)]
        if has_cache:
            ks.insert(0, cak_ref[h].astype(BF16))
            vs.insert(0, cav_ref[h].astype(BF16))
        a1 = _softmax_pv(jnp.where(lo, q, 0.0).astype(BF16), ks, vs)
        a2 = _softmax_pv(jnp.where(lo, 0.0, q).astype(BF16), ks, vs)
        dlt = a1 - lam * a2
        ms = jnp.mean(dlt * dlt, axis=-1, keepdims=True)
        oa_ref[:, sl] = dlt * lax.rsqrt(ms + EPS) * sg_ref[...] * (1.0 - lam_init)

    ks = [kb_ref[...].astype(BF16)]
    vs = [vb_ref[...].astype(BF16)]
    if has_cache:
        ks.insert(0, cbk_ref[...].astype(BF16))
        vs.insert(0, cbv_ref[...].astype(BF16))
    for j in range(HB // 2):
        sl = slice(j * LANES, (j + 1) * LANES)
        q = qb_ref[:, sl]
        o0 = _softmax_pv(jnp.where(lo, q, 0.0).astype(BF16), ks, vs)
        o1 = _softmax_pv(jnp.where(lo, 0.0, q).astype(BF16), ks, vs)
        ob_ref[:, sl] = jnp.where(lo, o0, o1)


def _attn_call(qa, qb, ka, va, kb, vb, cache, a_lambda, subln_g, lam_init, *, n_seq, seq, row0):
    tq = 256
    nq = seq // tq
    qblk0 = row0 // tq
    sblk0 = row0 // seq
    has_cache = cache is not None
    qspec = pl.BlockSpec((tq, 512), lambda b, i: (qblk0 + b * nq + i, 0))
    own512 = pl.BlockSpec((seq, 512), lambda b, i: (sblk0 + b, 0))
    own128 = pl.BlockSpec((seq, LANES), lambda b, i: (sblk0 + b, 0))
    in_specs = [qspec, qspec, own512, own512, own128, own128]
    args = [qa, qb, ka, va, kb, vb]
    if has_cache:
        in_specs += [
            pl.BlockSpec((None, HA, PAST_LEN, LANES), lambda b, i: (b, 0, 0, 0)),
            pl.BlockSpec((None, HA, PAST_LEN, LANES), lambda b, i: (b, 0, 0, 0)),
            pl.BlockSpec((None, PAST_LEN, LANES), lambda b, i: (b, 0, 0)),
            pl.BlockSpec((None, PAST_LEN, LANES), lambda b, i: (b, 0, 0)),
        ]
        args += list(cache)
    in_specs += [pl.BlockSpec((4, DH_A), lambda b, i: (0, 0)), pl.BlockSpec((1, LANES), lambda b, i: (0, 0))]
    args += [a_lambda, subln_g]
    ospec = pl.BlockSpec((tq, 512), lambda b, i: (b * nq + i, 0))
    return pl.pallas_call(
        functools.partial(_attn_kernel, has_cache=has_cache, lam_init=lam_init),
        grid=(n_seq, nq),
        in_specs=in_specs,
        out_specs=[ospec, ospec],
        out_shape=[jax.ShapeDtypeStruct((n_seq * seq, 512), F32)] * 2,
        compiler_params=_params(("arbitrary", "arbitrary")),
        name="attn_dec" if has_cache else "attn_ctx",
    )(*args)


def _out_proj_kernel(xl_ref, xh_ref, al_ref, ah_ref, bl_ref, bh_ref, mod_ref, w_ref, o_ref, *, tm):
    d = D_MODEL
    half = al_ref.shape[1]
    a = _pick(tm, al_ref, ah_ref).astype(BF16)
    b = _pick(tm, bl_ref, bh_ref).astype(BF16)
    mix = _dot(a, w_ref[0:half, :]) + _dot(b, w_ref[half:2 * half, :])
    o_ref[...] = _pick(tm, xl_ref, xh_ref) + mod_ref[:, 2 * d:3 * d] * mix


def _out_proj_call(x, a, b, mod_l, w):
    tm = 512
    return pl.pallas_call(
        functools.partial(_out_proj_kernel, tm=tm),
        grid=(N_TOK // tm,),
        in_specs=_pair_specs(tm, D_MODEL, x) + _pair_specs(tm, 512, a) + _pair_specs(tm, 512, b) + [
            pl.BlockSpec((None, 1, 6 * D_MODEL), lambda i: (_cond_row(i, tm), 0, 0)),
            pl.BlockSpec((D_MODEL, D_MODEL), lambda i: (0, 0)),
        ],
        out_specs=pl.BlockSpec((tm, D_MODEL), lambda i: (i, 0)),
        out_shape=jax.ShapeDtypeStruct((N_TOK, D_MODEL), F32),
        compiler_params=_params(("arbitrary",)),
        name="out_proj",
    )(x[0], x[1], a[0], a[1], b[0], b[1], mod_l, w)


def _ffn_kernel(x_ref, mod_ref, g_ref, wv_ref, wg_ref, cwv_ref, cwg_ref, cbv_ref, cbg_ref, wd_ref, fg_ref,
                o_ref, h_scr, acc_scr, u_scr, *, tm, final):
    d = D_MODEL
    i, j = pl.program_id(0), pl.program_id(1)
    ch = SEQ

    @pl.when(j == 0)
    def _():
        h_scr[...] = _norm_mod(x_ref[...], g_ref[...], mod_ref[:, 4 * d:5 * d], mod_ref[:, 3 * d:4 * d]).astype(BF16)
        acc_scr[...] = jnp.zeros_like(acc_scr)
        _zero_row_pads(u_scr, tm)

    seq_len = _seq_len(i, tm)

    def project(c):
        rows = slice(c * ch, (c + 1) * ch)
        u_scr[0, ROW_PAD + c * ch:ROW_PAD + (c + 1) * ch, :] = _dot(h_scr[rows, :], wv_ref[...])
        u_scr[1, ROW_PAD + c * ch:ROW_PAD + (c + 1) * ch, :] = _dot(h_scr[rows, :], wg_ref[...])

    def conv(k, c, cw_ref, cb_ref):
        prev, nxt = _neighbour_rows(u_scr, k, c, ch, seq_len)
        cur = u_scr[k, ROW_PAD + c * ch:ROW_PAD + (c + 1) * ch, :]
        return cw_ref[0:1, :] * prev + cw_ref[1:2, :] * cur + cw_ref[2:3, :] * nxt + cb_ref[...]

    def activate(c):
        val = conv(0, c, cwv_ref, cbv_ref)
        gate = conv(1, c, cwg_ref, cbg_ref)
        act = (_silu(gate) * val).astype(BF16)
        acc_scr[c * ch:(c + 1) * ch, :] += _dot(act, wd_ref[...])

    n_chunks = tm // ch
    project(0)
    for c in range(n_chunks):
        if c + 1 < n_chunks:
            project(c + 1)
        activate(c)

    @pl.when(j == pl.num_programs(1) - 1)
    def _():
        y = x_ref[...] + mod_ref[:, 5 * d:6 * d] * acc_scr[...]
        if final:
            ms = jnp.mean(y * y, axis=-1, keepdims=True)
            y = y * lax.rsqrt(ms + EPS) * fg_ref[...]
        o_ref[...] = y


def _ffn_call(x, mod_l, g, w_up, conv_w, conv_b, w_down, final_g, *, final):
    tm = DEC_SEQ
    nf = D_FF // FF_TILE
    up = lambda j: j
    down = lambda j: j
    return pl.pallas_call(
        functools.partial(_ffn_kernel, tm=tm, final=final),
        grid=(N_TOK // tm, nf),
        in_specs=[
            pl.BlockSpec((tm, D_MODEL), lambda i, j: (i, 0)),
            pl.BlockSpec((None, 1, 6 * D_MODEL), lambda i, j: (_cond_row(i, tm), 0, 0)),
            pl.BlockSpec((1, D_MODEL), lambda i, j: (0, 0)),
            pl.BlockSpec((D_MODEL, FF_TILE), lambda i, j: (0, up(j))),
            pl.BlockSpec((D_MODEL, FF_TILE), lambda i, j: (0, nf + up(j))),
            pl.BlockSpec((3, FF_TILE), lambda i, j: (0, up(j))),
            pl.BlockSpec((3, FF_TILE), lambda i, j: (0, nf + up(j))),
            pl.BlockSpec((1, FF_TILE), lambda i, j: (0, up(j))),
            pl.BlockSpec((1, FF_TILE), lambda i, j: (0, nf + up(j))),
            pl.BlockSpec((FF_TILE, D_MODEL), lambda i, j: (down(j), 0)),
            pl.BlockSpec((1, D_MODEL), lambda i, j: (0, 0)),
        ],
        out_specs=pl.BlockSpec((tm, D_MODEL), lambda i, j: (i, 0)),
        out_shape=jax.ShapeDtypeStruct((N_TOK, D_MODEL), F32),
        scratch_shapes=[pltpu.VMEM((tm, D_MODEL), BF16), pltpu.VMEM((tm, D_MODEL), F32),
                        pltpu.VMEM((2, tm + 2 * SUBLANES, FF_TILE), F32)],
        compiler_params=_params(("arbitrary", "arbitrary")),
        name="conv_ffn",
    )(x, mod_l, g, w_up, w_up, conv_w, conv_w, conv_b, conv_b, w_down, final_g)


def _odd_proj_kernel(x_ref, mod_ref, g_ref, w_ref, mu_ref, o_ref, h_scr, *u_scr, tm, shift):
    d = D_MODEL
    i, j = pl.program_id(0), pl.program_id(1)
    ch = SEQ
    n_chunks = tm // ch

    @pl.when(j == 0)
    def _():
        h_scr[...] = _norm_mod(x_ref[...], g_ref[...], mod_ref[:, d:2 * d], mod_ref[:, 0:d]).astype(BF16)
        if shift:
            _zero_row_pads(u_scr[0], tm)

    if not shift:
        for c in range(n_chunks):
            o_ref[c * ch:(c + 1) * ch, :] = _dot(h_scr[c * ch:(c + 1) * ch, :], w_ref[...])
        return

    u = u_scr[0]
    seq_len = _seq_len(i, tm)

    def project(c):
        u[0, ROW_PAD + c * ch:ROW_PAD + (c + 1) * ch, :] = _dot(h_scr[c * ch:(c + 1) * ch, :], w_ref[...])

    project(0)
    for c in range(n_chunks):
        if c + 1 < n_chunks:
            project(c + 1)
        prev, nxt = _neighbour_rows(u, 0, c, ch, seq_len)
        p = u[0, ROW_PAD + c * ch:ROW_PAD + (c + 1) * ch, :]
        o_ref[c * ch:(c + 1) * ch, :] = p + mu_ref[...] * (0.5 * (prev + nxt) - p)


def _odd_proj_call(x, mod_l, g, w, mu, *, shift):
    tm = DEC_SEQ
    tn = 640
    n = w.shape[1]
    return pl.pallas_call(
        functools.partial(_odd_proj_kernel, tm=tm, shift=shift),
        grid=(N_TOK // tm, n // tn),
        in_specs=[
            pl.BlockSpec((tm, D_MODEL), lambda i, j: (i, 0)),
            pl.BlockSpec((None, 1, 6 * D_MODEL), lambda i, j: (_cond_row(i, tm), 0, 0)),
            pl.BlockSpec((1, D_MODEL), lambda i, j: (0, 0)),
            pl.BlockSpec((D_MODEL, tn), lambda i, j: (0, j)),
            pl.BlockSpec((1, tn), lambda i, j: (0, j)),
        ],
        out_specs=pl.BlockSpec((tm, tn), lambda i, j: (i, j)),
        out_shape=jax.ShapeDtypeStruct((N_TOK, n), F32),
        scratch_shapes=[pltpu.VMEM((tm, D_MODEL), BF16)]
        + ([pltpu.VMEM((1, tm + 2 * ROW_PAD, tn), F32)] if shift else []),
        compiler_params=_params(("arbitrary", "arbitrary")),
        name="odd_proj_rwkv" if shift else "odd_proj_hgrn",
    )(x, mod_l, g, w, mu)


def _hgrn_kernel(q_ref, ff_ref, fb_ref, v_ref, gc_ref, lbl_ref, ng_ref, s0_ref, ones_ref, o_ref, sfin_ref,
                 oacc_f, oacc_b, st_scr, qd_scr, add_scr, dec_scr, start_scr, *, seq, lidx):
    L = HGRN_L
    nc = seq // L
    nb = L // SUBLANES
    ones = ones_ref[...]
    rowi = lax.broadcasted_iota(jnp.int32, (L, L), 0)
    coli = lax.broadcasted_iota(jnp.int32, (L, L), 1)
    rowv = lax.broadcasted_iota(jnp.int32, (L, 1), 0)
    same_block = (rowi // SUBLANES) == (coli // SUBLANES)

    def lower_bound(drc):
        lg = lbl_ref[drc]
        e = jnp.exp(lg - jnp.max(lg, axis=0, keepdims=True))
        sm = e / jnp.sum(e, axis=0, keepdims=True)
        return functools.reduce(jnp.add, [sm[i:i + 1, :] for i in range(1, lidx + 1)])

    def spread(t, s_l):
        return jnp.concatenate(
            [jnp.broadcast_to(t[b * SUBLANES + s_l:b * SUBLANES + s_l + 1, :], (SUBLANES, LANES)) for b in range(nb)],
            axis=0)

    def chunk(rev, r0, f_ref, lb, tri, hs):
        q = _silu(q_ref[pl.ds(r0, L), hs])
        f = lb + (1.0 - lb) * _sigmoid(f_ref[pl.ds(r0, L), hs])
        k = 1.0 - f
        lf = jnp.log(f)
        v = v_ref[pl.ds(r0, L), hs].astype(BF16)
        cum = _dot_hi(tri, lf)
        cum_ex = cum - lf
        q_dec = (q * jnp.exp(cum)).astype(BF16)

        scores = jnp.zeros((L, L), F32)
        h = L // 2
        while h >= SUBLANES:
            pieces = []
            for a in range(0, L, 2 * h):
                edge = a + h - 1 if rev else a + h
                pieces.append(jnp.broadcast_to(cum_ex[edge:edge + 1, :], (2 * h, LANES)))
            anchor = jnp.concatenate(pieces, axis=0) if len(pieces) > 1 else pieces[0]
            is_q = ((rowv // h) % 2) == (0 if rev else 1)
            d = cum - anchor
            e = jnp.exp(jnp.where(is_q, d, -d))
            qe = jnp.where(is_q, q * e, 0.0).astype(BF16)
            ke = jnp.where(is_q, 0.0, k * e).astype(BF16)
            same_pair = (rowi // (2 * h)) == (coli // (2 * h))
            scores = scores + jnp.where(same_pair, _dot_nt(qe, ke), 0.0)
            h //= 2

        prods = []
        for s_l in range(SUBLANES):
            e = jnp.exp(jnp.minimum(cum - spread(cum, s_l), 0.0))
            prods.append(q * spread(k, s_l) * e)
        diag = _dot(jnp.concatenate(prods, axis=0).astype(BF16), ones)
        for s_l in range(SUBLANES):
            ok = ((rowi % SUBLANES) <= s_l) if rev else ((rowi % SUBLANES) >= s_l)
            take = same_block & ((coli % SUBLANES) == s_l) & ok
            scores = jnp.where(take, diag[s_l * L:(s_l + 1) * L, 0:L], scores)
        o = _dot(scores.astype(BF16), v)

        end = 0 if rev else L - 1
        cend = cum[end:end + 1, :]
        kd = (k * jnp.exp(cend - cum)).astype(BF16)
        return q_dec, o, _dot_tn(v, kd), jnp.exp(cend)

    lb_f, lb_b = lower_bound(0), lower_bound(1)
    tri_f = jnp.where(coli <= rowi, 1.0, 0.0).astype(F32)
    tri_b = jnp.where(coli >= rowi, 1.0, 0.0).astype(F32)
    heads = [slice(hh * LANES, (hh + 1) * LANES) for hh in range(HGRN_HP)]
    chains = [(hh, hs, drc) for hh, hs in enumerate(heads) for drc in range(2)]
    oaccs = (oacc_f, oacc_b)

    def local(ci, carry):
        r0 = pl.multiple_of(ci * L, L)
        for c, (hh, hs, drc) in enumerate(chains):
            q_dec, o, add, dec = chunk(drc == 1, r0, fb_ref if drc else ff_ref, (lb_b if drc else lb_f)[:, hs],
                                       tri_b if drc else tri_f, hs)
            qd_scr[c, pl.ds(r0, L), :] = q_dec
            oaccs[drc][pl.ds(r0, L), hs] = o
            add_scr[c * nc + ci] = add
            dec_scr[c * nc + ci] = jnp.broadcast_to(dec, (SUBLANES, LANES))
        return carry

    lax.fori_loop(0, nc, local, 0, unroll=2)

    for c, (hh, hs, drc) in enumerate(chains):
        st_scr[c] = s0_ref[drc, hh].T

    def scan(n, carry):
        for c, (hh, hs, drc) in enumerate(chains):
            slot = c * nc + ((nc - 1 - n) if drc else n)
            st = st_scr[c]
            start_scr[slot] = st.astype(BF16)
            st_scr[c] = st * dec_scr[slot][0:1, :] + add_scr[slot]
        return carry

    lax.fori_loop(0, nc, scan, 0)

    def carried(ci, carry):
        r0 = pl.multiple_of(ci * L, L)
        for c, (hh, hs, drc) in enumerate(chains):
            oaccs[drc][pl.ds(r0, L), hs] += _dot_nt(qd_scr[c, pl.ds(r0, L), :], start_scr[c * nc + ci])
        return carry

    lax.fori_loop(0, nc, carried, 0)

    for hh, hs in enumerate(heads):
        sfin_ref[0, hh] = st_scr[2 * hh].T
        sfin_ref[1, hh] = st_scr[2 * hh + 1].T
        o = oacc_f[:, hs] + oacc_b[:, hs]
        ms = jnp.mean(o * o, axis=-1, keepdims=True)
        o_ref[:, hs] = o * lax.rsqrt(ms + EPS) * ng_ref[...] * _silu(gc_ref[:, hs])


def _hgrn_call(proj_h, lb_logits, norm_g, s0, ones, *, n_seq, seq, row0, lidx):
    sblk0 = row0 // seq

    hp = HGRN_HP
    width = hp * LANES

    def col(section):
        return pl.BlockSpec((seq, width), lambda b, h: (sblk0 + b, section * (HC // hp) + h))

    st_spec = pl.BlockSpec((None, 2, hp, DK_C, DV_C), lambda b, h: (b, 0, h, 0, 0))
    return pl.pallas_call(
        functools.partial(_hgrn_kernel, seq=seq, lidx=lidx),
        grid=(n_seq, HC // hp),
        in_specs=[
            col(0), col(1), col(2), col(3), col(4),
            pl.BlockSpec((2, DEPTH, width), lambda b, h: (0, 0, h)),
            pl.BlockSpec((1, LANES), lambda b, h: (0, 0)),
            st_spec,
            pl.BlockSpec((LANES, LANES), lambda b, h: (0, 0)),
        ],
        out_specs=[pl.BlockSpec((seq, width), lambda b, h: (b, h)), st_spec],
        out_shape=[jax.ShapeDtypeStruct((n_seq * seq, D_C), F32),
                   jax.ShapeDtypeStruct((n_seq, 2, HC, DK_C, DV_C), F32)],
        scratch_shapes=[pltpu.VMEM((seq, width), F32), pltpu.VMEM((seq, width), F32),
                        pltpu.VMEM((2 * hp, DV_C, DK_C), F32),
                        pltpu.VMEM((2 * hp, seq, DK_C), BF16),
                        pltpu.VMEM((2 * hp * (seq // HGRN_L), DV_C, DK_C), F32),
                        pltpu.VMEM((2 * hp * (seq // HGRN_L), SUBLANES, DK_C), F32),
                        pltpu.VMEM((2 * hp * (seq // HGRN_L), DV_C, DK_C), BF16)],
        compiler_params=_params(("arbitrary", "arbitrary")),
        name="hgrn_dec" if row0 else "hgrn_ctx",
    )(proj_h, proj_h, proj_h, proj_h, proj_h, lb_logits, norm_g, s0, ones)


def _rwkv_prep_kernel(p_ref, a0_ref, aup_ref, gup_ref, kk_ref, ka_ref, rk_ref, w0_ref, wup0_ref, wup1_ref,
                      gs_ref, w0o_ref, w1o_ref, nkk_ref, bb_ref, kt_ref, g_ref, rkv_ref):
    r = p_ref[:, 0:512]
    k = p_ref[:, 512:1024]
    v = p_ref[:, 1024:1536]
    wd = p_ref[:, 1536:1664]
    gd = p_ref[:, 1664:1792]
    ad = p_ref[:, 1792:1920]
    gs = gs_ref[...]

    def group_sum(t):
        return jnp.concatenate([_dot_b16(t[:, j * LANES:(j + 1) * LANES], gs) for j in range(4)], axis=1)

    a = _sigmoid(a0_ref[...] + _dot_b16(ad, aup_ref[...]))
    g_ref[...] = _dot_b16(_sigmoid(gd), gup_ref[...])
    kkr = k * kk_ref[...]
    kk = kkr / jnp.maximum(jnp.sqrt(group_sum(kkr * kkr)), 1e-12)
    kt = k * (1.0 + (a - 1.0) * ka_ref[...])
    th = jnp.tanh(wd)
    decay = math.exp(-0.5)
    w0o_ref[...] = jnp.exp(-decay * _sigmoid(w0_ref[0:1, :] + _dot_b16(th, wup0_ref[...])))
    w1o_ref[...] = jnp.exp(-decay * _sigmoid(w0_ref[1:2, :] + _dot_b16(th, wup1_ref[...])))
    nkk_ref[...] = -kk
    bb_ref[...] = kk * a
    kt_ref[...] = kt
    rkv_ref[...] = group_sum(r * kt * rk_ref[...]) * v


def _rwkv_prep_call(proj_r, a0, aup, gup, kk_k, k_a, r_k, w0, wup0, wup1, gs):
    tm = 512
    full = lambda shape: pl.BlockSpec(shape, lambda i: (0,) * len(shape))
    ospec = pl.BlockSpec((tm, D_D), lambda i: (i, 0))
    return pl.pallas_call(
        _rwkv_prep_kernel,
        grid=(N_TOK // tm,),
        in_specs=[
            pl.BlockSpec((tm, RWKV_PAD), lambda i: (i, 0)),
            full((1, D_D)), full((LANES, D_D)), full((LANES, D_D)), full((1, D_D)), full((1, D_D)), full((1, D_D)),
            full((2, D_D)), full((LANES, D_D)), full((LANES, D_D)), full((LANES, LANES)),
        ],
        out_specs=[ospec] * 7,
        out_shape=[jax.ShapeDtypeStruct((N_TOK, D_D), F32)] * 7,
        compiler_params=_params(("arbitrary",)),
        name="rwkv_prep",
    )(proj_r, a0, aup, gup, kk_k, k_a, r_k, w0, wup0, wup1, gs)


def _rwkv_scan_kernel(rf_ref, wf_ref, nf_ref, bf_ref, kf_ref, vf_ref, rb_ref, wb_ref, nb_ref, bb_ref, kb_ref,
                      vb_ref, s0_ref, q1_ref, wsp_ref, of_ref, ob_ref, sfin_ref, *scratch):
    tb = pl.program_id(1)
    ns = RWKV_NS
    ngrp = D_D // LANES
    steps = RWKV_TB
    members = [(s, drc) for s in range(ns) for drc in range(2)]
    groups = [members[i:i + RWKV_GROUP] for i in range(0, len(members), RWKV_GROUP)]
    st, ot, pabuf, obuf = (scratch[i * len(groups):(i + 1) * len(groups)] for i in range(4))
    m_rows = RWKV_GROUP * ngrp * DH_D
    dirs = ((rf_ref, wf_ref, nf_ref, bf_ref, kf_ref, vf_ref), (rb_ref, wb_ref, nb_ref, bb_ref, kb_ref, vb_ref))
    o_refs = (of_ref, ob_ref)

    def rows(c, n=DH_D, base=0):
        return slice(base + c * n, base + (c + 1) * n)

    def chains(q):
        return [(m * ngrp + g, m, s, drc, g) for m, (s, drc) in enumerate(groups[q]) for g in range(ngrp)]

    @pl.when(tb == 0)
    def _():
        for q in range(len(groups)):
            for c, _, s, drc, g in chains(q):
                st[q][rows(c), :] = s0_ref[s, drc, g]

    for q in range(len(groups)):
        ot[q][...] = jnp.zeros_like(ot[q])
    lane = lax.broadcasted_iota(jnp.int32, (1, LANES), 1)
    diag = (lax.broadcasted_iota(jnp.int32, (DH_D, LANES), 1) & (DH_D - 1)) == lax.broadcasted_iota(
        jnp.int32, (DH_D, LANES), 0)

    def step_group(t8, carry):
        bases = (pl.multiple_of(t8 * SUBLANES, SUBLANES), pl.multiple_of(steps - (t8 + 1) * SUBLANES, SUBLANES))
        blk = [[[ref[s, pl.ds(bases[drc], SUBLANES), :] for ref in dirs[drc]] for s, drc in grp] for grp in groups]

        def issue(j, q):
            for c, m, s, drc, g in chains(q):
                loc = (SUBLANES - 1 - j) if drc else j
                sl = slice(g * LANES, (g + 1) * LANES)
                n_b, v_b = blk[q][m][2], blk[q][m][5]
                pabuf[q][rows(c), :] = (st[q][rows(c), :] * n_b[loc:loc + 1, sl]).astype(BF16)
                pabuf[q][rows(c, base=m_rows), :] = jnp.where(diag, v_b[loc:loc + 1, sl], 0.0).astype(BF16)
            return _dot(pabuf[q][...], q1_ref[...])

        def retire(j, q, sums):
            for c, m, s, drc, g in chains(q):
                loc = (SUBLANES - 1 - j) if drc else j
                sl = slice(g * LANES, (g + 1) * LANES)
                r_b, w_b, _, b_b, k_b, _ = blk[q][m]
                sv = (st[q][rows(c), :] * w_b[loc:loc + 1, sl] + sums[rows(c)] * b_b[loc:loc + 1, sl]
                      + sums[rows(c, base=m_rows)] * k_b[loc:loc + 1, sl])
                st[q][rows(c), :] = sv
                obuf[q][rows(c), loc * LANES:(loc + 1) * LANES] = (sv * r_b[loc:loc + 1, sl]).astype(BF16)

        order = [(j, q) for j in range(SUBLANES) for q in range(len(groups))]
        inflight = []
        for slot, (j, q) in enumerate(order):
            inflight.append((j, q, issue(j, q)))
            if len(inflight) > RWKV_LAG:
                retire(*inflight.pop(0))
        for item in inflight:
            retire(*item)
        for q in range(len(groups)):
            o_all = _dot(obuf[q][...], wsp_ref[...])
            for c, m, s, drc, g in chains(q):
                fresh = (lane >= bases[drc]) & (lane < bases[drc] + SUBLANES)
                top, bot = slice(c * LANES, c * LANES + DH_D), slice(c * LANES + DH_D, (c + 1) * LANES)
                ot[q][top, :] = jnp.where(fresh, o_all[rows(c), 0:LANES], ot[q][top, :])
                ot[q][bot, :] = jnp.where(fresh, o_all[rows(c), LANES:2 * LANES], ot[q][bot, :])
        return carry

    lax.fori_loop(0, steps // SUBLANES, step_group, 0)

    for q in range(len(groups)):
        for c, m, s, drc, g in chains(q):
            o_refs[drc][s, :, g * LANES:(g + 1) * LANES] = ot[q][rows(c, LANES), :].T

    @pl.when(tb == pl.num_programs(1) - 1)
    def _():
        for q in range(len(groups)):
            for c, m, s, drc, g in chains(q):
                sfin_ref[s, drc, g] = st[q][rows(c), :]


def _rwkv_scan_call(r_src, w0, w1, nkk, bb, kt, s0, q1, wsp, *, n_seq, seq, row0):
    ns = RWKV_NS
    ntb = seq // RWKV_TB
    ngrp = D_D // LANES
    grp0 = row0 // (seq * ns)

    def view(t):
        return t.reshape(N_TOK // (seq * ns), ns, ntb, RWKV_TB, t.shape[-1])

    def tok(rev, cb=0):
        if rev:
            return pl.BlockSpec((None, ns, None, RWKV_TB, D_D), lambda b, t: (grp0 + b, 0, ntb - 1 - t, 0, cb))
        return pl.BlockSpec((None, ns, None, RWKV_TB, D_D), lambda b, t: (grp0 + b, 0, t, 0, cb))

    def out(rev):
        if rev:
            return pl.BlockSpec((None, ns, None, RWKV_TB, D_D), lambda b, t: (b, 0, ntb - 1 - t, 0, 0))
        return pl.BlockSpec((None, ns, None, RWKV_TB, D_D), lambda b, t: (b, 0, t, 0, 0))

    st_spec = pl.BlockSpec((ns, 2, ngrp, DH_D, LANES), lambda b, t: (b, 0, 0, 0, 0))
    in_specs = []
    for rev in (False, True):
        in_specs += [tok(rev, 0), tok(rev), tok(rev), tok(rev), tok(rev), tok(rev, 2)]
    in_specs += [st_spec, pl.BlockSpec((LANES, LANES), lambda b, t: (0, 0)),
                 pl.BlockSpec((SUBLANES * LANES, 2 * LANES), lambda b, t: (0, 0))]
    rv, w0v, w1v, nv, bv, kv = [view(t) for t in (r_src, w0, w1, nkk, bb, kt)]
    o_shape = jax.ShapeDtypeStruct((n_seq // ns, ns, ntb, RWKV_TB, D_D), F32)
    m_rows = RWKV_GROUP * ngrp * DH_D
    n_groups = ns * 2 // RWKV_GROUP
    scratch = ([pltpu.VMEM((m_rows, LANES), F32)] * n_groups + [pltpu.VMEM((2 * m_rows, LANES), F32)] * n_groups
               + [pltpu.VMEM((2 * m_rows, LANES), BF16)] * n_groups
               + [pltpu.VMEM((m_rows, SUBLANES * LANES), BF16)] * n_groups)
    o_f, o_b, s_fin = pl.pallas_call(
        _rwkv_scan_kernel,
        grid=(n_seq // ns, ntb),
        in_specs=in_specs,
        out_specs=[out(False), out(True), st_spec],
        out_shape=[o_shape, o_shape, jax.ShapeDtypeStruct((n_seq, 2, ngrp, DH_D, LANES), F32)],
        scratch_shapes=scratch,
        compiler_params=_params(("arbitrary", "arbitrary")),
        name="rwkv_dec" if row0 else "rwkv_ctx",
    )(rv, w0v, nv, bv, kv, rv, rv, w1v, nv, bv, kv, rv, s0, q1, wsp)
    return o_f.reshape(n_seq * seq, D_D), o_b.reshape(n_seq * seq, D_D), s_fin


def _rwkv_post_kernel(ofl_ref, ofh_ref, obl_ref, obh_ref, rkv_ref, g_ref, lng_ref, lnb_ref, gm_ref, o_ref, *, tm):
    gm = gm_ref[...]
    o_all = _pick(tm, ofl_ref, ofh_ref) + _pick(tm, obl_ref, obh_ref)
    for j in range(D_D // LANES):
        sl = slice(j * LANES, (j + 1) * LANES)
        o = o_all[:, sl]
        dlt = o - _dot_b16(o, gm)
        var = _dot_b16(dlt * dlt, gm)
        y = dlt * lax.rsqrt(var + RWKV_GN_EPS) * lng_ref[:, sl] + lnb_ref[:, sl]
        o_ref[:, sl] = (y + rkv_ref[:, sl]) * g_ref[:, sl]


def _rwkv_post_call(o_f, o_b, rkv, g, ln_g, ln_b, gm):
    tm = 512
    tok = pl.BlockSpec((tm, D_D), lambda i: (i, 0))
    row = pl.BlockSpec((1, D_D), lambda i: (0, 0))
    return pl.pallas_call(
        functools.partial(_rwkv_post_kernel, tm=tm),
        grid=(N_TOK // tm,),
        in_specs=_pair_specs(tm, D_D, o_f) + _pair_specs(tm, D_D, o_b)
        + [tok, tok, row, row, pl.BlockSpec((LANES, LANES), lambda i: (0, 0))],
        out_specs=tok,
        out_shape=jax.ShapeDtypeStruct((N_TOK, D_D), F32),
        compiler_params=_params(("arbitrary",)),
        name="rwkv_post",
    )(o_f[0], o_f[1], o_b[0], o_b[1], rkv, g, ln_g, ln_b, gm)


def _rope_tables():
    pos = np.arange(DEC_SEQ)
    pr, pc = pos // GRID_W, pos % GRID_W
    lane = np.arange(LANES)
    dd = lane % DH_A
    use_col = (dd // 32) == 1
    j = dd % 16
    is_lo = (dd % 32) < 16
    freq = ROPE_THETA ** (-(j.astype(np.float64)) / 16.0)
    p = np.where(use_col[None, :], pc[:, None], pr[:, None]).astype(np.float64)
    ang = (p.astype(np.float32) * freq.astype(np.float32)[None, :]).astype(np.float32)
    cos = np.cos(ang).astype(np.float32)
    sin = np.sin(ang).astype(np.float32)
    s1 = np.where(is_lo[None, :], -sin, 0.0).astype(np.float32)
    s2 = np.where(is_lo[None, :], 0.0, sin).astype(np.float32)
    ident = 512
    cos = np.concatenate([cos, np.ones((ident, LANES), np.float32)], 0)
    s1 = np.concatenate([s1, np.zeros((ident, LANES), np.float32)], 0)
    s2 = np.concatenate([s2, np.zeros((ident, LANES), np.float32)], 0)
    return jnp.asarray(cos), jnp.asarray(s1), jnp.asarray(s2)


def _block_diag(value):
    m = np.zeros((LANES, LANES), np.float32)
    half = LANES // 2
    m[:half, :half] = value
    m[half:, half:] = value
    return jnp.asarray(m)


def _head_spread():
    step = np.arange(SUBLANES * LANES) // LANES
    head = (np.arange(SUBLANES * LANES) % LANES) // DH_D
    out_head = np.arange(2 * LANES) // LANES
    out_step = (np.arange(2 * LANES) % LANES) % SUBLANES
    m = (head[:, None] == out_head[None, :]) & (step[:, None] == out_step[None, :])
    return jnp.asarray(m.astype(np.float32))


def _qb_perm():
    idx = np.zeros(D_B, np.int32)
    for j in range(HB // 2):
        for hh in range(2):
            for dch in range(DH_B):
                idx[j * LANES + hh * DH_B + dch] = (hh * (HB // 2) + j) * DH_B + dch
    return idx


def kernel(x_prompt, x_sample, cache_a_k, cache_a_v, cache_b_k, cache_b_v, state_hgrn, state_rwkv, c, c_ctx, ada_w, ada_b, norm_mix_g, norm_ffn_g, final_norm_g, ev_w_in, ev_w_out, a_lambda, a_subln_g, b_q_norm_g, b_k_norm_g, od_w_in, od_w_out, hgrn_lb_logits, hgrn_norm_g, rwkv_mu, rwkv_w0, rwkv_w_up, rwkv_a0, rwkv_a_up, rwkv_g_up, rwkv_k_k, rwkv_k_a, rwkv_r_k, rwkv_ln_g, rwkv_ln_b, ffn_w_up, ffn_conv_w, ffn_conv_b, ffn_w_down):
    d = D_MODEL
    x = (x_prompt.reshape(N_CTX, d), x_sample.reshape(N_DEC, d), 0)
    cond =jnp.concatenate([c_ctx[None, :], c, jnp.zeros((COND_ROWS - 1 - DEC_BATCH, d), F32)], axis=0)
    mod = _mod_call(cond, ada_w, ada_b).reshape(DEPTH, COND_ROWS, 1, 6 * d)

    cos, s1, s2 = _rope_tables()
    g_mean = _block_diag(1.0 / DH_B)
    g_sum = _block_diag(1.0)
    perm = _qb_perm()
    row = lambda t: t.reshape(1, -1)

    new_ctx = None
    for l in range(DEPTH):
        i = l // 2
        mod_l = mod[l]
        if l % 2 == 0:
            w_in = ev_w_in[i]
            w_in = jnp.concatenate([w_in[:, :1536], w_in[:, 1536:2048][:, perm], w_in[:, 2048:]], axis=1).astype(BF16)
            w_out = ev_w_out[i]
            w_out = jnp.concatenate([w_out[:D_A], w_out[D_A:][perm]], axis=0).astype(BF16)
            qg = row(jnp.tile(b_q_norm_g[i], 2))
            kg = row(jnp.tile(b_k_norm_g[i], 2))
            qa, ka, va, qb, kb, vb = _even_proj_call(x, mod_l, row(norm_mix_g[l]), w_in, qg, kg, cos, s1, s2, g_mean)
            lam_init = 0.8 - 0.6 * math.exp(-0.3 * l)
            sub_g = row(a_subln_g[i])
            oa_c, ob_c = _attn_call(qa, qb, ka, va, kb, vb, None, a_lambda[i], sub_g, lam_init,
                                    n_seq=BATCH, seq=SEQ, row0=0)
            cbk = jnp.transpose(cache_b_k[:, i], (0, 2, 1, 3)).reshape(DEC_BATCH, PAST_LEN, HKV_B * DH_B)
            cbv = jnp.transpose(cache_b_v[:, i], (0, 2, 1, 3)).reshape(DEC_BATCH, PAST_LEN, HKV_B * DH_B)
            cache = (cache_a_k[:, i], cache_a_v[:, i], cbk, cbv)
            oa_s, ob_s = _attn_call(qa, qb, ka, va, kb, vb, cache, a_lambda[i], sub_g, lam_init,
                                    n_seq=DEC_BATCH, seq=DEC_SEQ, row0=N_CTX)
            mix_a = (oa_c, oa_s, 0)
            mix_b = (ob_c, ob_s, 0)

            def heads_first(t, nh):
                t = t[:N_CTX].reshape(BATCH, SEQ, nh, -1)
                return jnp.transpose(t, (0, 2, 1, 3))[:, None]

            even_ctx = (heads_first(ka, HA), heads_first(va, HA), heads_first(kb, HKV_B), heads_first(vb, HKV_B))
        else:
            w = od_w_in[i]
            w_h = w[:, :HGRN_PROJ].astype(BF16)
            wr = w[:, HGRN_PROJ:]
            mu = rwkv_mu[i]

            def rwkv_cols(t):
                z = jnp.zeros(t.shape[:-1] + (RWKV_PAD - RWKV_PROJ,), t.dtype)
                return jnp.concatenate([t[..., :1664], t[..., 1728:1856], t[..., 1664:1728], z], axis=-1)

            w_r = rwkv_cols(wr).astype(BF16)
            mu_r = row(rwkv_cols(mu))
            g_l = row(norm_mix_g[l])
            assert x[0] is x[1]
            proj_h = _odd_proj_call(x[0], mod_l, g_l, w_h, jnp.zeros((1, HGRN_PROJ), F32), shift=False)
            proj_r = _odd_proj_call(x[0], mod_l, g_l, w_r, mu_r, shift=True)

            ng = row(hgrn_norm_g[i])
            zero_h = jnp.zeros((BATCH, 2, HC, DK_C, DV_C), F32)
            ones = jnp.ones((LANES, LANES), BF16)
            oc_c, sh_c = _hgrn_call(proj_h, hgrn_lb_logits, ng, zero_h, ones, n_seq=BATCH, seq=SEQ, row0=0, lidx=l)
            oc_s, _ = _hgrn_call(proj_h, hgrn_lb_logits, ng, state_hgrn[:, i], ones, n_seq=DEC_BATCH, seq=DEC_SEQ,
                                 row0=N_CTX, lidx=l)

            pad_rows = lambda t: jnp.concatenate([t, jnp.zeros((LANES - t.shape[0], t.shape[1]), F32)], axis=0)
            wup0 = pad_rows(rwkv_w_up[i, 0])
            wup1 = jnp.concatenate([jnp.zeros((W_LORA, D_D), F32), rwkv_w_up[i, 1]], axis=0)
            w0o, w1o, nkk, bb, kt, gg, rkv = _rwkv_prep_call(
                proj_r, row(rwkv_a0[i]), pad_rows(rwkv_a_up[i]), rwkv_g_up[i], row(rwkv_k_k[i]), row(rwkv_k_a[i]),
                row(rwkv_r_k[i]), rwkv_w0[i], wup0, wup1, g_sum)

            def to_tiles(s):
                b = s.shape[0]
                s = s.reshape(b, 2, HD // 2, 2, DH_D, DH_D)
                return jnp.transpose(s, (0, 1, 2, 4, 3, 5)).reshape(b, 2, HD // 2, DH_D, LANES)

            def from_tiles(s):
                b = s.shape[0]
                s = s.reshape(b, 2, HD // 2, DH_D, 2, DH_D)
                return jnp.transpose(s, (0, 1, 2, 4, 3, 5)).reshape(b, 2, HD, DH_D, DH_D)

            zero_r = jnp.zeros((BATCH, 2, HD // 2, DH_D, LANES), F32)
            q1 = g_sum.astype(BF16)
            wsp = _head_spread().astype(BF16)
            of_c, ob_c, sr_c = _rwkv_scan_call(proj_r, w0o, w1o, nkk, bb, kt, zero_r, q1, wsp,
                                               n_seq=BATCH, seq=SEQ, row0=0)
            of_s, ob_s, _ = _rwkv_scan_call(proj_r, w0o, w1o, nkk, bb, kt, to_tiles(state_rwkv[:, i]), q1, wsp,
                                            n_seq=DEC_BATCH, seq=DEC_SEQ, row0=N_CTX)
            mix_b = _whole(_rwkv_post_call((of_c, of_s, 0), (ob_c, ob_s, 0), rkv, gg, row(rwkv_ln_g[i]),
                                           row(rwkv_ln_b[i]), g_mean))
            mix_a = (oc_c, oc_s, 0)
            w_out = od_w_out[i].astype(BF16)
            odd_ctx = (sh_c[:, None], from_tiles(sr_c)[:, None])

        x1 = _out_proj_call(x, mix_a, mix_b, mod_l, w_out)
        x2 = _ffn_call(x1, mod_l, row(norm_ffn_g[l]), ffn_w_up[l].astype(BF16), ffn_conv_w[l], row(ffn_conv_b[l]),
                       ffn_w_down[l].astype(BF16), row(final_norm_g), final=(l == DEPTH - 1))
        x = _whole(x2)

    y_prompt = x2[:N_CTX].reshape(BATCH, SEQ, d)
    y_sample = x2[N_CTX:].reshape(DEC_BATCH, DEC_SEQ, d)
    return (y_prompt, y_sample) + even_ctx + odd_ctx
```

```python
import functools
import math

import jax
import jax.numpy as jnp
import numpy as np
from jax import lax
from jax.experimental import pallas as pl
from jax.experimental.pallas import tpu as pltpu

D_MODEL = 1024
BATCH = 16
SEQ = 256
DEPTH = 2
DEC_BATCH = 4
DEC_SEQ = 1024
PAST_LEN = 512
GRID_W = 64
ROPE_THETA = 10000.0
EPS = 1e-6
RWKV_GN_EPS = 64e-5
HA = 4
DH_A = 64
DV_A = 2 * DH_A
HB = 8
HKV_B = 2
DH_B = 64
HC = 4
DK_C = 128
DV_C = 128
HD = 8
DH_D = 64
W_LORA = 64
A_LORA = 64
G_LORA = 128
D_FF = 2816

D_A = HA * DV_A
D_B = HB * DH_B
D_C = HC * DV_C
D_D = HD * DH_D
EVEN_PROJ = 2304
HGRN_PROJ = 2560
RWKV_PROJ = 1856
RWKV_PAD = 1920

N_CTX = BATCH * SEQ
N_DEC = DEC_BATCH * DEC_SEQ
N_TOK = N_CTX + N_DEC
COND_ROWS = 8

LANES = 128
SUBLANES = 8
VMEM_LIMIT = 56 * 1024 * 1024

F32 = jnp.float32
BF16 = jnp.bfloat16
HI = lax.Precision.HIGHEST

FF_TILE = 256
HGRN_L = 128
HGRN_HP = 2
RWKV_TB = 128
RWKV_NS = 4
RWKV_GROUP = 1
RWKV_LAG = 3
assert RWKV_LAG < 2 * RWKV_NS // RWKV_GROUP


def _dot(a, b):
    return jnp.dot(a, b, preferred_element_type=F32)


def _dot_hi(a, b):
    return jnp.dot(a, b, preferred_element_type=F32, precision=HI)


def _dot_b16(a, b):
    return _dot(a.astype(BF16), b.astype(BF16))


def _dot_nt(a, b, precision=None):
    return lax.dot_general(a, b, (((1,), (1,)), ((), ())), preferred_element_type=F32, precision=precision)


def _dot_tn(a, b, precision=None):
    return lax.dot_general(a, b, (((0,), (0,)), ((), ())), preferred_element_type=F32, precision=precision)


def _sigmoid(x):
    return 1.0 / (1.0 + jnp.exp(-x))


def _silu(x):
    return x * _sigmoid(x)


def _norm_mod(x, g, sc, sh):
    ms = jnp.mean(x * x, axis=-1, keepdims=True)
    return (x * lax.rsqrt(ms + EPS) * g) * (1.0 + sc) + sh


def _params(sem):
    return pltpu.CompilerParams(dimension_semantics=sem, vmem_limit_bytes=VMEM_LIMIT)


def _cond_row(i, tm):
    r0 = i * tm
    return jnp.where(r0 < N_CTX, 0, 1 + (r0 - N_CTX) // DEC_SEQ)


def _seq_len(i, tm):
    return jnp.where(i * tm < N_CTX, SEQ, DEC_SEQ)


ROW_PAD = SUBLANES


def _zero_row_pads(u_ref, tm):
    for k in range(u_ref.shape[0]):
        u_ref[k, 0:ROW_PAD, :] = jnp.zeros((ROW_PAD, u_ref.shape[2]), u_ref.dtype)
        u_ref[k, ROW_PAD + tm:2 * ROW_PAD + tm, :] = jnp.zeros((ROW_PAD, u_ref.shape[2]), u_ref.dtype)


def _neighbour_rows(u_ref, k, c, ch, seq_len):
    r0 = ROW_PAD + c * ch
    row8 = lax.broadcasted_iota(jnp.int32, (SUBLANES, 1), 0)
    at_start = (c * ch) % seq_len == 0
    at_end = ((c + 1) * ch) % seq_len == 0
    prev = u_ref[k, r0 - 1:r0 - 1 + ch, :]
    nxt = u_ref[k, r0 + 1:r0 + 1 + ch, :]
    prev = jnp.concatenate([jnp.where((row8 == 0) & at_start, 0.0, prev[0:SUBLANES]), prev[SUBLANES:]], axis=0)
    nxt = jnp.concatenate(
        [nxt[:ch - SUBLANES], jnp.where((row8 == SUBLANES - 1) & at_end, 0.0, nxt[ch - SUBLANES:])], axis=0)
    return prev, nxt


def _pick(tm, lo_ref, hi_ref, rows=slice(None)):
    return jnp.where(pl.program_id(0) * tm < N_CTX, lo_ref[rows, :], hi_ref[rows, :])


def _pair_specs(tm, width, pair):
    nct = N_CTX // tm
    hi0 = pair[2] // tm
    return [pl.BlockSpec((tm, width), lambda i: (jnp.minimum(i, nct - 1), 0)),
            pl.BlockSpec((tm, width), lambda i: (hi0 + jnp.maximum(i - nct, 0), 0))]


def _whole(t):
    return (t, t, N_CTX)


def _mod_kernel(c_ref, w_ref, b_ref, o_ref):
    s = _silu(c_ref[...]).astype(BF16)
    o_ref[...] = _dot(s, w_ref[...].astype(BF16)) + b_ref[...]


def _mod_call(cond, ada_w, ada_b):
    tn = 1536
    n = 6 * D_MODEL
    return pl.pallas_call(
        _mod_kernel,
        grid=(DEPTH, n // tn),
        in_specs=[
            pl.BlockSpec((COND_ROWS, D_MODEL), lambda l, j: (0, 0)),
            pl.BlockSpec((None, D_MODEL, tn), lambda l, j: (l, 0, j)),
            pl.BlockSpec((None, 1, tn), lambda l, j: (l, 0, j)),
        ],
        out_specs=pl.BlockSpec((None, COND_ROWS, tn), lambda l, j: (l, 0, j)),
        out_shape=jax.ShapeDtypeStruct((DEPTH, COND_ROWS, n), F32),
        compiler_params=_params(("arbitrary", "arbitrary")),
        name="ada_mod",
    )(cond, ada_w, ada_b.reshape(DEPTH, 1, n))


def _even_proj_kernel(xl_ref, xh_ref, mod_ref, g_ref, w_ref, qg_ref, kg_ref, cos_ref, s1_ref, s2_ref, gm_ref,
                      qa_ref, ka_ref, va_ref, qb_ref, kb_ref, vb_ref, *, tm):
    d = D_MODEL
    gm = gm_ref[...]
    scale = DH_A ** -0.5

    def head_norm(t, g):
        return t * lax.rsqrt(_dot_b16(t * t, gm) + EPS) * g

    for c in range(tm // SEQ):
        rows = slice(c * SEQ, (c + 1) * SEQ)
        h = _norm_mod(_pick(tm, xl_ref, xh_ref, rows), g_ref[...], mod_ref[:, d:2 * d], mod_ref[:, 0:d]).astype(BF16)
        proj = _dot(h, w_ref[...])
        cos, s1, s2 = cos_ref[rows, :], s1_ref[rows, :], s2_ref[rows, :]

        def rope(t):
            return t * cos + pltpu.roll(t, LANES - 16, 1) * s1 + pltpu.roll(t, 16, 1) * s2

        for j in range(4):
            sl = slice(j * LANES, (j + 1) * LANES)
            qa_ref[rows, sl] = rope(proj[:, j * LANES:(j + 1) * LANES]) * scale
            ka_ref[rows, sl] = rope(proj[:, 512 + j * LANES:512 + (j + 1) * LANES])
            va_ref[rows, sl] = proj[:, 1024 + j * LANES:1024 + (j + 1) * LANES]
            qb_ref[rows, sl] = rope(head_norm(proj[:, 1536 + j * LANES:1536 + (j + 1) * LANES], qg_ref[...])) * scale
        kb_ref[rows, :] = rope(head_norm(proj[:, 2048:2176], kg_ref[...]))
        vb_ref[rows, :] = proj[:, 2176:2304]


def _even_proj_call(x, mod_l, g, w, qg, kg, cos, s1, s2, gm):
    tm = 512
    nt = N_TOK // tm
    n_rope_blk = DEC_SEQ // tm

    def rope_idx(i):
        return (jnp.where(i * tm < N_CTX, n_rope_blk, (i - N_CTX // tm) % n_rope_blk), 0)

    full = lambda shape: pl.BlockSpec(shape, lambda i: (0,) * len(shape))
    out512 = pl.BlockSpec((tm, 512), lambda i: (i, 0))
    out128 = pl.BlockSpec((tm, LANES), lambda i: (i, 0))
    return pl.pallas_call(
        functools.partial(_even_proj_kernel, tm=tm),
        grid=(nt,),
        in_specs=_pair_specs(tm, D_MODEL, x) + [
            pl.BlockSpec((None, 1, 6 * D_MODEL), lambda i: (_cond_row(i, tm), 0, 0)),
            full((1, D_MODEL)),
            full((D_MODEL, EVEN_PROJ)),
            full((1, LANES)),
            full((1, LANES)),
            pl.BlockSpec((tm, LANES), rope_idx),
            pl.BlockSpec((tm, LANES), rope_idx),
            pl.BlockSpec((tm, LANES), rope_idx),
            full((LANES, LANES)),
        ],
        out_specs=[out512, out512, out512, out512, out128, out128],
        out_shape=[jax.ShapeDtypeStruct((N_TOK, 512), F32)] * 4 + [jax.ShapeDtypeStruct((N_TOK, LANES), F32)] * 2,
        compiler_params=_params(("arbitrary",)),
        name="even_proj",
    )(x[0], x[1], mod_l, g, w, qg, kg, cos, s1, s2, gm)


def _softmax_pv(q, ks, vs):
    ss = [_dot_nt(q, k) for k in ks]
    m = functools.reduce(jnp.maximum, [jnp.max(s, axis=-1, keepdims=True) for s in ss])
    ps = [jnp.exp(s - m) for s in ss]
    l = functools.reduce(jnp.add, [jnp.sum(p, axis=-1, keepdims=True) for p in ps])
    acc = functools.reduce(jnp.add, [_dot(p.astype(BF16), v) for p, v in zip(ps, vs)])
    return acc / l


def _attn_kernel(*refs, has_cache, lam_init):
    if has_cache:
        (qa_ref, qb_ref, ka_ref, va_ref, kb_ref, vb_ref, cak_ref, cav_ref, cbk_ref, cbv_ref,
         al_ref, sg_ref, oa_ref, ob_ref) = refs
    else:
        qa_ref, qb_ref, ka_ref, va_ref, kb_ref, vb_ref, al_ref, sg_ref, oa_ref, ob_ref = refs
    al = al_ref[...]
    lam = (jnp.exp(jnp.sum(al[0:1] * al[1:2], axis=-1, keepdims=True))
           - jnp.exp(jnp.sum(al[2:3] * al[3:4], axis=-1, keepdims=True)) + lam_init)
    lo = lax.broadcasted_iota(jnp.int32, (1, LANES), 1) < DH_A

    for h in range(HA):
        sl = slice(h * LANES, (h + 1) * LANES)
        q = qa_ref[:, sl]
        ks = [ka_ref[:, sl].astype(BF16)]
        vs = [va_ref[:, sl].astype(BF16)]
        if has_cache:
            ks.insert(0, cak_ref[h].astype(BF16))
            vs.insert(0, cav_ref[h].astype(BF16))
        a1 = _softmax_pv(jnp.where(lo, q, 0.0).astype(BF16), ks, vs)
        a2 = _softmax_pv(jnp.where(lo, 0.0, q).astype(BF16), ks, vs)
        dlt = a1 - lam * a2
        ms = jnp.mean(dlt * dlt, axis=-1, keepdims=True)
        oa_ref[:, sl] = dlt * lax.rsqrt(ms + EPS) * sg_ref[...] * (1.0 - lam_init)

    ks = [kb_ref[...].astype(BF16)]
    vs = [vb_ref[...].astype(BF16)]
    if has_cache:
        ks.insert(0, cbk_ref[...].astype(BF16))
        vs.insert(0, cbv_ref[...].astype(BF16))
    for j in range(HB // 2):
        sl = slice(j * LANES, (j + 1) * LANES)
        q = qb_ref[:, sl]
        o0 = _softmax_pv(jnp.where(lo, q, 0.0).astype(BF16), ks, vs)
        o1 = _softmax_pv(jnp.where(lo, 0.0, q).astype(BF16), ks, vs)
        ob_ref[:, sl] = jnp.where(lo, o0, o1)


def _attn_call(qa, qb, ka, va, kb, vb, cache, a_lambda, subln_g, lam_init, *, n_seq, seq, row0):
    tq = 256
    nq = seq // tq
    qblk0 = row0 // tq
    sblk0 = row0 // seq
    has_cache = cache is not None
    qspec = pl.BlockSpec((tq, 512), lambda b, i: (qblk0 + b * nq + i, 0))
    own512 = pl.BlockSpec((seq, 512), lambda b, i: (sblk0 + b, 0))
    own128 = pl.BlockSpec((seq, LANES), lambda b, i: (sblk0 + b, 0))
    in_specs = [qspec, qspec, own512, own512, own128, own128]
    args = [qa, qb, ka, va, kb, vb]
    if has_cache:
        in_specs += [
            pl.BlockSpec((None, HA, PAST_LEN, LANES), lambda b, i: (b, 0, 0, 0)),
            pl.BlockSpec((None, HA, PAST_LEN, LANES), lambda b, i: (b, 0, 0, 0)),
            pl.BlockSpec((None, PAST_LEN, LANES), lambda b, i: (b, 0, 0)),
            pl.BlockSpec((None, PAST_LEN, LANES), lambda b, i: (b, 0, 0)),
        ]
        args += list(cache)
    in_specs += [pl.BlockSpec((4, DH_A), lambda b, i: (0, 0)), pl.BlockSpec((1, LANES), lambda b, i: (0, 0))]
    args += [a_lambda, subln_g]
    ospec = pl.BlockSpec((tq, 512), lambda b, i: (b * nq + i, 0))
    return pl.pallas_call(
        functools.partial(_attn_kernel, has_cache=has_cache, lam_init=lam_init),
        grid=(n_seq, nq),
        in_specs=in_specs,
        out_specs=[ospec, ospec],
        out_shape=[jax.ShapeDtypeStruct((n_seq * seq, 512), F32)] * 2,
        compiler_params=_params(("arbitrary", "arbitrary")),
        name="attn_dec" if has_cache else "attn_ctx",
    )(*args)


def _out_proj_kernel(xl_ref, xh_ref, al_ref, ah_ref, bl_ref, bh_ref, mod_ref, w_ref, o_ref, *, tm):
    d = D_MODEL
    half = al_ref.shape[1]
    a = _pick(tm, al_ref, ah_ref).astype(BF16)
    b = _pick(tm, bl_ref, bh_ref).astype(BF16)
    mix = _dot(a, w_ref[0:half, :]) + _dot(b, w_ref[half:2 * half, :])
    o_ref[...] = _pick(tm, xl_ref, xh_ref) + mod_ref[:, 2 * d:3 * d] * mix


def _out_proj_call(x, a, b, mod_l, w):
    tm = 512
    return pl.pallas_call(
        functools.partial(_out_proj_kernel, tm=tm),
        grid=(N_TOK // tm,),
        in_specs=_pair_specs(tm, D_MODEL, x) + _pair_specs(tm, 512, a) + _pair_specs(tm, 512, b) + [
            pl.BlockSpec((None, 1, 6 * D_MODEL), lambda i: (_cond_row(i, tm), 0, 0)),
            pl.BlockSpec((D_MODEL, D_MODEL), lambda i: (0, 0)),
        ],
        out_specs=pl.BlockSpec((tm, D_MODEL), lambda i: (i, 0)),
        out_shape=jax.ShapeDtypeStruct((N_TOK, D_MODEL), F32),
        compiler_params=_params(("arbitrary",)),
        name="out_proj",
    )(x[0], x[1], a[0], a[1], b[0], b[1], mod_l, w)


def _ffn_kernel(x_ref, mod_ref, g_ref, wv_ref, wg_ref, cwv_ref, cwg_ref, cbv_ref, cbg_ref, wd_ref, fg_ref,
                o_ref, h_scr, acc_scr, u_scr, *, tm, final):
    d = D_MODEL
    i, j = pl.program_id(0), pl.program_id(1)
    ch = SEQ

    @pl.when(j == 0)
    def _():
        h_scr[...] = _norm_mod(x_ref[...], g_ref[...], mod_ref[:, 4 * d:5 * d], mod_ref[:, 3 * d:4 * d]).astype(BF16)
        acc_scr[...] = jnp.zeros_like(acc_scr)
        _zero_row_pads(u_scr, tm)

    seq_len = _seq_len(i, tm)

    def project(c):
        rows = slice(c * ch, (c + 1) * ch)
        u_scr[0, ROW_PAD + c * ch:ROW_PAD + (c + 1) * ch, :] = _dot(h_scr[rows, :], wv_ref[...])
        u_scr[1, ROW_PAD + c * ch:ROW_PAD + (c + 1) * ch, :] = _dot(h_scr[rows, :], wg_ref[...])

    def conv(k, c, cw_ref, cb_ref):
        prev, nxt = _neighbour_rows(u_scr, k, c, ch, seq_len)
        cur = u_scr[k, ROW_PAD + c * ch:ROW_PAD + (c + 1) * ch, :]
        return cw_ref[0:1, :] * prev + cw_ref[1:2, :] * cur + cw_ref[2:3, :] * nxt + cb_ref[...]

    def activate(c):
        val = conv(0, c, cwv_ref, cbv_ref)
        gate = conv(1, c, cwg_ref, cbg_ref)
        act = (_silu(gate) * val).astype(BF16)
        acc_scr[c * ch:(c + 1) * ch, :] += _dot(act, wd_ref[...])

    n_chunks = tm // ch
    project(0)
    for c in range(n_chunks):
        if c + 1 < n_chunks:
            project(c + 1)
        activate(c)

    @pl.when(j == pl.num_programs(1) - 1)
    def _():
        y = x_ref[...] + mod_ref[:, 5 * d:6 * d] * acc_scr[...]
        if final:
            ms = jnp.mean(y * y, axis=-1, keepdims=True)
            y = y * lax.rsqrt(ms + EPS) * fg_ref[...]
        o_ref[...] = y


def _ffn_call(x, mod_l, g, w_up, conv_w, conv_b, w_down, final_g, *, final):
    tm = DEC_SEQ
    nf = D_FF // FF_TILE
    up = lambda j: j
    down = lambda j: j
    return pl.pallas_call(
        functools.partial(_ffn_kernel, tm=tm, final=final),
        grid=(N_TOK // tm, nf),
        in_specs=[
            pl.BlockSpec((tm, D_MODEL), lambda i, j: (i, 0)),
            pl.BlockSpec((None, 1, 6 * D_MODEL), lambda i, j: (_cond_row(i, tm), 0, 0)),
            pl.BlockSpec((1, D_MODEL), lambda i, j: (0, 0)),
            pl.BlockSpec((D_MODEL, FF_TILE), lambda i, j: (0, up(j))),
            pl.BlockSpec((D_MODEL, FF_TILE), lambda i, j: (0, nf + up(j))),
            pl.BlockSpec((3, FF_TILE), lambda i, j: (0, up(j))),
            pl.BlockSpec((3, FF_TILE), lambda i, j: (0, nf + up(j))),
            pl.BlockSpec((1, FF_TILE), lambda i, j: (0, up(j))),
            pl.BlockSpec((1, FF_TILE), lambda i, j: (0, nf + up(j))),
            pl.BlockSpec((FF_TILE, D_MODEL), lambda i, j: (down(j), 0)),
            pl.BlockSpec((1, D_MODEL), lambda i, j: (0, 0)),
        ],
        out_specs=pl.BlockSpec((tm, D_MODEL), lambda i, j: (i, 0)),
        out_shape=jax.ShapeDtypeStruct((N_TOK, D_MODEL), F32),
        scratch_shapes=[pltpu.VMEM((tm, D_MODEL), BF16), pltpu.VMEM((tm, D_MODEL), F32),
                        pltpu.VMEM((2, tm + 2 * SUBLANES, FF_TILE), F32)],
        compiler_params=_params(("arbitrary", "arbitrary")),
        name="conv_ffn",
    )(x, mod_l, g, w_up, w_up, conv_w, conv_w, conv_b, conv_b, w_down, final_g)


def _odd_proj_kernel(x_ref, mod_ref, g_ref, w_ref, mu_ref, o_ref, h_scr, *u_scr, tm, shift):
    d = D_MODEL
    i, j = pl.program_id(0), pl.program_id(1)
    ch = SEQ
    n_chunks = tm // ch

    @pl.when(j == 0)
    def _():
        h_scr[...] = _norm_mod(x_ref[...], g_ref[...], mod_ref[:, d:2 * d], mod_ref[:, 0:d]).astype(BF16)
        if shift:
            _zero_row_pads(u_scr[0], tm)

    if not shift:
        for c in range(n_chunks):
            o_ref[c * ch:(c + 1) * ch, :] = _dot(h_scr[c * ch:(c + 1) * ch, :], w_ref[...])
        return

    u = u_scr[0]
    seq_len = _seq_len(i, tm)

    def project(c):
        u[0, ROW_PAD + c * ch:ROW_PAD + (c + 1) * ch, :] = _dot(h_scr[c * ch:(c + 1) * ch, :], w_ref[...])

    project(0)
    for c in range(n_chunks):
        if c + 1 < n_chunks:
            project(c + 1)
        prev, nxt = _neighbour_rows(u, 0, c, ch, seq_len)
        p = u[0, ROW_PAD + c * ch:ROW_PAD + (c + 1) * ch, :]
        o_ref[c * ch:(c + 1) * ch, :] = p + mu_ref[...] * (0.5 * (prev + nxt) - p)


def _odd_proj_call(x, mod_l, g, w, mu, *, shift):
    tm = DEC_SEQ
    tn = 640
    n = w.shape[1]
    return pl.pallas_call(
        functools.partial(_odd_proj_kernel, tm=tm, shift=shift),
        grid=(N_TOK // tm, n // tn),
        in_specs=[
            pl.BlockSpec((tm, D_MODEL), lambda i, j: (i, 0)),
            pl.BlockSpec((None, 1, 6 * D_MODEL), lambda i, j: (_cond_row(i, tm), 0, 0)),
            pl.BlockSpec((1, D_MODEL), lambda i, j: (0, 0)),
            pl.BlockSpec((D_MODEL, tn), lambda i, j: (0, j)),
            pl.BlockSpec((1, tn), lambda i, j: (0, j)),
        ],
        out_specs=pl.BlockSpec((tm, tn), lambda i, j: (i, j)),
        out_shape=jax.ShapeDtypeStruct((N_TOK, n), F32),
        scratch_shapes=[pltpu.VMEM((tm, D_MODEL), BF16)]
        + ([pltpu.VMEM((1, tm + 2 * ROW_PAD, tn), F32)] if shift else []),
        compiler_params=_params(("arbitrary", "arbitrary")),
        name="odd_proj_rwkv" if shift else "odd_proj_hgrn",
    )(x, mod_l, g, w, mu)


def _hgrn_kernel(q_ref, ff_ref, fb_ref, v_ref, gc_ref, lbl_ref, ng_ref, s0_ref, ones_ref, o_ref, sfin_ref,
                 oacc_f, oacc_b, st_scr, qd_scr, add_scr, dec_scr, start_scr, *, seq, lidx):
    L = HGRN_L
    nc = seq // L
    nb = L // SUBLANES
    ones = ones_ref[...]
    rowi = lax.broadcasted_iota(jnp.int32, (L, L), 0)
    coli = lax.broadcasted_iota(jnp.int32, (L, L), 1)
    rowv = lax.broadcasted_iota(jnp.int32, (L, 1), 0)
    same_block = (rowi // SUBLANES) == (coli // SUBLANES)

    def lower_bound(drc):
        lg = lbl_ref[drc]
        e = jnp.exp(lg - jnp.max(lg, axis=0, keepdims=True))
        sm = e / jnp.sum(e, axis=0, keepdims=True)
        return functools.reduce(jnp.add, [sm[i:i + 1, :] for i in range(1, lidx + 1)])

    def spread(t, s_l):
        return jnp.concatenate(
            [jnp.broadcast_to(t[b * SUBLANES + s_l:b * SUBLANES + s_l + 1, :], (SUBLANES, LANES)) for b in range(nb)],
            axis=0)

    def chunk(rev, r0, f_ref, lb, tri, hs):
        q = _silu(q_ref[pl.ds(r0, L), hs])
        f = lb + (1.0 - lb) * _sigmoid(f_ref[pl.ds(r0, L), hs])
        k = 1.0 - f
        lf = jnp.log(f)
        v = v_ref[pl.ds(r0, L), hs].astype(BF16)
        cum = _dot_hi(tri, lf)
        cum_ex = cum - lf
        q_dec = (q * jnp.exp(cum)).astype(BF16)

        scores = jnp.zeros((L, L), F32)
        h = L // 2
        while h >= SUBLANES:
            pieces = []
            for a in range(0, L, 2 * h):
                edge = a + h - 1 if rev else a + h
                pieces.append(jnp.broadcast_to(cum_ex[edge:edge + 1, :], (2 * h, LANES)))
            anchor = jnp.concatenate(pieces, axis=0) if len(pieces) > 1 else pieces[0]
            is_q = ((rowv // h) % 2) == (0 if rev else 1)
            d = cum - anchor
            e = jnp.exp(jnp.where(is_q, d, -d))
            qe = jnp.where(is_q, q * e, 0.0).astype(BF16)
            ke = jnp.where(is_q, 0.0, k * e).astype(BF16)
            same_pair = (rowi // (2 * h)) == (coli // (2 * h))
            scores = scores + jnp.where(same_pair, _dot_nt(qe, ke), 0.0)
            h //= 2

        prods = []
        for s_l in range(SUBLANES):
            e = jnp.exp(jnp.minimum(cum - spread(cum, s_l), 0.0))
            prods.append(q * spread(k, s_l) * e)
        diag = _dot(jnp.concatenate(prods, axis=0).astype(BF16), ones)
        for s_l in range(SUBLANES):
            ok = ((rowi % SUBLANES) <= s_l) if rev else ((rowi % SUBLANES) >= s_l)
            take = same_block & ((coli % SUBLANES) == s_l) & ok
            sums = diag[s_l * L:(s_l + 1) * L, :]
            sums = sums[:, 0:L] if L <= LANES else jnp.concatenate([sums] * (L // LANES), axis=1)
            scores = jnp.where(take, sums, scores)
        o = _dot(scores.astype(BF16), v)

        end = 0 if rev else L - 1
        cend = cum[end:end + 1, :]
        kd = (k * jnp.exp(cend - cum)).astype(BF16)
        return q_dec, o, _dot_tn(v, kd), jnp.exp(cend)

    lb_f, lb_b = lower_bound(0), lower_bound(1)
    tri_f = jnp.where(coli <= rowi, 1.0, 0.0).astype(F32)
    tri_b = jnp.where(coli >= rowi, 1.0, 0.0).astype(F32)
    heads = [slice(hh * LANES, (hh + 1) * LANES) for hh in range(HGRN_HP)]
    chains = [(hh, hs, drc) for hh, hs in enumerate(heads) for drc in range(2)]
    oaccs = (oacc_f, oacc_b)

    def local(ci, carry):
        r0 = pl.multiple_of(ci * L, L)
        for c, (hh, hs, drc) in enumerate(chains):
            q_dec, o, add, dec = chunk(drc == 1, r0, fb_ref if drc else ff_ref, (lb_b if drc else lb_f)[:, hs],
                                       tri_b if drc else tri_f, hs)
            qd_scr[c, pl.ds(r0, L), :] = q_dec
            oaccs[drc][pl.ds(r0, L), hs] = o
            add_scr[c * nc + ci] = add
            dec_scr[c * nc + ci] = jnp.broadcast_to(dec, (SUBLANES, LANES))
        return carry

    lax.fori_loop(0, nc, local, 0, unroll=min(2, nc))

    for c, (hh, hs, drc) in enumerate(chains):
        st_scr[c] = s0_ref[drc, hh].T

    def scan(n, carry):
        for c, (hh, hs, drc) in enumerate(chains):
            slot = c * nc + ((nc - 1 - n) if drc else n)
            st = st_scr[c]
            start_scr[slot] = st.astype(BF16)
            st_scr[c] = st * dec_scr[slot][0:1, :] + add_scr[slot]
        return carry

    lax.fori_loop(0, nc, scan, 0)

    def carried(ci, carry):
        r0 = pl.multiple_of(ci * L, L)
        for c, (hh, hs, drc) in enumerate(chains):
            oaccs[drc][pl.ds(r0, L), hs] += _dot_nt(qd_scr[c, pl.ds(r0, L), :], start_scr[c * nc + ci])
        return carry

    lax.fori_loop(0, nc, carried, 0)

    for hh, hs in enumerate(heads):
        sfin_ref[0, hh] = st_scr[2 * hh].T
        sfin_ref[1, hh] = st_scr[2 * hh + 1].T
        o = oacc_f[:, hs] + oacc_b[:, hs]
        ms = jnp.mean(o * o, axis=-1, keepdims=True)
        o_ref[:, hs] = o * lax.rsqrt(ms + EPS) * ng_ref[...] * _silu(gc_ref[:, hs])


def _hgrn_call(proj_h, lb_logits, norm_g, s0, ones, *, n_seq, seq, row0, lidx):
    sblk0 = row0 // seq

    hp = HGRN_HP
    width = hp * LANES

    def col(section):
        return pl.BlockSpec((seq, width), lambda b, h: (sblk0 + b, section * (HC // hp) + h))

    st_spec = pl.BlockSpec((None, 2, hp, DK_C, DV_C), lambda b, h: (b, 0, h, 0, 0))
    return pl.pallas_call(
        functools.partial(_hgrn_kernel, seq=seq, lidx=lidx),
        grid=(n_seq, HC // hp),
        in_specs=[
            col(0), col(1), col(2), col(3), col(4),
            pl.BlockSpec((2, DEPTH, width), lambda b, h: (0, 0, h)),
            pl.BlockSpec((1, LANES), lambda b, h: (0, 0)),
            st_spec,
            pl.BlockSpec((LANES, LANES), lambda b, h: (0, 0)),
        ],
        out_specs=[pl.BlockSpec((seq, width), lambda b, h: (b, h)), st_spec],
        out_shape=[jax.ShapeDtypeStruct((n_seq * seq, D_C), F32),
                   jax.ShapeDtypeStruct((n_seq, 2, HC, DK_C, DV_C), F32)],
        scratch_shapes=[pltpu.VMEM((seq, width), F32), pltpu.VMEM((seq, width), F32),
                        pltpu.VMEM((2 * hp, DV_C, DK_C), F32),
                        pltpu.VMEM((2 * hp, seq, DK_C), BF16),
                        pltpu.VMEM((2 * hp * (seq // HGRN_L), DV_C, DK_C), F32),
                        pltpu.VMEM((2 * hp * (seq // HGRN_L), SUBLANES, DK_C), F32),
                        pltpu.VMEM((2 * hp * (seq // HGRN_L), DV_C, DK_C), BF16)],
        compiler_params=_params(("arbitrary", "arbitrary")),
        name="hgrn_dec" if row0 else "hgrn_ctx",
    )(proj_h, proj_h, proj_h, proj_h, proj_h, lb_logits, norm_g, s0, ones)


def _rwkv_prep_kernel(p_ref, a0_ref, aup_ref, gup_ref, kk_ref, ka_ref, rk_ref, w0_ref, wup0_ref, wup1_ref,
                      gs_ref, w0o_ref, w1o_ref, nkk_ref, bb_ref, kt_ref, g_ref, rkv_ref):
    r = p_ref[:, 0:512]
    k = p_ref[:, 512:1024]
    v = p_ref[:, 1024:1536]
    wd = p_ref[:, 1536:1664]
    gd = p_ref[:, 1664:1792]
    ad = p_ref[:, 1792:1920]
    gs = gs_ref[...]

    def group_sum(t):
        return jnp.concatenate([_dot_b16(t[:, j * LANES:(j + 1) * LANES], gs) for j in range(4)], axis=1)

    a = _sigmoid(a0_ref[...] + _dot_b16(ad, aup_ref[...]))
    g_ref[...] = _dot_b16(_sigmoid(gd), gup_ref[...])
    kkr = k * kk_ref[...]
    kk = kkr / jnp.maximum(jnp.sqrt(group_sum(kkr * kkr)), 1e-12)
    kt = k * (1.0 + (a - 1.0) * ka_ref[...])
    th = jnp.tanh(wd)
    decay = math.exp(-0.5)
    w0o_ref[...] = jnp.exp(-decay * _sigmoid(w0_ref[0:1, :] + _dot_b16(th, wup0_ref[...])))
    w1o_ref[...] = jnp.exp(-decay * _sigmoid(w0_ref[1:2, :] + _dot_b16(th, wup1_ref[...])))
    nkk_ref[...] = -kk
    bb_ref[...] = kk * a
    kt_ref[...] = kt
    rkv_ref[...] = group_sum(r * kt * rk_ref[...]) * v


def _rwkv_prep_call(proj_r, a0, aup, gup, kk_k, k_a, r_k, w0, wup0, wup1, gs):
    tm = 512
    full = lambda shape: pl.BlockSpec(shape, lambda i: (0,) * len(shape))
    ospec = pl.BlockSpec((tm, D_D), lambda i: (i, 0))
    return pl.pallas_call(
        _rwkv_prep_kernel,
        grid=(N_TOK // tm,),
        in_specs=[
            pl.BlockSpec((tm, RWKV_PAD), lambda i: (i, 0)),
            full((1, D_D)), full((LANES, D_D)), full((LANES, D_D)), full((1, D_D)), full((1, D_D)), full((1, D_D)),
            full((2, D_D)), full((LANES, D_D)), full((LANES, D_D)), full((LANES, LANES)),
        ],
        out_specs=[ospec] * 7,
        out_shape=[jax.ShapeDtypeStruct((N_TOK, D_D), F32)] * 7,
        compiler_params=_params(("arbitrary",)),
        name="rwkv_prep",
    )(proj_r, a0, aup, gup, kk_k, k_a, r_k, w0, wup0, wup1, gs)


def _rwkv_scan_kernel(rf_ref, wf_ref, nf_ref, bf_ref, kf_ref, vf_ref, rb_ref, wb_ref, nb_ref, bb_ref, kb_ref,
                      vb_ref, s0_ref, q1_ref, wsp_ref, of_ref, ob_ref, sfin_ref, *scratch):
    tb = pl.program_id(1)
    ns = RWKV_NS
    ngrp = D_D // LANES
    steps = RWKV_TB
    members = [(s, drc) for s in range(ns) for drc in range(2)]
    groups = [members[i:i + RWKV_GROUP] for i in range(0, len(members), RWKV_GROUP)]
    st, ot, pabuf, obuf = (scratch[i * len(groups):(i + 1) * len(groups)] for i in range(4))
    m_rows = RWKV_GROUP * ngrp * DH_D
    dirs = ((rf_ref, wf_ref, nf_ref, bf_ref, kf_ref, vf_ref), (rb_ref, wb_ref, nb_ref, bb_ref, kb_ref, vb_ref))
    o_refs = (of_ref, ob_ref)

    def rows(c, n=DH_D, base=0):
        return slice(base + c * n, base + (c + 1) * n)

    def chains(q):
        return [(m * ngrp + g, m, s, drc, g) for m, (s, drc) in enumerate(groups[q]) for g in range(ngrp)]

    @pl.when(tb == 0)
    def _():
        for q in range(len(groups)):
            for c, _, s, drc, g in chains(q):
                st[q][rows(c), :] = s0_ref[s, drc, g]

    for q in range(len(groups)):
        ot[q][...] = jnp.zeros_like(ot[q])
    lane = lax.broadcasted_iota(jnp.int32, (1, LANES), 1)
    diag = (lax.broadcasted_iota(jnp.int32, (DH_D, LANES), 1) & (DH_D - 1)) == lax.broadcasted_iota(
        jnp.int32, (DH_D, LANES), 0)

    def step_group(t8, carry):
        bases = (pl.multiple_of(t8 * SUBLANES, SUBLANES), pl.multiple_of(steps - (t8 + 1) * SUBLANES, SUBLANES))
        blk = [[[ref[s, pl.ds(bases[drc], SUBLANES), :] for ref in dirs[drc]] for s, drc in grp] for grp in groups]

        def issue(j, q):
            for c, m, s, drc, g in chains(q):
                loc = (SUBLANES - 1 - j) if drc else j
                sl = slice(g * LANES, (g + 1) * LANES)
                n_b, v_b = blk[q][m][2], blk[q][m][5]
                pabuf[q][rows(c), :] = (st[q][rows(c), :] * n_b[loc:loc + 1, sl]).astype(BF16)
                pabuf[q][rows(c, base=m_rows), :] = jnp.where(diag, v_b[loc:loc + 1, sl], 0.0).astype(BF16)
            return _dot(pabuf[q][...], q1_ref[...])

        def retire(j, q, sums):
            for c, m, s, drc, g in chains(q):
                loc = (SUBLANES - 1 - j) if drc else j
                sl = slice(g * LANES, (g + 1) * LANES)
                r_b, w_b, _, b_b, k_b, _ = blk[q][m]
                sv = (st[q][rows(c), :] * w_b[loc:loc + 1, sl] + sums[rows(c)] * b_b[loc:loc + 1, sl]
                      + sums[rows(c, base=m_rows)] * k_b[loc:loc + 1, sl])
                st[q][rows(c), :] = sv
                obuf[q][rows(c), loc * LANES:(loc + 1) * LANES] = (sv * r_b[loc:loc + 1, sl]).astype(BF16)

        order = [(j, q) for j in range(SUBLANES) for q in range(len(groups))]
        inflight = []
        for slot, (j, q) in enumerate(order):
            inflight.append((j, q, issue(j, q)))
            if len(inflight) > RWKV_LAG:
                retire(*inflight.pop(0))
        for item in inflight:
            retire(*item)
        for q in range(len(groups)):
            o_all = _dot(obuf[q][...], wsp_ref[...])
            for c, m, s, drc, g in chains(q):
                fresh = (lane >= bases[drc]) & (lane < bases[drc] + SUBLANES)
                top, bot = slice(c * LANES, c * LANES + DH_D), slice(c * LANES + DH_D, (c + 1) * LANES)
                ot[q][top, :] = jnp.where(fresh, o_all[rows(c), 0:LANES], ot[q][top, :])
                ot[q][bot, :] = jnp.where(fresh, o_all[rows(c), LANES:2 * LANES], ot[q][bot, :])
        return carry

    lax.fori_loop(0, steps // SUBLANES, step_group, 0)

    for q in range(len(groups)):
        for c, m, s, drc, g in chains(q):
            o_refs[drc][s, :, g * LANES:(g + 1) * LANES] = ot[q][rows(c, LANES), :].T

    @pl.when(tb == pl.num_programs(1) - 1)
    def _():
        for q in range(len(groups)):
            for c, m, s, drc, g in chains(q):
                sfin_ref[s, drc, g] = st[q][rows(c), :]


def _rwkv_scan_call(r_src, w0, w1, nkk, bb, kt, s0, q1, wsp, *, n_seq, seq, row0):
    ns = RWKV_NS
    ntb = seq // RWKV_TB
    ngrp = D_D // LANES
    grp0 = row0 // (seq * ns)

    def view(t):
        return t.reshape(N_TOK // (seq * ns), ns, ntb, RWKV_TB, t.shape[-1])

    def tok(rev, cb=0):
        if rev:
            return pl.BlockSpec((None, ns, None, RWKV_TB, D_D), lambda b, t: (grp0 + b, 0, ntb - 1 - t, 0, cb))
        return pl.BlockSpec((None, ns, None, RWKV_TB, D_D), lambda b, t: (grp0 + b, 0, t, 0, cb))

    def out(rev):
        if rev:
            return pl.BlockSpec((None, ns, None, RWKV_TB, D_D), lambda b, t: (b, 0, ntb - 1 - t, 0, 0))
        return pl.BlockSpec((None, ns, None, RWKV_TB, D_D), lambda b, t: (b, 0, t, 0, 0))

    st_spec = pl.BlockSpec((ns, 2, ngrp, DH_D, LANES), lambda b, t: (b, 0, 0, 0, 0))
    in_specs = []
    for rev in (False, True):
        in_specs += [tok(rev, 0), tok(rev), tok(rev), tok(rev), tok(rev), tok(rev, 2)]
    in_specs += [st_spec, pl.BlockSpec((LANES, LANES), lambda b, t: (0, 0)),
                 pl.BlockSpec((SUBLANES * LANES, 2 * LANES), lambda b, t: (0, 0))]
    rv, w0v, w1v, nv, bv, kv = [view(t) for t in (r_src, w0, w1, nkk, bb, kt)]
    o_shape = jax.ShapeDtypeStruct((n_seq // ns, ns, ntb, RWKV_TB, D_D), F32)
    m_rows = RWKV_GROUP * ngrp * DH_D
    n_groups = ns * 2 // RWKV_GROUP
    scratch = ([pltpu.VMEM((m_rows, LANES), F32)] * n_groups + [pltpu.VMEM((2 * m_rows, LANES), F32)] * n_groups
               + [pltpu.VMEM((2 * m_rows, LANES), BF16)] * n_groups
               + [pltpu.VMEM((m_rows, SUBLANES * LANES), BF16)] * n_groups)
    o_f, o_b, s_fin = pl.pallas_call(
        _rwkv_scan_kernel,
        grid=(n_seq // ns, ntb),
        in_specs=in_specs,
        out_specs=[out(False), out(True), st_spec],
        out_shape=[o_shape, o_shape, jax.ShapeDtypeStruct((n_seq, 2, ngrp, DH_D, LANES), F32)],
        scratch_shapes=scratch,
        compiler_params=_params(("arbitrary", "arbitrary")),
        name="rwkv_dec" if row0 else "rwkv_ctx",
    )(rv, w0v, nv, bv, kv, rv, rv, w1v, nv, bv, kv, rv, s0, q1, wsp)
    return o_f.reshape(n_seq * seq, D_D), o_b.reshape(n_seq * seq, D_D), s_fin


def _rwkv_post_kernel(ofl_ref, ofh_ref, obl_ref, obh_ref, rkv_ref, g_ref, lng_ref, lnb_ref, gm_ref, o_ref, *, tm):
    gm = gm_ref[...]
    o_all = _pick(tm, ofl_ref, ofh_ref) + _pick(tm, obl_ref, obh_ref)
    for j in range(D_D // LANES):
        sl = slice(j * LANES, (j + 1) * LANES)
        o = o_all[:, sl]
        dlt = o - _dot_b16(o, gm)
        var = _dot_b16(dlt * dlt, gm)
        y = dlt * lax.rsqrt(var + RWKV_GN_EPS) * lng_ref[:, sl] + lnb_ref[:, sl]
        o_ref[:, sl] = (y + rkv_ref[:, sl]) * g_ref[:, sl]


def _rwkv_post_call(o_f, o_b, rkv, g, ln_g, ln_b, gm):
    tm = 512
    tok = pl.BlockSpec((tm, D_D), lambda i: (i, 0))
    row = pl.BlockSpec((1, D_D), lambda i: (0, 0))
    return pl.pallas_call(
        functools.partial(_rwkv_post_kernel, tm=tm),
        grid=(N_TOK // tm,),
        in_specs=_pair_specs(tm, D_D, o_f) + _pair_specs(tm, D_D, o_b)
        + [tok, tok, row, row, pl.BlockSpec((LANES, LANES), lambda i: (0, 0))],
        out_specs=tok,
        out_shape=jax.ShapeDtypeStruct((N_TOK, D_D), F32),
        compiler_params=_params(("arbitrary",)),
        name="rwkv_post",
    )(o_f[0], o_f[1], o_b[0], o_b[1], rkv, g, ln_g, ln_b, gm)


def _rope_tables():
    pos = np.arange(DEC_SEQ)
    pr, pc = pos // GRID_W, pos % GRID_W
    lane = np.arange(LANES)
    dd = lane % DH_A
    use_col = (dd // 32) == 1
    j = dd % 16
    is_lo = (dd % 32) < 16
    freq = ROPE_THETA ** (-(j.astype(np.float64)) / 16.0)
    p = np.where(use_col[None, :], pc[:, None], pr[:, None]).astype(np.float64)
    ang = (p.astype(np.float32) * freq.astype(np.float32)[None, :]).astype(np.float32)
    cos = np.cos(ang).astype(np.float32)
    sin = np.sin(ang).astype(np.float32)
    s1 = np.where(is_lo[None, :], -sin, 0.0).astype(np.float32)
    s2 = np.where(is_lo[None, :], 0.0, sin).astype(np.float32)
    ident = 512
    cos = np.concatenate([cos, np.ones((ident, LANES), np.float32)], 0)
    s1 = np.concatenate([s1, np.zeros((ident, LANES), np.float32)], 0)
    s2 = np.concatenate([s2, np.zeros((ident, LANES), np.float32)], 0)
    return jnp.asarray(cos), jnp.asarray(s1), jnp.asarray(s2)


def _block_diag(value):
    m = np.zeros((LANES, LANES), np.float32)
    half = LANES // 2
    m[:half, :half] = value
    m[half:, half:] = value
    return jnp.asarray(m)


def _head_spread():
    step = np.arange(SUBLANES * LANES) // LANES
    head = (np.arange(SUBLANES * LANES) % LANES) // DH_D
    out_head = np.arange(2 * LANES) // LANES
    out_step = (np.arange(2 * LANES) % LANES) % SUBLANES
    m = (head[:, None] == out_head[None, :]) & (step[:, None] == out_step[None, :])
    return jnp.asarray(m.astype(np.float32))


def _qb_perm():
    idx = np.zeros(D_B, np.int32)
    for j in range(HB // 2):
        for hh in range(2):
            for dch in range(DH_B):
                idx[j * LANES + hh * DH_B + dch] = (hh * (HB // 2) + j) * DH_B + dch
    return idx


def kernel(x_prompt, x_sample, cache_a_k, cache_a_v, cache_b_k, cache_b_v, state_hgrn, state_rwkv, c, c_ctx, ada_w, ada_b, norm_mix_g, norm_ffn_g, final_norm_g, ev_w_in, ev_w_out, a_lambda, a_subln_g, b_q_norm_g, b_k_norm_g, od_w_in, od_w_out, hgrn_lb_logits, hgrn_norm_g, rwkv_mu, rwkv_w0, rwkv_w_up, rwkv_a0, rwkv_a_up, rwkv_g_up, rwkv_k_k, rwkv_k_a, rwkv_r_k, rwkv_ln_g, rwkv_ln_b, ffn_w_up, ffn_conv_w, ffn_conv_b, ffn_w_down):
    d = D_MODEL
    x = (x_prompt.reshape(N_CTX, d), x_sample.reshape(N_DEC, d), 0)
    cond =jnp.concatenate([c_ctx[None, :], c, jnp.zeros((COND_ROWS - 1 - DEC_BATCH, d), F32)], axis=0)
    mod = _mod_call(cond, ada_w, ada_b).reshape(DEPTH, COND_ROWS, 1, 6 * d)

    cos, s1, s2 = _rope_tables()
    g_mean = _block_diag(1.0 / DH_B)
    g_sum = _block_diag(1.0)
    perm = _qb_perm()
    row = lambda t: t.reshape(1, -1)

    new_ctx = None
    for l in range(DEPTH):
        i = l // 2
        mod_l = mod[l]
        if l % 2 == 0:
            w_in = ev_w_in[i]
            w_in = jnp.concatenate([w_in[:, :1536], w_in[:, 1536:2048][:, perm], w_in[:, 2048:]], axis=1).astype(BF16)
            w_out = ev_w_out[i]
            w_out = jnp.concatenate([w_out[:D_A], w_out[D_A:][perm]], axis=0).astype(BF16)
            qg = row(jnp.tile(b_q_norm_g[i], 2))
            kg = row(jnp.tile(b_k_norm_g[i], 2))
            qa, ka, va, qb, kb, vb = _even_proj_call(x, mod_l, row(norm_mix_g[l]), w_in, qg, kg, cos, s1, s2, g_mean)
            lam_init = 0.8 - 0.6 * math.exp(-0.3 * l)
            sub_g = row(a_subln_g[i])
            oa_c, ob_c = _attn_call(qa, qb, ka, va, kb, vb, None, a_lambda[i], sub_g, lam_init,
                                    n_seq=BATCH, seq=SEQ, row0=0)
            cbk = jnp.transpose(cache_b_k[:, i], (0, 2, 1, 3)).reshape(DEC_BATCH, PAST_LEN, HKV_B * DH_B)
            cbv = jnp.transpose(cache_b_v[:, i], (0, 2, 1, 3)).reshape(DEC_BATCH, PAST_LEN, HKV_B * DH_B)
            cache = (cache_a_k[:, i], cache_a_v[:, i], cbk, cbv)
            oa_s, ob_s = _attn_call(qa, qb, ka, va, kb, vb, cache, a_lambda[i], sub_g, lam_init,
                                    n_seq=DEC_BATCH, seq=DEC_SEQ, row0=N_CTX)
            mix_a = (oa_c, oa_s, 0)
            mix_b = (ob_c, ob_s, 0)

            def heads_first(t, nh):
                t = t[:N_CTX].reshape(BATCH, SEQ, nh, -1)
                return jnp.transpose(t, (0, 2, 1, 3))[:, None]

            even_ctx = (heads_first(ka, HA), heads_first(va, HA), heads_first(kb, HKV_B), heads_first(vb, HKV_B))
        else:
            w = od_w_in[i]
            w_h = w[:, :HGRN_PROJ].astype(BF16)
            wr = w[:, HGRN_PROJ:]
            mu = rwkv_mu[i]

            def rwkv_cols(t):
                z = jnp.zeros(t.shape[:-1] + (RWKV_PAD - RWKV_PROJ,), t.dtype)
                return jnp.concatenate([t[..., :1664], t[..., 1728:1856], t[..., 1664:1728], z], axis=-1)

            w_r = rwkv_cols(wr).astype(BF16)
            mu_r = row(rwkv_cols(mu))
            g_l = row(norm_mix_g[l])
            assert x[0] is x[1]
            proj_h = _odd_proj_call(x[0], mod_l, g_l, w_h, jnp.zeros((1, HGRN_PROJ), F32), shift=False)
            proj_r = _odd_proj_call(x[0], mod_l, g_l, w_r, mu_r, shift=True)

            ng = row(hgrn_norm_g[i])
            zero_h = jnp.zeros((BATCH, 2, HC, DK_C, DV_C), F32)
            ones = jnp.ones((LANES, LANES), BF16)
            oc_c, sh_c = _hgrn_call(proj_h, hgrn_lb_logits, ng, zero_h, ones, n_seq=BATCH, seq=SEQ, row0=0, lidx=l)
            oc_s, _ = _hgrn_call(proj_h, hgrn_lb_logits, ng, state_hgrn[:, i], ones, n_seq=DEC_BATCH, seq=DEC_SEQ,
                                 row0=N_CTX, lidx=l)

            pad_rows = lambda t: jnp.concatenate([t, jnp.zeros((LANES - t.shape[0], t.shape[1]), F32)], axis=0)
            wup0 = pad_rows(rwkv_w_up[i, 0])
            wup1 = jnp.concatenate([jnp.zeros((W_LORA, D_D), F32), rwkv_w_up[i, 1]], axis=0)
            w0o, w1o, nkk, bb, kt, gg, rkv = _rwkv_prep_call(
                proj_r, row(rwkv_a0[i]), pad_rows(rwkv_a_up[i]), rwkv_g_up[i], row(rwkv_k_k[i]), row(rwkv_k_a[i]),
                row(rwkv_r_k[i]), rwkv_w0[i], wup0, wup1, g_sum)

            def to_tiles(s):
                b = s.shape[0]
                s = s.reshape(b, 2, HD // 2, 2, DH_D, DH_D)
                return jnp.transpose(s, (0, 1, 2, 4, 3, 5)).reshape(b, 2, HD // 2, DH_D, LANES)

            def from_tiles(s):
                b = s.shape[0]
                s = s.reshape(b, 2, HD // 2, DH_D, 2, DH_D)
                return jnp.transpose(s, (0, 1, 2, 4, 3, 5)).reshape(b, 2, HD, DH_D, DH_D)

            zero_r = jnp.zeros((BATCH, 2, HD // 2, DH_D, LANES), F32)
            q1 = g_sum.astype(BF16)
            wsp = _head_spread().astype(BF16)
            of_c, ob_c, sr_c = _rwkv_scan_call(proj_r, w0o, w1o, nkk, bb, kt, zero_r, q1, wsp,
                                               n_seq=BATCH, seq=SEQ, row0=0)
            of_s, ob_s, _ = _rwkv_scan_call(proj_r, w0o, w1o, nkk, bb, kt, to_tiles(state_rwkv[:, i]), q1, wsp,
                                            n_seq=DEC_BATCH, seq=DEC_SEQ, row0=N_CTX)
            mix_b = _whole(_rwkv_post_call((of_c, of_s, 0), (ob_c, ob_s, 0), rkv, gg, row(rwkv_ln_g[i]),
                                           row(rwkv_ln_b[i]), g_mean))
            mix_a = (oc_c, oc_s, 0)
            w_out = od_w_out[i].astype(BF16)
            odd_ctx = (sh_c[:, None], from_tiles(sr_c)[:, None])

        x1 = _out_proj_call(x, mix_a, mix_b, mod_l, w_out)
        x2 = _ffn_call(x1, mod_l, row(norm_ffn_g[l]), ffn_w_up[l].astype(BF16), ffn_conv_w[l], row(ffn_conv_b[l]),
                       ffn_w_down[l].astype(BF16), row(final_norm_g), final=(l == DEPTH - 1))
        x = _whole(x2)

    y_prompt = x2[:N_CTX].reshape(BATCH, SEQ, d)
    y_sample = x2[N_CTX:].reshape(DEC_BATCH, DEC_SEQ, d)
    return (y_prompt, y_sample) + even_ctx + odd_ctx
```

```python
import functools
import math

import jax
import jax.numpy as jnp
import numpy as np
from jax import lax
from jax.experimental import pallas as pl
from jax.experimental.pallas import tpu as pltpu

D_MODEL = 1024
BATCH = 16
SEQ = 256
DEPTH = 2
DEC_BATCH = 4
DEC_SEQ = 1024
PAST_LEN = 512
GRID_W = 64
ROPE_THETA = 10000.0
EPS = 1e-6
RWKV_GN_EPS = 64e-5
HA = 4
DH_A = 64
DV_A = 2 * DH_A
HB = 8
HKV_B = 2
DH_B = 64
HC = 4
DK_C = 128
DV_C = 128
HD = 8
DH_D = 64
W_LORA = 64
A_LORA = 64
G_LORA = 128
D_FF = 2816

D_A = HA * DV_A
D_B = HB * DH_B
D_C = HC * DV_C
D_D = HD * DH_D
EVEN_PROJ = 2304
HGRN_PROJ = 2560
RWKV_PROJ = 1856
RWKV_PAD = 1920

N_CTX = BATCH * SEQ
N_DEC = DEC_BATCH * DEC_SEQ
N_TOK = N_CTX + N_DEC
COND_ROWS = 8

LANES = 128
SUBLANES = 8
VMEM_LIMIT = 56 * 1024 * 1024

F32 = jnp.float32
BF16 = jnp.bfloat16
HI = lax.Precision.HIGHEST

FF_TILE = 256
HGRN_L = 128
HGRN_HP = 2
RWKV_TB = 128
RWKV_NS = 4
RWKV_GROUP = 1
RWKV_LAG = 3
assert RWKV_LAG < 2 * RWKV_NS // RWKV_GROUP


def _dot(a, b):
    return jnp.dot(a, b, preferred_element_type=F32)


def _dot_hi(a, b):
    return jnp.dot(a, b, preferred_element_type=F32, precision=HI)


def _dot_b16(a, b):
    return _dot(a.astype(BF16), b.astype(BF16))


def _dot_nt(a, b, precision=None):
    return lax.dot_general(a, b, (((1,), (1,)), ((), ())), preferred_element_type=F32, precision=precision)


def _dot_tn(a, b, precision=None):
    return lax.dot_general(a, b, (((0,), (0,)), ((), ())), preferred_element_type=F32, precision=precision)


def _sigmoid(x):
    return 1.0 / (1.0 + jnp.exp(-x))


def _silu(x):
    return x * _sigmoid(x)


def _norm_mod(x, g, sc, sh):
    ms = jnp.mean(x * x, axis=-1, keepdims=True)
    return (x * lax.rsqrt(ms + EPS) * g) * (1.0 + sc) + sh


def _params(sem):
    return pltpu.CompilerParams(dimension_semantics=sem, vmem_limit_bytes=VMEM_LIMIT)


def _cond_row(i, tm):
    r0 = i * tm
    return jnp.where(r0 < N_CTX, 0, 1 + (r0 - N_CTX) // DEC_SEQ)


def _seq_len(i, tm):
    return jnp.where(i * tm < N_CTX, SEQ, DEC_SEQ)


ROW_PAD = SUBLANES


def _zero_row_pads(u_ref, tm):
    for k in range(u_ref.shape[0]):
        u_ref[k, 0:ROW_PAD, :] = jnp.zeros((ROW_PAD, u_ref.shape[2]), u_ref.dtype)
        u_ref[k, ROW_PAD + tm:2 * ROW_PAD + tm, :] = jnp.zeros((ROW_PAD, u_ref.shape[2]), u_ref.dtype)


def _neighbour_rows(u_ref, k, c, ch, seq_len):
    r0 = ROW_PAD + c * ch
    row8 = lax.broadcasted_iota(jnp.int32, (SUBLANES, 1), 0)
    at_start = (c * ch) % seq_len == 0
    at_end = ((c + 1) * ch) % seq_len == 0
    prev = u_ref[k, r0 - 1:r0 - 1 + ch, :]
    nxt = u_ref[k, r0 + 1:r0 + 1 + ch, :]
    prev = jnp.concatenate([jnp.where((row8 == 0) & at_start, 0.0, prev[0:SUBLANES]), prev[SUBLANES:]], axis=0)
    nxt = jnp.concatenate(
        [nxt[:ch - SUBLANES], jnp.where((row8 == SUBLANES - 1) & at_end, 0.0, nxt[ch - SUBLANES:])], axis=0)
    return prev, nxt


def _pick(tm, lo_ref, hi_ref, rows=slice(None)):
    return jnp.where(pl.program_id(0) * tm < N_CTX, lo_ref[rows, :], hi_ref[rows, :])


def _pair_specs(tm, width, pair):
    nct = N_CTX // tm
    hi0 = pair[2] // tm
    return [pl.BlockSpec((tm, width), lambda i: (jnp.minimum(i, nct - 1), 0)),
            pl.BlockSpec((tm, width), lambda i: (hi0 + jnp.maximum(i - nct, 0), 0))]


def _whole(t):
    return (t, t, N_CTX)


def _mod_kernel(c_ref, w_ref, b_ref, o_ref):
    s = _silu(c_ref[...]).astype(BF16)
    o_ref[...] = _dot(s, w_ref[...].astype(BF16)) + b_ref[...]


def _mod_call(cond, ada_w, ada_b):
    tn = 1536
    n = 6 * D_MODEL
    return pl.pallas_call(
        _mod_kernel,
        grid=(DEPTH, n // tn),
        in_specs=[
            pl.BlockSpec((COND_ROWS, D_MODEL), lambda l, j: (0, 0)),
            pl.BlockSpec((None, D_MODEL, tn), lambda l, j: (l, 0, j)),
            pl.BlockSpec((None, 1, tn), lambda l, j: (l, 0, j)),
        ],
        out_specs=pl.BlockSpec((None, COND_ROWS, tn), lambda l, j: (l, 0, j)),
        out_shape=jax.ShapeDtypeStruct((DEPTH, COND_ROWS, n), F32),
        compiler_params=_params(("arbitrary", "arbitrary")),
        name="ada_mod",
    )(cond, ada_w, ada_b.reshape(DEPTH, 1, n))


def _even_proj_kernel(xl_ref, xh_ref, mod_ref, g_ref, w_ref, qg_ref, kg_ref, cos_ref, s1_ref, s2_ref, gm_ref,
                      qa_ref, ka_ref, va_ref, qb_ref, kb_ref, vb_ref, *, tm):
    d = D_MODEL
    gm = gm_ref[...]
    scale = DH_A ** -0.5

    def head_norm(t, g):
        return t * lax.rsqrt(_dot_b16(t * t, gm) + EPS) * g

    for c in range(tm // SEQ):
        rows = slice(c * SEQ, (c + 1) * SEQ)
        h = _norm_mod(_pick(tm, xl_ref, xh_ref, rows), g_ref[...], mod_ref[:, d:2 * d], mod_ref[:, 0:d]).astype(BF16)
        proj = _dot(h, w_ref[...])
        cos, s1, s2 = cos_ref[rows, :], s1_ref[rows, :], s2_ref[rows, :]

        def rope(t):
            return t * cos + pltpu.roll(t, LANES - 16, 1) * s1 + pltpu.roll(t, 16, 1) * s2

        for j in range(4):
            sl = slice(j * LANES, (j + 1) * LANES)
            qa_ref[rows, sl] = rope(proj[:, j * LANES:(j + 1) * LANES]) * scale
            ka_ref[rows, sl] = rope(proj[:, 512 + j * LANES:512 + (j + 1) * LANES])
            va_ref[rows, sl] = proj[:, 1024 + j * LANES:1024 + (j + 1) * LANES]
            qb_ref[rows, sl] = rope(head_norm(proj[:, 1536 + j * LANES:1536 + (j + 1) * LANES], qg_ref[...])) * scale
        kb_ref[rows, :] = rope(head_norm(proj[:, 2048:2176], kg_ref[...]))
        vb_ref[rows, :] = proj[:, 2176:2304]


def _even_proj_call(x, mod_l, g, w, qg, kg, cos, s1, s2, gm):
    tm = 512
    nt = N_TOK // tm
    n_rope_blk = DEC_SEQ // tm

    def rope_idx(i):
        return (jnp.where(i * tm < N_CTX, n_rope_blk, (i - N_CTX // tm) % n_rope_blk), 0)

    full = lambda shape: pl.BlockSpec(shape, lambda i: (0,) * len(shape))
    out512 = pl.BlockSpec((tm, 512), lambda i: (i, 0))
    out128 = pl.BlockSpec((tm, LANES), lambda i: (i, 0))
    return pl.pallas_call(
        functools.partial(_even_proj_kernel, tm=tm),
        grid=(nt,),
        in_specs=_pair_specs(tm, D_MODEL, x) + [
            pl.BlockSpec((None, 1, 6 * D_MODEL), lambda i: (_cond_row(i, tm), 0, 0)),
            full((1, D_MODEL)),
            full((D_MODEL, EVEN_PROJ)),
            full((1, LANES)),
            full((1, LANES)),
            pl.BlockSpec((tm, LANES), rope_idx),
            pl.BlockSpec((tm, LANES), rope_idx),
            pl.BlockSpec((tm, LANES), rope_idx),
            full((LANES, LANES)),
        ],
        out_specs=[out512, out512, out512, out512, out128, out128],
        out_shape=[jax.ShapeDtypeStruct((N_TOK, 512), F32)] * 4 + [jax.ShapeDtypeStruct((N_TOK, LANES), F32)] * 2,
        compiler_params=_params(("arbitrary",)),
        name="even_proj",
    )(x[0], x[1], mod_l, g, w, qg, kg, cos, s1, s2, gm)


def _softmax_pv(q, ks, vs):
    ss = [_dot_nt(q, k) for k in ks]
    m = functools.reduce(jnp.maximum, [jnp.max(s, axis=-1, keepdims=True) for s in ss])
    ps = [jnp.exp(s - m) for s in ss]
    l = functools.reduce(jnp.add, [jnp.sum(p, axis=-1, keepdims=True) for p in ps])
    acc = functools.reduce(jnp.add, [_dot(p.astype(BF16), v) for p, v in zip(ps, vs)])
    return acc / l


def _attn_kernel(*refs, has_cache, lam_init):
    if has_cache:
        (qa_ref, qb_ref, ka_ref, va_ref, kb_ref, vb_ref, cak_ref, cav_ref, cbk_ref, cbv_ref,
         al_ref, sg_ref, oa_ref, ob_ref) = refs
    else:
        qa_ref, qb_ref, ka_ref, va_ref, kb_ref, vb_ref, al_ref, sg_ref, oa_ref, ob_ref = refs
    al = al_ref[...]
    lam = (jnp.exp(jnp.sum(al[0:1] * al[1:2], axis=-1, keepdims=True))
           - jnp.exp(jnp.sum(al[2:3] * al[3:4], axis=-1, keepdims=True)) + lam_init)
    lo = lax.broadcasted_iota(jnp.int32, (1, LANES), 1) < DH_A

    for h in range(HA):
        sl = slice(h * LANES, (h + 1) * LANES)
        q = qa_ref[:, sl]
        ks = [ka_ref[:, sl].astype(BF16)]
        vs = [va_ref[:, sl].astype(BF16)]
        if has_cache:
            ks.insert(0, cak_ref[h].astype(BF16))
            vs.insert(0, cav_ref[h].astype(BF16))
        a1 = _softmax_pv(jnp.where(lo, q, 0.0).astype(BF16), ks, vs)
        a2 = _softmax_pv(jnp.where(lo, 0.0, q).astype(BF16), ks, vs)
        dlt = a1 - lam * a2
        ms = jnp.mean(dlt * dlt, axis=-1, keepdims=True)
        oa_ref[:, sl] = dlt * lax.rsqrt(ms + EPS) * sg_ref[...] * (1.0 - lam_init)

    ks = [kb_ref[...].astype(BF16)]
    vs = [vb_ref[...].astype(BF16)]
    if has_cache:
        ks.insert(0, cbk_ref[...].astype(BF16))
        vs.insert(0, cbv_ref[...].astype(BF16))
    for j in range(HB // 2):
        sl = slice(j * LANES, (j + 1) * LANES)
        q = qb_ref[:, sl]
        o0 = _softmax_pv(jnp.where(lo, q, 0.0).astype(BF16), ks, vs)
        o1 = _softmax_pv(jnp.where(lo, 0.0, q).astype(BF16), ks, vs)
        ob_ref[:, sl] = jnp.where(lo, o0, o1)


def _attn_call(qa, qb, ka, va, kb, vb, cache, a_lambda, subln_g, lam_init, *, n_seq, seq, row0):
    tq = 256
    nq = seq // tq
    qblk0 = row0 // tq
    sblk0 = row0 // seq
    has_cache = cache is not None
    qspec = pl.BlockSpec((tq, 512), lambda b, i: (qblk0 + b * nq + i, 0))
    own512 = pl.BlockSpec((seq, 512), lambda b, i: (sblk0 + b, 0))
    own128 = pl.BlockSpec((seq, LANES), lambda b, i: (sblk0 + b, 0))
    in_specs = [qspec, qspec, own512, own512, own128, own128]
    args = [qa, qb, ka, va, kb, vb]
    if has_cache:
        in_specs += [
            pl.BlockSpec((None, HA, PAST_LEN, LANES), lambda b, i: (b, 0, 0, 0)),
            pl.BlockSpec((None, HA, PAST_LEN, LANES), lambda b, i: (b, 0, 0, 0)),
            pl.BlockSpec((None, PAST_LEN, LANES), lambda b, i: (b, 0, 0)),
            pl.BlockSpec((None, PAST_LEN, LANES), lambda b, i: (b, 0, 0)),
        ]
        args += list(cache)
    in_specs += [pl.BlockSpec((4, DH_A), lambda b, i: (0, 0)), pl.BlockSpec((1, LANES), lambda b, i: (0, 0))]
    args += [a_lambda, subln_g]
    ospec = pl.BlockSpec((tq, 512), lambda b, i: (b * nq + i, 0))
    return pl.pallas_call(
        functools.partial(_attn_kernel, has_cache=has_cache, lam_init=lam_init),
        grid=(n_seq, nq),
        in_specs=in_specs,
        out_specs=[ospec, ospec],
        out_shape=[jax.ShapeDtypeStruct((n_seq * seq, 512), F32)] * 2,
        compiler_params=_params(("arbitrary", "arbitrary")),
        name="attn_dec" if has_cache else "attn_ctx",
    )(*args)


def _out_proj_kernel(xl_ref, xh_ref, al_ref, ah_ref, bl_ref, bh_ref, mod_ref, w_ref, o_ref, *, tm):
    d = D_MODEL
    half = al_ref.shape[1]
    a = _pick(tm, al_ref, ah_ref).astype(BF16)
    b = _pick(tm, bl_ref, bh_ref).astype(BF16)
    mix = _dot(a, w_ref[0:half, :]) + _dot(b, w_ref[half:2 * half, :])
    o_ref[...] = _pick(tm, xl_ref, xh_ref) + mod_ref[:, 2 * d:3 * d] * mix


def _out_proj_call(x, a, b, mod_l, w):
    tm = 512
    return pl.pallas_call(
        functools.partial(_out_proj_kernel, tm=tm),
        grid=(N_TOK // tm,),
        in_specs=_pair_specs(tm, D_MODEL, x) + _pair_specs(tm, 512, a) + _pair_specs(tm, 512, b) + [
            pl.BlockSpec((None, 1, 6 * D_MODEL), lambda i: (_cond_row(i, tm), 0, 0)),
            pl.BlockSpec((D_MODEL, D_MODEL), lambda i: (0, 0)),
        ],
        out_specs=pl.BlockSpec((tm, D_MODEL), lambda i: (i, 0)),
        out_shape=jax.ShapeDtypeStruct((N_TOK, D_MODEL), F32),
        compiler_params=_params(("arbitrary",)),
        name="out_proj",
    )(x[0], x[1], a[0], a[1], b[0], b[1], mod_l, w)


def _ffn_kernel(x_ref, mod_ref, g_ref, wv_ref, wg_ref, cwv_ref, cwg_ref, cbv_ref, cbg_ref, wd_ref, fg_ref,
                o_ref, h_scr, acc_scr, u_scr, *, tm, final):
    d = D_MODEL
    i, j = pl.program_id(0), pl.program_id(1)
    ch = SEQ

    @pl.when(j == 0)
    def _():
        h_scr[...] = _norm_mod(x_ref[...], g_ref[...], mod_ref[:, 4 * d:5 * d], mod_ref[:, 3 * d:4 * d]).astype(BF16)
        acc_scr[...] = jnp.zeros_like(acc_scr)
        _zero_row_pads(u_scr, tm)

    seq_len = _seq_len(i, tm)

    def project(c):
        rows = slice(c * ch, (c + 1) * ch)
        u_scr[0, ROW_PAD + c * ch:ROW_PAD + (c + 1) * ch, :] = _dot(h_scr[rows, :], wv_ref[...])
        u_scr[1, ROW_PAD + c * ch:ROW_PAD + (c + 1) * ch, :] = _dot(h_scr[rows, :], wg_ref[...])

    def conv(k, c, cw_ref, cb_ref):
        prev, nxt = _neighbour_rows(u_scr, k, c, ch, seq_len)
        cur = u_scr[k, ROW_PAD + c * ch:ROW_PAD + (c + 1) * ch, :]
        return cw_ref[0:1, :] * prev + cw_ref[1:2, :] * cur + cw_ref[2:3, :] * nxt + cb_ref[...]

    def activate(c):
        val = conv(0, c, cwv_ref, cbv_ref)
        gate = conv(1, c, cwg_ref, cbg_ref)
        act = (_silu(gate) * val).astype(BF16)
        acc_scr[c * ch:(c + 1) * ch, :] += _dot(act, wd_ref[...])

    n_chunks = tm // ch
    project(0)
    for c in range(n_chunks):
        if c + 1 < n_chunks:
            project(c + 1)
        activate(c)

    @pl.when(j == pl.num_programs(1) - 1)
    def _():
        y = x_ref[...] + mod_ref[:, 5 * d:6 * d] * acc_scr[...]
        if final:
            ms = jnp.mean(y * y, axis=-1, keepdims=True)
            y = y * lax.rsqrt(ms + EPS) * fg_ref[...]
        o_ref[...] = y


def _ffn_call(x, mod_l, g, w_up, conv_w, conv_b, w_down, final_g, *, final):
    tm = DEC_SEQ
    nf = D_FF // FF_TILE
    up = lambda j: j
    down = lambda j: j
    return pl.pallas_call(
        functools.partial(_ffn_kernel, tm=tm, final=final),
        grid=(N_TOK // tm, nf),
        in_specs=[
            pl.BlockSpec((tm, D_MODEL), lambda i, j: (i, 0)),
            pl.BlockSpec((None, 1, 6 * D_MODEL), lambda i, j: (_cond_row(i, tm), 0, 0)),
            pl.BlockSpec((1, D_MODEL), lambda i, j: (0, 0)),
            pl.BlockSpec((D_MODEL, FF_TILE), lambda i, j: (0, up(j))),
            pl.BlockSpec((D_MODEL, FF_TILE), lambda i, j: (0, nf + up(j))),
            pl.BlockSpec((3, FF_TILE), lambda i, j: (0, up(j))),
            pl.BlockSpec((3, FF_TILE), lambda i, j: (0, nf + up(j))),
            pl.BlockSpec((1, FF_TILE), lambda i, j: (0, up(j))),
            pl.BlockSpec((1, FF_TILE), lambda i, j: (0, nf + up(j))),
            pl.BlockSpec((FF_TILE, D_MODEL), lambda i, j: (down(j), 0)),
            pl.BlockSpec((1, D_MODEL), lambda i, j: (0, 0)),
        ],
        out_specs=pl.BlockSpec((tm, D_MODEL), lambda i, j: (i, 0)),
        out_shape=jax.ShapeDtypeStruct((N_TOK, D_MODEL), F32),
        scratch_shapes=[pltpu.VMEM((tm, D_MODEL), BF16), pltpu.VMEM((tm, D_MODEL), F32),
                        pltpu.VMEM((2, tm + 2 * SUBLANES, FF_TILE), F32)],
        compiler_params=_params(("arbitrary", "arbitrary")),
        name="conv_ffn",
    )(x, mod_l, g, w_up, w_up, conv_w, conv_w, conv_b, conv_b, w_down, final_g)


def _odd_proj_kernel(x_ref, mod_ref, g_ref, w_ref, mu_ref, o_ref, h_scr, *u_scr, tm, shift):
    d = D_MODEL
    i, j = pl.program_id(0), pl.program_id(1)
    ch = SEQ
    n_chunks = tm // ch

    @pl.when(j == 0)
    def _():
        h_scr[...] = _norm_mod(x_ref[...], g_ref[...], mod_ref[:, d:2 * d], mod_ref[:, 0:d]).astype(BF16)
        if shift:
            _zero_row_pads(u_scr[0], tm)

    if not shift:
        for c in range(n_chunks):
            o_ref[c * ch:(c + 1) * ch, :] = _dot(h_scr[c * ch:(c + 1) * ch, :], w_ref[...])
        return

    u = u_scr[0]
    seq_len = _seq_len(i, tm)

    def project(c):
        u[0, ROW_PAD + c * ch:ROW_PAD + (c + 1) * ch, :] = _dot(h_scr[c * ch:(c + 1) * ch, :], w_ref[...])

    project(0)
    for c in range(n_chunks):
        if c + 1 < n_chunks:
            project(c + 1)
        prev, nxt = _neighbour_rows(u, 0, c, ch, seq_len)
        p = u[0, ROW_PAD + c * ch:ROW_PAD + (c + 1) * ch, :]
        o_ref[c * ch:(c + 1) * ch, :] = p + mu_ref[...] * (0.5 * (prev + nxt) - p)


def _odd_proj_call(x, mod_l, g, w, mu, *, shift):
    tm = DEC_SEQ
    n = w.shape[1]
    tn = n if shift else n // 2
    return pl.pallas_call(
        functools.partial(_odd_proj_kernel, tm=tm, shift=shift),
        grid=(N_TOK // tm, n // tn),
        in_specs=[
            pl.BlockSpec((tm, D_MODEL), lambda i, j: (i, 0)),
            pl.BlockSpec((None, 1, 6 * D_MODEL), lambda i, j: (_cond_row(i, tm), 0, 0)),
            pl.BlockSpec((1, D_MODEL), lambda i, j: (0, 0)),
            pl.BlockSpec((D_MODEL, tn), lambda i, j: (0, j)),
            pl.BlockSpec((1, tn), lambda i, j: (0, j)),
        ],
        out_specs=pl.BlockSpec((tm, tn), lambda i, j: (i, j)),
        out_shape=jax.ShapeDtypeStruct((N_TOK, n), F32),
        scratch_shapes=[pltpu.VMEM((tm, D_MODEL), BF16)]
        + ([pltpu.VMEM((1, tm + 2 * ROW_PAD, tn), F32)] if shift else []),
        compiler_params=_params(("arbitrary", "arbitrary")),
        name="odd_proj_rwkv" if shift else "odd_proj_hgrn",
    )(x, mod_l, g, w, mu)


def _hgrn_kernel(q_ref, ff_ref, fb_ref, v_ref, gc_ref, lbl_ref, ng_ref, s0_ref, ones_ref, o_ref, sfin_ref,
                 oacc_f, oacc_b, st_scr, qd_scr, add_scr, dec_scr, start_scr, *, seq, lidx):
    L = HGRN_L
    nc = seq // L
    nb = L // SUBLANES
    ones = ones_ref[...]
    rowi = lax.broadcasted_iota(jnp.int32, (L, L), 0)
    coli = lax.broadcasted_iota(jnp.int32, (L, L), 1)
    rowv = lax.broadcasted_iota(jnp.int32, (L, 1), 0)

    def lower_bound(drc):
        lg = lbl_ref[drc]
        e = jnp.exp(lg - jnp.max(lg, axis=0, keepdims=True))
        sm = e / jnp.sum(e, axis=0, keepdims=True)
        return functools.reduce(jnp.add, [sm[i:i + 1, :] for i in range(1, lidx + 1)])

    def spread(t, s_l):
        return jnp.concatenate(
            [jnp.broadcast_to(t[b * SUBLANES + s_l:b * SUBLANES + s_l + 1, :], (SUBLANES, LANES)) for b in range(nb)],
            axis=0)

    def chunk(rev, r0, f_ref, lb, tri, hs):
        q = _silu(q_ref[pl.ds(r0, L), hs])
        f = lb + (1.0 - lb) * _sigmoid(f_ref[pl.ds(r0, L), hs])
        k = 1.0 - f
        lf = jnp.log(f)
        v = v_ref[pl.ds(r0, L), hs].astype(BF16)
        cum = _dot_hi(tri, lf)
        cum_ex = cum - lf
        q_dec = (q * jnp.exp(cum)).astype(BF16)

        scores = jnp.zeros((L, L), F32)
        sub = rowv % SUBLANES
        h = L // 2
        while h >= 1:
            if h >= SUBLANES:
                pieces = []
                for a in range(0, L, 2 * h):
                    edge = a + h - 1 if rev else a + h
                    pieces.append(jnp.broadcast_to(cum_ex[edge:edge + 1, :], (2 * h, LANES)))
                anchor = jnp.concatenate(pieces, axis=0) if len(pieces) > 1 else pieces[0]
            else:
                anchor = None
                for a in range(0, SUBLANES, 2 * h):
                    edge = a + h - 1 if rev else a + h
                    cand = spread(cum_ex, edge)
                    anchor = cand if anchor is None else jnp.where(sub >= a, cand, anchor)
            is_q = ((rowv // h) % 2) == (0 if rev else 1)
            d = cum - anchor
            e = jnp.exp(jnp.where(is_q, d, -d))
            qe = jnp.where(is_q, q * e, 0.0).astype(BF16)
            ke = jnp.where(is_q, 0.0, k * e).astype(BF16)
            same_pair = (rowi // (2 * h)) == (coli // (2 * h))
            scores = scores + jnp.where(same_pair, _dot_nt(qe, ke), 0.0)
            h //= 2

        own = _dot((q * k).astype(BF16), ones)
        own = own[:, 0:L] if L <= LANES else jnp.concatenate([own] * (L // LANES), axis=1)
        scores = jnp.where(rowi == coli, own, scores)
        o = _dot(scores.astype(BF16), v)

        end = 0 if rev else L - 1
        cend = cum[end:end + 1, :]
        kd = (k * jnp.exp(cend - cum)).astype(BF16)
        return q_dec, o, _dot_tn(v, kd), jnp.exp(cend)

    lb_f, lb_b = lower_bound(0), lower_bound(1)
    tri_f = jnp.where(coli <= rowi, 1.0, 0.0).astype(F32)
    tri_b = jnp.where(coli >= rowi, 1.0, 0.0).astype(F32)
    heads = [slice(hh * LANES, (hh + 1) * LANES) for hh in range(HGRN_HP)]
    chains = [(hh, hs, drc) for hh, hs in enumerate(heads) for drc in range(2)]
    oaccs = (oacc_f, oacc_b)

    def local(ci, carry):
        r0 = pl.multiple_of(ci * L, L)
        for c, (hh, hs, drc) in enumerate(chains):
            q_dec, o, add, dec = chunk(drc == 1, r0, fb_ref if drc else ff_ref, (lb_b if drc else lb_f)[:, hs],
                                       tri_b if drc else tri_f, hs)
            qd_scr[c, pl.ds(r0, L), :] = q_dec
            oaccs[drc][pl.ds(r0, L), hs] = o
            add_scr[c * nc + ci] = add
            dec_scr[c * nc + ci] = jnp.broadcast_to(dec, (SUBLANES, LANES))
        return carry

    lax.fori_loop(0, nc, local, 0, unroll=min(2, nc))

    for c, (hh, hs, drc) in enumerate(chains):
        st_scr[c] = s0_ref[drc, hh].T

    def scan(n, carry):
        for c, (hh, hs, drc) in enumerate(chains):
            slot = c * nc + ((nc - 1 - n) if drc else n)
            st = st_scr[c]
            start_scr[slot] = st.astype(BF16)
            st_scr[c] = st * dec_scr[slot][0:1, :] + add_scr[slot]
        return carry

    lax.fori_loop(0, nc, scan, 0)

    def carried(ci, carry):
        r0 = pl.multiple_of(ci * L, L)
        for c, (hh, hs, drc) in enumerate(chains):
            oaccs[drc][pl.ds(r0, L), hs] += _dot_nt(qd_scr[c, pl.ds(r0, L), :], start_scr[c * nc + ci])
        return carry

    lax.fori_loop(0, nc, carried, 0)

    for hh, hs in enumerate(heads):
        sfin_ref[0, hh] = st_scr[2 * hh].T
        sfin_ref[1, hh] = st_scr[2 * hh + 1].T
        o = oacc_f[:, hs] + oacc_b[:, hs]
        ms = jnp.mean(o * o, axis=-1, keepdims=True)
        o_ref[:, hs] = o * lax.rsqrt(ms + EPS) * ng_ref[...] * _silu(gc_ref[:, hs])


def _hgrn_call(proj_h, lb_logits, norm_g, s0, ones, *, n_seq, seq, row0, lidx):
    sblk0 = row0 // seq

    hp = HGRN_HP
    width = hp * LANES

    def col(section):
        return pl.BlockSpec((seq, width), lambda b, h: (sblk0 + b, section * (HC // hp) + h))

    st_spec = pl.BlockSpec((None, 2, hp, DK_C, DV_C), lambda b, h: (b, 0, h, 0, 0))
    return pl.pallas_call(
        functools.partial(_hgrn_kernel, seq=seq, lidx=lidx),
        grid=(n_seq, HC // hp),
        in_specs=[
            col(0), col(1), col(2), col(3), col(4),
            pl.BlockSpec((2, DEPTH, width), lambda b, h: (0, 0, h)),
            pl.BlockSpec((1, LANES), lambda b, h: (0, 0)),
            st_spec,
            pl.BlockSpec((LANES, LANES), lambda b, h: (0, 0)),
        ],
        out_specs=[pl.BlockSpec((seq, width), lambda b, h: (b, h)), st_spec],
        out_shape=[jax.ShapeDtypeStruct((n_seq * seq, D_C), F32),
                   jax.ShapeDtypeStruct((n_seq, 2, HC, DK_C, DV_C), F32)],
        scratch_shapes=[pltpu.VMEM((seq, width), F32), pltpu.VMEM((seq, width), F32),
                        pltpu.VMEM((2 * hp, DV_C, DK_C), F32),
                        pltpu.VMEM((2 * hp, seq, DK_C), BF16),
                        pltpu.VMEM((2 * hp * (seq // HGRN_L), DV_C, DK_C), F32),
                        pltpu.VMEM((2 * hp * (seq // HGRN_L), SUBLANES, DK_C), F32),
                        pltpu.VMEM((2 * hp * (seq // HGRN_L), DV_C, DK_C), BF16)],
        compiler_params=_params(("arbitrary", "arbitrary")),
        name="hgrn_dec" if row0 else "hgrn_ctx",
    )(proj_h, proj_h, proj_h, proj_h, proj_h, lb_logits, norm_g, s0, ones)


def _rwkv_prep_kernel(p_ref, a0_ref, aup_ref, gup_ref, kk_ref, ka_ref, rk_ref, w0_ref, wup0_ref, wup1_ref,
                      gs_ref, w0o_ref, w1o_ref, nkk_ref, bb_ref, kt_ref, g_ref, rkv_ref):
    r = p_ref[:, 0:512]
    k = p_ref[:, 512:1024]
    v = p_ref[:, 1024:1536]
    wd = p_ref[:, 1536:1664]
    gd = p_ref[:, 1664:1792]
    ad = p_ref[:, 1792:1920]
    gs = gs_ref[...]

    def group_sum(t):
        return jnp.concatenate([_dot_b16(t[:, j * LANES:(j + 1) * LANES], gs) for j in range(4)], axis=1)

    a = _sigmoid(a0_ref[...] + _dot_b16(ad, aup_ref[...]))
    g_ref[...] = _dot_b16(_sigmoid(gd), gup_ref[...])
    kkr = k * kk_ref[...]
    kk = kkr / jnp.maximum(jnp.sqrt(group_sum(kkr * kkr)), 1e-12)
    kt = k * (1.0 + (a - 1.0) * ka_ref[...])
    th = jnp.tanh(wd)
    decay = math.exp(-0.5)
    w0o_ref[...] = jnp.exp(-decay * _sigmoid(w0_ref[0:1, :] + _dot_b16(th, wup0_ref[...])))
    w1o_ref[...] = jnp.exp(-decay * _sigmoid(w0_ref[1:2, :] + _dot_b16(th, wup1_ref[...])))
    nkk_ref[...] = -kk
    bb_ref[...] = kk * a
    kt_ref[...] = kt
    rkv_ref[...] = group_sum(r * kt * rk_ref[...]) * v


def _rwkv_prep_call(proj_r, a0, aup, gup, kk_k, k_a, r_k, w0, wup0, wup1, gs):
    tm = 512
    full = lambda shape: pl.BlockSpec(shape, lambda i: (0,) * len(shape))
    ospec = pl.BlockSpec((tm, D_D), lambda i: (i, 0))
    return pl.pallas_call(
        _rwkv_prep_kernel,
        grid=(N_TOK // tm,),
        in_specs=[
            pl.BlockSpec((tm, RWKV_PAD), lambda i: (i, 0)),
            full((1, D_D)), full((LANES, D_D)), full((LANES, D_D)), full((1, D_D)), full((1, D_D)), full((1, D_D)),
            full((2, D_D)), full((LANES, D_D)), full((LANES, D_D)), full((LANES, LANES)),
        ],
        out_specs=[ospec] * 7,
        out_shape=[jax.ShapeDtypeStruct((N_TOK, D_D), F32)] * 7,
        compiler_params=_params(("arbitrary",)),
        name="rwkv_prep",
    )(proj_r, a0, aup, gup, kk_k, k_a, r_k, w0, wup0, wup1, gs)


def _rwkv_scan_kernel(rf_ref, wf_ref, nf_ref, bf_ref, kf_ref, vf_ref, rb_ref, wb_ref, nb_ref, bb_ref, kb_ref,
                      vb_ref, s0_ref, q1_ref, wsp_ref, of_ref, ob_ref, sfin_ref, *scratch):
    tb = pl.program_id(1)
    ns = RWKV_NS
    ngrp = D_D // LANES
    steps = RWKV_TB
    members = [(s, drc) for s in range(ns) for drc in range(2)]
    groups = [members[i:i + RWKV_GROUP] for i in range(0, len(members), RWKV_GROUP)]
    st, ot, pabuf, obuf = (scratch[i * len(groups):(i + 1) * len(groups)] for i in range(4))
    m_rows = RWKV_GROUP * ngrp * DH_D
    dirs = ((rf_ref, wf_ref, nf_ref, bf_ref, kf_ref, vf_ref), (rb_ref, wb_ref, nb_ref, bb_ref, kb_ref, vb_ref))
    o_refs = (of_ref, ob_ref)

    def rows(c, n=DH_D, base=0):
        return slice(base + c * n, base + (c + 1) * n)

    def chains(q):
        return [(m * ngrp + g, m, s, drc, g) for m, (s, drc) in enumerate(groups[q]) for g in range(ngrp)]

    @pl.when(tb == 0)
    def _():
        for q in range(len(groups)):
            for c, _, s, drc, g in chains(q):
                st[q][rows(c), :] = s0_ref[s, drc, g]

    for q in range(len(groups)):
        ot[q][...] = jnp.zeros_like(ot[q])
    lane = lax.broadcasted_iota(jnp.int32, (1, LANES), 1)
    diag = (lax.broadcasted_iota(jnp.int32, (DH_D, LANES), 1) & (DH_D - 1)) == lax.broadcasted_iota(
        jnp.int32, (DH_D, LANES), 0)

    def step_group(t8, carry):
        bases = (pl.multiple_of(t8 * SUBLANES, SUBLANES), pl.multiple_of(steps - (t8 + 1) * SUBLANES, SUBLANES))
        blk = [[[ref[s, pl.ds(bases[drc], SUBLANES), :] for ref in dirs[drc]] for s, drc in grp] for grp in groups]

        def issue(j, q):
            for c, m, s, drc, g in chains(q):
                loc = (SUBLANES - 1 - j) if drc else j
                sl = slice(g * LANES, (g + 1) * LANES)
                n_b, v_b = blk[q][m][2], blk[q][m][5]
                pabuf[q][rows(c), :] = (st[q][rows(c), :] * n_b[loc:loc + 1, sl]).astype(BF16)
                pabuf[q][rows(c, base=m_rows), :] = jnp.where(diag, v_b[loc:loc + 1, sl], 0.0).astype(BF16)
            return _dot(pabuf[q][...], q1_ref[...])

        def retire(j, q, sums):
            for c, m, s, drc, g in chains(q):
                loc = (SUBLANES - 1 - j) if drc else j
                sl = slice(g * LANES, (g + 1) * LANES)
                r_b, w_b, _, b_b, k_b, _ = blk[q][m]
                sv = (st[q][rows(c), :] * w_b[loc:loc + 1, sl] + sums[rows(c)] * b_b[loc:loc + 1, sl]
                      + sums[rows(c, base=m_rows)] * k_b[loc:loc + 1, sl])
                st[q][rows(c), :] = sv
                obuf[q][rows(c), loc * LANES:(loc + 1) * LANES] = (sv * r_b[loc:loc + 1, sl]).astype(BF16)

        order = [(j, q) for j in range(SUBLANES) for q in range(len(groups))]
        inflight = []
        for slot, (j, q) in enumerate(order):
            inflight.append((j, q, issue(j, q)))
            if len(inflight) > RWKV_LAG:
                retire(*inflight.pop(0))
        for item in inflight:
            retire(*item)
        for q in range(len(groups)):
            o_all = _dot(obuf[q][...], wsp_ref[...])
            for c, m, s, drc, g in chains(q):
                fresh = (lane >= bases[drc]) & (lane < bases[drc] + SUBLANES)
                top, bot = slice(c * LANES, c * LANES + DH_D), slice(c * LANES + DH_D, (c + 1) * LANES)
                ot[q][top, :] = jnp.where(fresh, o_all[rows(c), 0:LANES], ot[q][top, :])
                ot[q][bot, :] = jnp.where(fresh, o_all[rows(c), LANES:2 * LANES], ot[q][bot, :])
        return carry

    lax.fori_loop(0, steps // SUBLANES, step_group, 0)

    for q in range(len(groups)):
        for c, m, s, drc, g in chains(q):
            o_refs[drc][s, :, g * LANES:(g + 1) * LANES] = ot[q][rows(c, LANES), :].T

    @pl.when(tb == pl.num_programs(1) - 1)
    def _():
        for q in range(len(groups)):
            for c, m, s, drc, g in chains(q):
                sfin_ref[s, drc, g] = st[q][rows(c), :]


def _rwkv_scan_call(r_src, w0, w1, nkk, bb, kt, s0, q1, wsp, *, n_seq, seq, row0):
    ns = RWKV_NS
    ntb = seq // RWKV_TB
    ngrp = D_D // LANES
    grp0 = row0 // (seq * ns)

    def view(t):
        return t.reshape(N_TOK // (seq * ns), ns, ntb, RWKV_TB, t.shape[-1])

    def tok(rev, cb=0):
        if rev:
            return pl.BlockSpec((None, ns, None, RWKV_TB, D_D), lambda b, t: (grp0 + b, 0, ntb - 1 - t, 0, cb))
        return pl.BlockSpec((None, ns, None, RWKV_TB, D_D), lambda b, t: (grp0 + b, 0, t, 0, cb))

    def out(rev):
        if rev:
            return pl.BlockSpec((None, ns, None, RWKV_TB, D_D), lambda b, t: (b, 0, ntb - 1 - t, 0, 0))
        return pl.BlockSpec((None, ns, None, RWKV_TB, D_D), lambda b, t: (b, 0, t, 0, 0))

    st_spec = pl.BlockSpec((ns, 2, ngrp, DH_D, LANES), lambda b, t: (b, 0, 0, 0, 0))
    in_specs = []
    for rev in (False, True):
        in_specs += [tok(rev, 0), tok(rev), tok(rev), tok(rev), tok(rev), tok(rev, 2)]
    in_specs += [st_spec, pl.BlockSpec((LANES, LANES), lambda b, t: (0, 0)),
                 pl.BlockSpec((SUBLANES * LANES, 2 * LANES), lambda b, t: (0, 0))]
    rv, w0v, w1v, nv, bv, kv = [view(t) for t in (r_src, w0, w1, nkk, bb, kt)]
    o_shape = jax.ShapeDtypeStruct((n_seq // ns, ns, ntb, RWKV_TB, D_D), F32)
    m_rows = RWKV_GROUP * ngrp * DH_D
    n_groups = ns * 2 // RWKV_GROUP
    scratch = ([pltpu.VMEM((m_rows, LANES), F32)] * n_groups + [pltpu.VMEM((2 * m_rows, LANES), F32)] * n_groups
               + [pltpu.VMEM((2 * m_rows, LANES), BF16)] * n_groups
               + [pltpu.VMEM((m_rows, SUBLANES * LANES), BF16)] * n_groups)
    o_f, o_b, s_fin = pl.pallas_call(
        _rwkv_scan_kernel,
        grid=(n_seq // ns, ntb),
        in_specs=in_specs,
        out_specs=[out(False), out(True), st_spec],
        out_shape=[o_shape, o_shape, jax.ShapeDtypeStruct((n_seq, 2, ngrp, DH_D, LANES), F32)],
        scratch_shapes=scratch,
        compiler_params=_params(("arbitrary", "arbitrary")),
        name="rwkv_dec" if row0 else "rwkv_ctx",
    )(rv, w0v, nv, bv, kv, rv, rv, w1v, nv, bv, kv, rv, s0, q1, wsp)
    return o_f.reshape(n_seq * seq, D_D), o_b.reshape(n_seq * seq, D_D), s_fin


def _rwkv_post_kernel(ofl_ref, ofh_ref, obl_ref, obh_ref, rkv_ref, g_ref, lng_ref, lnb_ref, gm_ref, o_ref, *, tm):
    gm = gm_ref[...]
    o_all = _pick(tm, ofl_ref, ofh_ref) + _pick(tm, obl_ref, obh_ref)
    for j in range(D_D // LANES):
        sl = slice(j * LANES, (j + 1) * LANES)
        o = o_all[:, sl]
        dlt = o - _dot_b16(o, gm)
        var = _dot_b16(dlt * dlt, gm)
        y = dlt * lax.rsqrt(var + RWKV_GN_EPS) * lng_ref[:, sl] + lnb_ref[:, sl]
        o_ref[:, sl] = (y + rkv_ref[:, sl]) * g_ref[:, sl]


def _rwkv_post_call(o_f, o_b, rkv, g, ln_g, ln_b, gm):
    tm = 512
    tok = pl.BlockSpec((tm, D_D), lambda i: (i, 0))
    row = pl.BlockSpec((1, D_D), lambda i: (0, 0))
    return pl.pallas_call(
        functools.partial(_rwkv_post_kernel, tm=tm),
        grid=(N_TOK // tm,),
        in_specs=_pair_specs(tm, D_D, o_f) + _pair_specs(tm, D_D, o_b)
        + [tok, tok, row, row, pl.BlockSpec((LANES, LANES), lambda i: (0, 0))],
        out_specs=tok,
        out_shape=jax.ShapeDtypeStruct((N_TOK, D_D), F32),
        compiler_params=_params(("arbitrary",)),
        name="rwkv_post",
    )(o_f[0], o_f[1], o_b[0], o_b[1], rkv, g, ln_g, ln_b, gm)


def _rope_tables():
    pos = np.arange(DEC_SEQ)
    pr, pc = pos // GRID_W, pos % GRID_W
    lane = np.arange(LANES)
    dd = lane % DH_A
    use_col = (dd // 32) == 1
    j = dd % 16
    is_lo = (dd % 32) < 16
    freq = ROPE_THETA ** (-(j.astype(np.float64)) / 16.0)
    p = np.where(use_col[None, :], pc[:, None], pr[:, None]).astype(np.float64)
    ang = (p.astype(np.float32) * freq.astype(np.float32)[None, :]).astype(np.float32)
    cos = np.cos(ang).astype(np.float32)
    sin = np.sin(ang).astype(np.float32)
    s1 = np.where(is_lo[None, :], -sin, 0.0).astype(np.float32)
    s2 = np.where(is_lo[None, :], 0.0, sin).astype(np.float32)
    ident = 512
    cos = np.concatenate([cos, np.ones((ident, LANES), np.float32)], 0)
    s1 = np.concatenate([s1, np.zeros((ident, LANES), np.float32)], 0)
    s2 = np.concatenate([s2, np.zeros((ident, LANES), np.float32)], 0)
    return jnp.asarray(cos), jnp.asarray(s1), jnp.asarray(s2)


def _block_diag(value):
    m = np.zeros((LANES, LANES), np.float32)
    half = LANES // 2
    m[:half, :half] = value
    m[half:, half:] = value
    return jnp.asarray(m)


def _head_spread():
    step = np.arange(SUBLANES * LANES) // LANES
    head = (np.arange(SUBLANES * LANES) % LANES) // DH_D
    out_head = np.arange(2 * LANES) // LANES
    out_step = (np.arange(2 * LANES) % LANES) % SUBLANES
    m = (head[:, None] == out_head[None, :]) & (step[:, None] == out_step[None, :])
    return jnp.asarray(m.astype(np.float32))


def _qb_perm():
    idx = np.zeros(D_B, np.int32)
    for j in range(HB // 2):
        for hh in range(2):
            for dch in range(DH_B):
                idx[j * LANES + hh * DH_B + dch] = (hh * (HB // 2) + j) * DH_B + dch
    return idx


def kernel(x_prompt, x_sample, cache_a_k, cache_a_v, cache_b_k, cache_b_v, state_hgrn, state_rwkv, c, c_ctx, ada_w, ada_b, norm_mix_g, norm_ffn_g, final_norm_g, ev_w_in, ev_w_out, a_lambda, a_subln_g, b_q_norm_g, b_k_norm_g, od_w_in, od_w_out, hgrn_lb_logits, hgrn_norm_g, rwkv_mu, rwkv_w0, rwkv_w_up, rwkv_a0, rwkv_a_up, rwkv_g_up, rwkv_k_k, rwkv_k_a, rwkv_r_k, rwkv_ln_g, rwkv_ln_b, ffn_w_up, ffn_conv_w, ffn_conv_b, ffn_w_down):
    d = D_MODEL
    x = (x_prompt.reshape(N_CTX, d), x_sample.reshape(N_DEC, d), 0)
    cond =jnp.concatenate([c_ctx[None, :], c, jnp.zeros((COND_ROWS - 1 - DEC_BATCH, d), F32)], axis=0)
    mod = _mod_call(cond, ada_w, ada_b).reshape(DEPTH, COND_ROWS, 1, 6 * d)

    cos, s1, s2 = _rope_tables()
    g_mean = _block_diag(1.0 / DH_B)
    g_sum = _block_diag(1.0)
    perm = _qb_perm()
    row = lambda t: t.reshape(1, -1)

    new_ctx = None
    for l in range(DEPTH):
        i = l // 2
        mod_l = mod[l]
        if l % 2 == 0:
            w_in = ev_w_in[i]
            w_in = jnp.concatenate([w_in[:, :1536], w_in[:, 1536:2048][:, perm], w_in[:, 2048:]], axis=1).astype(BF16)
            w_out = ev_w_out[i]
            w_out = jnp.concatenate([w_out[:D_A], w_out[D_A:][perm]], axis=0).astype(BF16)
            qg = row(jnp.tile(b_q_norm_g[i], 2))
            kg = row(jnp.tile(b_k_norm_g[i], 2))
            qa, ka, va, qb, kb, vb = _even_proj_call(x, mod_l, row(norm_mix_g[l]), w_in, qg, kg, cos, s1, s2, g_mean)
            lam_init = 0.8 - 0.6 * math.exp(-0.3 * l)
            sub_g = row(a_subln_g[i])
            oa_c, ob_c = _attn_call(qa, qb, ka, va, kb, vb, None, a_lambda[i], sub_g, lam_init,
                                    n_seq=BATCH, seq=SEQ, row0=0)
            cbk = jnp.transpose(cache_b_k[:, i], (0, 2, 1, 3)).reshape(DEC_BATCH, PAST_LEN, HKV_B * DH_B)
            cbv = jnp.transpose(cache_b_v[:, i], (0, 2, 1, 3)).reshape(DEC_BATCH, PAST_LEN, HKV_B * DH_B)
            cache = (cache_a_k[:, i], cache_a_v[:, i], cbk, cbv)
            oa_s, ob_s = _attn_call(qa, qb, ka, va, kb, vb, cache, a_lambda[i], sub_g, lam_init,
                                    n_seq=DEC_BATCH, seq=DEC_SEQ, row0=N_CTX)
            mix_a = (oa_c, oa_s, 0)
            mix_b = (ob_c, ob_s, 0)

            def heads_first(t, nh):
                t = t[:N_CTX].reshape(BATCH, SEQ, nh, -1)
                return jnp.transpose(t, (0, 2, 1, 3))[:, None]

            even_ctx = (heads_first(ka, HA), heads_first(va, HA), heads_first(kb, HKV_B), heads_first(vb, HKV_B))
        else:
            w = od_w_in[i]
            w_h = w[:, :HGRN_PROJ].astype(BF16)
            wr = w[:, HGRN_PROJ:]
            mu = rwkv_mu[i]

            def rwkv_cols(t):
                z = jnp.zeros(t.shape[:-1] + (RWKV_PAD - RWKV_PROJ,), t.dtype)
                return jnp.concatenate([t[..., :1664], t[..., 1728:1856], t[..., 1664:1728], z], axis=-1)

            w_r = rwkv_cols(wr).astype(BF16)
            mu_r = row(rwkv_cols(mu))
            g_l = row(norm_mix_g[l])
            assert x[0] is x[1]
            proj_h = _odd_proj_call(x[0], mod_l, g_l, w_h, jnp.zeros((1, HGRN_PROJ), F32), shift=False)
            proj_r = _odd_proj_call(x[0], mod_l, g_l, w_r, mu_r, shift=True)

            ng = row(hgrn_norm_g[i])
            zero_h = jnp.zeros((BATCH, 2, HC, DK_C, DV_C), F32)
            ones = jnp.ones((LANES, LANES), BF16)
            oc_c, sh_c = _hgrn_call(proj_h, hgrn_lb_logits, ng, zero_h, ones, n_seq=BATCH, seq=SEQ, row0=0, lidx=l)
            oc_s, _ = _hgrn_call(proj_h, hgrn_lb_logits, ng, state_hgrn[:, i], ones, n_seq=DEC_BATCH, seq=DEC_SEQ,
                                 row0=N_CTX, lidx=l)

            pad_rows = lambda t: jnp.concatenate([t, jnp.zeros((LANES - t.shape[0], t.shape[1]), F32)], axis=0)
            wup0 = pad_rows(rwkv_w_up[i, 0])
            wup1 = jnp.concatenate([jnp.zeros((W_LORA, D_D), F32), rwkv_w_up[i, 1]], axis=0)
            w0o, w1o, nkk, bb, kt, gg, rkv = _rwkv_prep_call(
                proj_r, row(rwkv_a0[i]), pad_rows(rwkv_a_up[i]), rwkv_g_up[i], row(rwkv_k_k[i]), row(rwkv_k_a[i]),
                row(rwkv_r_k[i]), rwkv_w0[i], wup0, wup1, g_sum)

            def to_tiles(s):
                b = s.shape[0]
                s = s.reshape(b, 2, HD // 2, 2, DH_D, DH_D)
                return jnp.transpose(s, (0, 1, 2, 4, 3, 5)).reshape(b, 2, HD // 2, DH_D, LANES)

            def from_tiles(s):
                b = s.shape[0]
                s = s.reshape(b, 2, HD // 2, DH_D, 2, DH_D)
                return jnp.transpose(s, (0, 1, 2, 4, 3, 5)).reshape(b, 2, HD, DH_D, DH_D)

            zero_r = jnp.zeros((BATCH, 2, HD // 2, DH_D, LANES), F32)
            q1 = g_sum.astype(BF16)
            wsp = _head_spread().astype(BF16)
            of_c, ob_c, sr_c = _rwkv_scan_call(proj_r, w0o, w1o, nkk, bb, kt, zero_r, q1, wsp,
                                               n_seq=BATCH, seq=SEQ, row0=0)
            of_s, ob_s, _ = _rwkv_scan_call(proj_r, w0o, w1o, nkk, bb, kt, to_tiles(state_rwkv[:, i]), q1, wsp,
                                            n_seq=DEC_BATCH, seq=DEC_SEQ, row0=N_CTX)
            mix_b = _whole(_rwkv_post_call((of_c, of_s, 0), (ob_c, ob_s, 0), rkv, gg, row(rwkv_ln_g[i]),
                                           row(rwkv_ln_b[i]), g_mean))
            mix_a = (oc_c, oc_s, 0)
            w_out = od_w_out[i].astype(BF16)
            odd_ctx = (sh_c[:, None], from_tiles(sr_c)[:, None])

        x1 = _out_proj_call(x, mix_a, mix_b, mod_l, w_out)
        x2 = _ffn_call(x1, mod_l, row(norm_ffn_g[l]), ffn_w_up[l].astype(BF16), ffn_conv_w[l], row(ffn_conv_b[l]),
                       ffn_w_down[l].astype(BF16), row(final_norm_g), final=(l == DEPTH - 1))
        x = _whole(x2)

    y_prompt = x2[:N_CTX].reshape(BATCH, SEQ, d)
    y_sample = x2[N_CTX:].reshape(DEC_BATCH, DEC_SEQ, d)
    return (y_prompt, y_sample) + even_ctx + odd_ctx
```

```python
import functools
import math

import jax
import jax.numpy as jnp
import numpy as np
from jax import lax
from jax.experimental import pallas as pl
from jax.experimental.pallas import tpu as pltpu

D_MODEL = 1024
BATCH = 16
SEQ = 256
DEPTH = 2
DEC_BATCH = 4
DEC_SEQ = 1024
PAST_LEN = 512
GRID_W = 64
ROPE_THETA = 10000.0
EPS = 1e-6
RWKV_GN_EPS = 64e-5
HA = 4
DH_A = 64
DV_A = 2 * DH_A
HB = 8
HKV_B = 2
DH_B = 64
HC = 4
DK_C = 128
DV_C = 128
HD = 8
DH_D = 64
W_LORA = 64
A_LORA = 64
G_LORA = 128
D_FF = 2816

D_A = HA * DV_A
D_B = HB * DH_B
D_C = HC * DV_C
D_D = HD * DH_D
EVEN_PROJ = 2304
HGRN_PROJ = 2560
RWKV_PROJ = 1856
RWKV_PAD = 1920

N_CTX = BATCH * SEQ
N_DEC = DEC_BATCH * DEC_SEQ
N_TOK = N_CTX + N_DEC
COND_ROWS = 8

LANES = 128
SUBLANES = 8
VMEM_LIMIT = 56 * 1024 * 1024

F32 = jnp.float32
BF16 = jnp.bfloat16
HI = lax.Precision.HIGHEST

FF_TILE = 256
HGRN_L = 128
HGRN_HP = 2
RWKV_TB = 128
RWKV_NS = 4
RWKV_GROUP = 1
RWKV_LAG = 3
assert RWKV_LAG < 2 * RWKV_NS // RWKV_GROUP


def _dot(a, b):
    return jnp.dot(a, b, preferred_element_type=F32)


def _dot_hi(a, b):
    return jnp.dot(a, b, preferred_element_type=F32, precision=HI)


def _dot_b16(a, b):
    return _dot(a.astype(BF16), b.astype(BF16))


def _dot_nt(a, b, precision=None):
    return lax.dot_general(a, b, (((1,), (1,)), ((), ())), preferred_element_type=F32, precision=precision)


def _dot_tn(a, b, precision=None):
    return lax.dot_general(a, b, (((0,), (0,)), ((), ())), preferred_element_type=F32, precision=precision)


def _sigmoid(x):
    return 1.0 / (1.0 + jnp.exp(-x))


def _silu(x):
    return x * _sigmoid(x)


def _norm_mod(x, g, sc, sh):
    ms = jnp.mean(x * x, axis=-1, keepdims=True)
    return (x * lax.rsqrt(ms + EPS) * g) * (1.0 + sc) + sh


def _params(sem):
    return pltpu.CompilerParams(dimension_semantics=sem, vmem_limit_bytes=VMEM_LIMIT)


def _cond_row(i, tm):
    r0 = i * tm
    return jnp.where(r0 < N_CTX, 0, 1 + (r0 - N_CTX) // DEC_SEQ)


def _seq_len(i, tm):
    return jnp.where(i * tm < N_CTX, SEQ, DEC_SEQ)


ROW_PAD = SUBLANES


def _zero_row_pads(u_ref, tm):
    for k in range(u_ref.shape[0]):
        u_ref[k, 0:ROW_PAD, :] = jnp.zeros((ROW_PAD, u_ref.shape[2]), u_ref.dtype)
        u_ref[k, ROW_PAD + tm:2 * ROW_PAD + tm, :] = jnp.zeros((ROW_PAD, u_ref.shape[2]), u_ref.dtype)


def _neighbour_rows(u_ref, k, c, ch, seq_len):
    r0 = ROW_PAD + c * ch
    row8 = lax.broadcasted_iota(jnp.int32, (SUBLANES, 1), 0)
    at_start = (c * ch) % seq_len == 0
    at_end = ((c + 1) * ch) % seq_len == 0
    prev = u_ref[k, r0 - 1:r0 - 1 + ch, :]
    nxt = u_ref[k, r0 + 1:r0 + 1 + ch, :]
    prev = jnp.concatenate([jnp.where((row8 == 0) & at_start, 0.0, prev[0:SUBLANES]), prev[SUBLANES:]], axis=0)
    nxt = jnp.concatenate(
        [nxt[:ch - SUBLANES], jnp.where((row8 == SUBLANES - 1) & at_end, 0.0, nxt[ch - SUBLANES:])], axis=0)
    return prev, nxt


def _pick(tm, lo_ref, hi_ref, rows=slice(None)):
    return jnp.where(pl.program_id(0) * tm < N_CTX, lo_ref[rows, :], hi_ref[rows, :])


def _pair_specs(tm, width, pair):
    nct = N_CTX // tm
    hi0 = pair[2] // tm
    return [pl.BlockSpec((tm, width), lambda i: (jnp.minimum(i, nct - 1), 0)),
            pl.BlockSpec((tm, width), lambda i: (hi0 + jnp.maximum(i - nct, 0), 0))]


def _whole(t):
    return (t, t, N_CTX)


def _mod_kernel(c_ref, w_ref, b_ref, o_ref):
    s = _silu(c_ref[...]).astype(BF16)
    o_ref[...] = _dot(s, w_ref[...].astype(BF16)) + b_ref[...]


def _mod_call(cond, ada_w, ada_b):
    tn = 1536
    n = 6 * D_MODEL
    return pl.pallas_call(
        _mod_kernel,
        grid=(DEPTH, n // tn),
        in_specs=[
            pl.BlockSpec((COND_ROWS, D_MODEL), lambda l, j: (0, 0)),
            pl.BlockSpec((None, D_MODEL, tn), lambda l, j: (l, 0, j)),
            pl.BlockSpec((None, 1, tn), lambda l, j: (l, 0, j)),
        ],
        out_specs=pl.BlockSpec((None, COND_ROWS, tn), lambda l, j: (l, 0, j)),
        out_shape=jax.ShapeDtypeStruct((DEPTH, COND_ROWS, n), F32),
        compiler_params=_params(("arbitrary", "arbitrary")),
        name="ada_mod",
    )(cond, ada_w, ada_b.reshape(DEPTH, 1, n))


def _even_proj_kernel(xl_ref, xh_ref, mod_ref, g_ref, w_ref, qg_ref, kg_ref, cos_ref, s1_ref, s2_ref, gm_ref,
                      qa_ref, ka_ref, va_ref, qb_ref, kb_ref, vb_ref, cak_ref, cav_ref, cbk_ref, cbv_ref, *, tm):
    d = D_MODEL
    gm = gm_ref[...]
    scale = DH_A ** -0.5
    is_ctx = pl.program_id(0) * tm < N_CTX

    def head_norm(t, g):
        return t * lax.rsqrt(_dot_b16(t * t, gm) + EPS) * g

    for c in range(tm // SEQ):
        rows = slice(c * SEQ, (c + 1) * SEQ)
        h = _norm_mod(_pick(tm, xl_ref, xh_ref, rows), g_ref[...], mod_ref[:, d:2 * d], mod_ref[:, 0:d]).astype(BF16)
        proj = _dot(h, w_ref[...])
        cos, s1, s2 = cos_ref[rows, :], s1_ref[rows, :], s2_ref[rows, :]

        def rope(t):
            return t * cos + pltpu.roll(t, LANES - 16, 1) * s1 + pltpu.roll(t, 16, 1) * s2

        for j in range(4):
            sl = slice(j * LANES, (j + 1) * LANES)
            qa_ref[rows, sl] = rope(proj[:, j * LANES:(j + 1) * LANES]) * scale
            ka_ref[rows, sl] = rope(proj[:, 512 + j * LANES:512 + (j + 1) * LANES])
            va_ref[rows, sl] = proj[:, 1024 + j * LANES:1024 + (j + 1) * LANES]
            qb_ref[rows, sl] = rope(head_norm(proj[:, 1536 + j * LANES:1536 + (j + 1) * LANES], qg_ref[...])) * scale
        kb_ref[rows, :] = rope(head_norm(proj[:, 2048:2176], kg_ref[...]))
        vb_ref[rows, :] = proj[:, 2176:2304]

        @pl.when(is_ctx)
        def _(c=c, rows=rows):
            for hh in range(HA):
                sl = slice(hh * DV_A, (hh + 1) * DV_A)
                cak_ref[c, hh] = ka_ref[rows, sl]
                cav_ref[c, hh] = va_ref[rows, sl]
            for hh in range(HKV_B):
                sl = slice(hh * DH_B, (hh + 1) * DH_B)
                cbk_ref[c, hh] = kb_ref[rows, sl]
                cbv_ref[c, hh] = vb_ref[rows, sl]


def _even_proj_call(x, mod_l, g, w, qg, kg, cos, s1, s2, gm):
    tm = 512
    nt = N_TOK // tm
    n_rope_blk = DEC_SEQ // tm

    def rope_idx(i):
        return (jnp.where(i * tm < N_CTX, n_rope_blk, (i - N_CTX // tm) % n_rope_blk), 0)

    full = lambda shape: pl.BlockSpec(shape, lambda i: (0,) * len(shape))
    out512 = pl.BlockSpec((tm, 512), lambda i: (i, 0))
    out128 = pl.BlockSpec((tm, LANES), lambda i: (i, 0))
    last_ctx = N_CTX // tm - 1

    def cache(heads, width):
        return pl.BlockSpec((tm // SEQ, None, heads, SEQ, width), lambda i: (jnp.minimum(i, last_ctx), 0, 0, 0, 0))

    return pl.pallas_call(
        functools.partial(_even_proj_kernel, tm=tm),
        grid=(nt,),
        in_specs=_pair_specs(tm, D_MODEL, x) + [
            pl.BlockSpec((None, 1, 6 * D_MODEL), lambda i: (_cond_row(i, tm), 0, 0)),
            full((1, D_MODEL)),
            full((D_MODEL, EVEN_PROJ)),
            full((1, LANES)),
            full((1, LANES)),
            pl.BlockSpec((tm, LANES), rope_idx),
            pl.BlockSpec((tm, LANES), rope_idx),
            pl.BlockSpec((tm, LANES), rope_idx),
            full((LANES, LANES)),
        ],
        out_specs=[out512, out512, out512, out512, out128, out128, cache(HA, DV_A), cache(HA, DV_A),
                   cache(HKV_B, DH_B), cache(HKV_B, DH_B)],
        out_shape=[jax.ShapeDtypeStruct((N_TOK, 512), F32)] * 4 + [jax.ShapeDtypeStruct((N_TOK, LANES), F32)] * 2
        + [jax.ShapeDtypeStruct((BATCH, 1, HA, SEQ, DV_A), F32)] * 2
        + [jax.ShapeDtypeStruct((BATCH, 1, HKV_B, SEQ, DH_B), F32)] * 2,
        compiler_params=_params(("arbitrary",)),
        name="even_proj",
    )(x[0], x[1], mod_l, g, w, qg, kg, cos, s1, s2, gm)


def _softmax_pv(q, ks, vs):
    ss = [_dot_nt(q, k) for k in ks]
    m = functools.reduce(jnp.maximum, [jnp.max(s, axis=-1, keepdims=True) for s in ss])
    ps = [jnp.exp(s - m) for s in ss]
    l = functools.reduce(jnp.add, [jnp.sum(p, axis=-1, keepdims=True) for p in ps])
    acc = functools.reduce(jnp.add, [_dot(p.astype(BF16), v) for p, v in zip(ps, vs)])
    return acc / l


def _attn_kernel(*refs, has_cache, lam_init):
    if has_cache:
        (qa_ref, qb_ref, ka_ref, va_ref, kb_ref, vb_ref, cak_ref, cav_ref, cbk_ref, cbv_ref,
         al_ref, sg_ref, oa_ref, ob_ref) = refs
    else:
        qa_ref, qb_ref, ka_ref, va_ref, kb_ref, vb_ref, al_ref, sg_ref, oa_ref, ob_ref = refs
    al = al_ref[...]
    lam = (jnp.exp(jnp.sum(al[0:1] * al[1:2], axis=-1, keepdims=True))
           - jnp.exp(jnp.sum(al[2:3] * al[3:4], axis=-1, keepdims=True)) + lam_init)
    lo = lax.broadcasted_iota(jnp.int32, (1, LANES), 1) < DH_A

    for h in range(HA):
        sl = slice(h * LANES, (h + 1) * LANES)
        q = qa_ref[:, sl]
        ks = [ka_ref[:, sl].astype(BF16)]
        vs = [va_ref[:, sl].astype(BF16)]
        if has_cache:
            ks.insert(0, cak_ref[h].astype(BF16))
            vs.insert(0, cav_ref[h].astype(BF16))
        a1 = _softmax_pv(jnp.where(lo, q, 0.0).astype(BF16), ks, vs)
        a2 = _softmax_pv(jnp.where(lo, 0.0, q).astype(BF16), ks, vs)
        dlt = a1 - lam * a2
        ms = jnp.mean(dlt * dlt, axis=-1, keepdims=True)
        oa_ref[:, sl] = dlt * lax.rsqrt(ms + EPS) * sg_ref[...] * (1.0 - lam_init)

    ks = [kb_ref[...].astype(BF16)]
    vs = [vb_ref[...].astype(BF16)]
    if has_cache:
        ks.insert(0, cbk_ref[...].astype(BF16))
        vs.insert(0, cbv_ref[...].astype(BF16))
    for j in range(HB // 2):
        sl = slice(j * LANES, (j + 1) * LANES)
        q = qb_ref[:, sl]
        o0 = _softmax_pv(jnp.where(lo, q, 0.0).astype(BF16), ks, vs)
        o1 = _softmax_pv(jnp.where(lo, 0.0, q).astype(BF16), ks, vs)
        ob_ref[:, sl] = jnp.where(lo, o0, o1)


def _attn_call(qa, qb, ka, va, kb, vb, cache, a_lambda, subln_g, lam_init, *, n_seq, seq, row0):
    tq = 256
    nq = seq // tq
    qblk0 = row0 // tq
    sblk0 = row0 // seq
    has_cache = cache is not None
    qspec = pl.BlockSpec((tq, 512), lambda b, i: (qblk0 + b * nq + i, 0))
    own512 = pl.BlockSpec((seq, 512), lambda b, i: (sblk0 + b, 0))
    own128 = pl.BlockSpec((seq, LANES), lambda b, i: (sblk0 + b, 0))
    in_specs = [qspec, qspec, own512, own512, own128, own128]
    args = [qa, qb, ka, va, kb, vb]
    if has_cache:
        in_specs += [
            pl.BlockSpec((None, HA, PAST_LEN, LANES), lambda b, i: (b, 0, 0, 0)),
            pl.BlockSpec((None, HA, PAST_LEN, LANES), lambda b, i: (b, 0, 0, 0)),
            pl.BlockSpec((None, PAST_LEN, LANES), lambda b, i: (b, 0, 0)),
            pl.BlockSpec((None, PAST_LEN, LANES), lambda b, i: (b, 0, 0)),
        ]
        args += list(cache)
    in_specs += [pl.BlockSpec((4, DH_A), lambda b, i: (0, 0)), pl.BlockSpec((1, LANES), lambda b, i: (0, 0))]
    args += [a_lambda, subln_g]
    ospec = pl.BlockSpec((tq, 512), lambda b, i: (b * nq + i, 0))
    return pl.pallas_call(
        functools.partial(_attn_kernel, has_cache=has_cache, lam_init=lam_init),
        grid=(n_seq, nq),
        in_specs=in_specs,
        out_specs=[ospec, ospec],
        out_shape=[jax.ShapeDtypeStruct((n_seq * seq, 512), F32)] * 2,
        compiler_params=_params(("arbitrary", "arbitrary")),
        name="attn_dec" if has_cache else "attn_ctx",
    )(*args)


def _out_proj_kernel(xl_ref, xh_ref, al_ref, ah_ref, bl_ref, bh_ref, mod_ref, w_ref, o_ref, *, tm):
    d = D_MODEL
    half = al_ref.shape[1]
    a = _pick(tm, al_ref, ah_ref).astype(BF16)
    b = _pick(tm, bl_ref, bh_ref).astype(BF16)
    mix = _dot(a, w_ref[0:half, :]) + _dot(b, w_ref[half:2 * half, :])
    o_ref[...] = _pick(tm, xl_ref, xh_ref) + mod_ref[:, 2 * d:3 * d] * mix


def _out_proj_call(x, a, b, mod_l, w):
    tm = 512
    return pl.pallas_call(
        functools.partial(_out_proj_kernel, tm=tm),
        grid=(N_TOK // tm,),
        in_specs=_pair_specs(tm, D_MODEL, x) + _pair_specs(tm, 512, a) + _pair_specs(tm, 512, b) + [
            pl.BlockSpec((None, 1, 6 * D_MODEL), lambda i: (_cond_row(i, tm), 0, 0)),
            pl.BlockSpec((D_MODEL, D_MODEL), lambda i: (0, 0)),
        ],
        out_specs=pl.BlockSpec((tm, D_MODEL), lambda i: (i, 0)),
        out_shape=jax.ShapeDtypeStruct((N_TOK, D_MODEL), F32),
        compiler_params=_params(("arbitrary",)),
        name="out_proj",
    )(x[0], x[1], a[0], a[1], b[0], b[1], mod_l, w)


def _ffn_kernel(x_ref, mod_ref, g_ref, wv_ref, wg_ref, cwv_ref, cwg_ref, cbv_ref, cbg_ref, wd_ref, fg_ref,
                *rest, tm, final):
    if final:
        oc_ref, od_ref, h_scr, acc_scr, u_scr = rest
    else:
        o_ref, h_scr, acc_scr, u_scr = rest
    d = D_MODEL
    i, j = pl.program_id(0), pl.program_id(1)
    ch = SEQ

    @pl.when(j == 0)
    def _():
        h_scr[...] = _norm_mod(x_ref[...], g_ref[...], mod_ref[:, 4 * d:5 * d], mod_ref[:, 3 * d:4 * d]).astype(BF16)
        acc_scr[...] = jnp.zeros_like(acc_scr)
        _zero_row_pads(u_scr, tm)

    seq_len = _seq_len(i, tm)

    def project(c):
        rows = slice(c * ch, (c + 1) * ch)
        u_scr[0, ROW_PAD + c * ch:ROW_PAD + (c + 1) * ch, :] = _dot(h_scr[rows, :], wv_ref[...])
        u_scr[1, ROW_PAD + c * ch:ROW_PAD + (c + 1) * ch, :] = _dot(h_scr[rows, :], wg_ref[...])

    def conv(k, c, cw_ref, cb_ref):
        prev, nxt = _neighbour_rows(u_scr, k, c, ch, seq_len)
        cur = u_scr[k, ROW_PAD + c * ch:ROW_PAD + (c + 1) * ch, :]
        return cw_ref[0:1, :] * prev + cw_ref[1:2, :] * cur + cw_ref[2:3, :] * nxt + cb_ref[...]

    def activate(c):
        val = conv(0, c, cwv_ref, cbv_ref)
        gate = conv(1, c, cwg_ref, cbg_ref)
        act = (_silu(gate) * val).astype(BF16)
        acc_scr[c * ch:(c + 1) * ch, :] += _dot(act, wd_ref[...])

    n_chunks = tm // ch
    project(0)
    for c in range(n_chunks):
        if c + 1 < n_chunks:
            project(c + 1)
        activate(c)

    @pl.when(j == pl.num_programs(1) - 1)
    def _():
        y = x_ref[...] + mod_ref[:, 5 * d:6 * d] * acc_scr[...]
        if not final:
            o_ref[...] = y
            return
        ms = jnp.mean(y * y, axis=-1, keepdims=True)
        y = y * lax.rsqrt(ms + EPS) * fg_ref[...]

        @pl.when(i * tm < N_CTX)
        def _():
            oc_ref[...] = y

        @pl.when(i * tm >= N_CTX)
        def _():
            od_ref[...] = y


def _ffn_call(x, mod_l, g, w_up, conv_w, conv_b, w_down, final_g, *, final):
    tm = DEC_SEQ
    nf = D_FF // FF_TILE
    up = lambda j: j
    down = lambda j: j
    nct = N_CTX // tm
    if final:
        out_specs = [pl.BlockSpec((tm, D_MODEL), lambda i, j: (jnp.minimum(i, nct - 1), 0)),
                     pl.BlockSpec((tm, D_MODEL), lambda i, j: (jnp.maximum(i - nct, 0), 0))]
        out_shape = [jax.ShapeDtypeStruct((N_CTX, D_MODEL), F32), jax.ShapeDtypeStruct((N_DEC, D_MODEL), F32)]
    else:
        out_specs = pl.BlockSpec((tm, D_MODEL), lambda i, j: (i, 0))
        out_shape = jax.ShapeDtypeStruct((N_TOK, D_MODEL), F32)
    return pl.pallas_call(
        functools.partial(_ffn_kernel, tm=tm, final=final),
        grid=(N_TOK // tm, nf),
        in_specs=[
            pl.BlockSpec((tm, D_MODEL), lambda i, j: (i, 0)),
            pl.BlockSpec((None, 1, 6 * D_MODEL), lambda i, j: (_cond_row(i, tm), 0, 0)),
            pl.BlockSpec((1, D_MODEL), lambda i, j: (0, 0)),
            pl.BlockSpec((D_MODEL, FF_TILE), lambda i, j: (0, up(j))),
            pl.BlockSpec((D_MODEL, FF_TILE), lambda i, j: (0, nf + up(j))),
            pl.BlockSpec((3, FF_TILE), lambda i, j: (0, up(j))),
            pl.BlockSpec((3, FF_TILE), lambda i, j: (0, nf + up(j))),
            pl.BlockSpec((1, FF_TILE), lambda i, j: (0, up(j))),
            pl.BlockSpec((1, FF_TILE), lambda i, j: (0, nf + up(j))),
            pl.BlockSpec((FF_TILE, D_MODEL), lambda i, j: (down(j), 0)),
            pl.BlockSpec((1, D_MODEL), lambda i, j: (0, 0)),
        ],
        out_specs=out_specs,
        out_shape=out_shape,
        scratch_shapes=[pltpu.VMEM((tm, D_MODEL), BF16), pltpu.VMEM((tm, D_MODEL), F32),
                        pltpu.VMEM((2, tm + 2 * SUBLANES, FF_TILE), F32)],
        compiler_params=_params(("arbitrary", "arbitrary")),
        name="conv_ffn",
    )(x, mod_l, g, w_up, w_up, conv_w, conv_w, conv_b, conv_b, w_down, final_g)


def _odd_proj_kernel(x_ref, mod_ref, g_ref, w_ref, mu_ref, o_ref, h_scr, *u_scr, tm, shift):
    d = D_MODEL
    i, j = pl.program_id(0), pl.program_id(1)
    ch = SEQ
    n_chunks = tm // ch

    @pl.when(j == 0)
    def _():
        h_scr[...] = _norm_mod(x_ref[...], g_ref[...], mod_ref[:, d:2 * d], mod_ref[:, 0:d]).astype(BF16)
        if shift:
            _zero_row_pads(u_scr[0], tm)

    if not shift:
        for c in range(n_chunks):
            o_ref[c * ch:(c + 1) * ch, :] = _dot(h_scr[c * ch:(c + 1) * ch, :], w_ref[...])
        return

    u = u_scr[0]
    seq_len = _seq_len(i, tm)

    def project(c):
        u[0, ROW_PAD + c * ch:ROW_PAD + (c + 1) * ch, :] = _dot(h_scr[c * ch:(c + 1) * ch, :], w_ref[...])

    project(0)
    for c in range(n_chunks):
        if c + 1 < n_chunks:
            project(c + 1)
        prev, nxt = _neighbour_rows(u, 0, c, ch, seq_len)
        p = u[0, ROW_PAD + c * ch:ROW_PAD + (c + 1) * ch, :]
        o_ref[c * ch:(c + 1) * ch, :] = p + mu_ref[...] * (0.5 * (prev + nxt) - p)


def _odd_proj_call(x, mod_l, g, w, mu, *, shift):
    tm = DEC_SEQ
    n = w.shape[1]
    tn = n if shift else n // 2
    return pl.pallas_call(
        functools.partial(_odd_proj_kernel, tm=tm, shift=shift),
        grid=(N_TOK // tm, n // tn),
        in_specs=[
            pl.BlockSpec((tm, D_MODEL), lambda i, j: (i, 0)),
            pl.BlockSpec((None, 1, 6 * D_MODEL), lambda i, j: (_cond_row(i, tm), 0, 0)),
            pl.BlockSpec((1, D_MODEL), lambda i, j: (0, 0)),
            pl.BlockSpec((D_MODEL, tn), lambda i, j: (0, j)),
            pl.BlockSpec((1, tn), lambda i, j: (0, j)),
        ],
        out_specs=pl.BlockSpec((tm, tn), lambda i, j: (i, j)),
        out_shape=jax.ShapeDtypeStruct((N_TOK, n), F32),
        scratch_shapes=[pltpu.VMEM((tm, D_MODEL), BF16)]
        + ([pltpu.VMEM((1, tm + 2 * ROW_PAD, tn), F32)] if shift else []),
        compiler_params=_params(("arbitrary", "arbitrary")),
        name="odd_proj_rwkv" if shift else "odd_proj_hgrn",
    )(x, mod_l, g, w, mu)


def _hgrn_kernel(q_ref, ff_ref, fb_ref, v_ref, gc_ref, lbl_ref, ng_ref, s0_ref, ones_ref, o_ref, sfin_ref,
                 oacc_f, oacc_b, st_scr, qd_scr, add_scr, dec_scr, start_scr, *, seq, lidx):
    L = HGRN_L
    nc = seq // L
    nb = L // SUBLANES
    ones = ones_ref[...]
    rowi = lax.broadcasted_iota(jnp.int32, (L, L), 0)
    coli = lax.broadcasted_iota(jnp.int32, (L, L), 1)
    rowv = lax.broadcasted_iota(jnp.int32, (L, 1), 0)

    def lower_bound(drc):
        lg = lbl_ref[drc]
        e = jnp.exp(lg - jnp.max(lg, axis=0, keepdims=True))
        sm = e / jnp.sum(e, axis=0, keepdims=True)
        return functools.reduce(jnp.add, [sm[i:i + 1, :] for i in range(1, lidx + 1)])

    def spread(t, s_l):
        return jnp.concatenate(
            [jnp.broadcast_to(t[b * SUBLANES + s_l:b * SUBLANES + s_l + 1, :], (SUBLANES, LANES)) for b in range(nb)],
            axis=0)

    def chunk(rev, r0, f_ref, lb, tri, hs):
        q = _silu(q_ref[pl.ds(r0, L), hs])
        f = lb + (1.0 - lb) * _sigmoid(f_ref[pl.ds(r0, L), hs])
        k = 1.0 - f
        lf = jnp.log(f)
        v = v_ref[pl.ds(r0, L), hs].astype(BF16)
        cum = _dot_hi(tri, lf)
        cum_ex = cum - lf
        q_dec = (q * jnp.exp(cum)).astype(BF16)

        scores = jnp.zeros((L, L), F32)
        sub = rowv % SUBLANES
        h = L // 2
        while h >= 1:
            if h >= SUBLANES:
                pieces = []
                for a in range(0, L, 2 * h):
                    edge = a + h - 1 if rev else a + h
                    pieces.append(jnp.broadcast_to(cum_ex[edge:edge + 1, :], (2 * h, LANES)))
                anchor = jnp.concatenate(pieces, axis=0) if len(pieces) > 1 else pieces[0]
            else:
                anchor = None
                for a in range(0, SUBLANES, 2 * h):
                    edge = a + h - 1 if rev else a + h
                    cand = spread(cum_ex, edge)
                    anchor = cand if anchor is None else jnp.where(sub >= a, cand, anchor)
            is_q = ((rowv // h) % 2) == (0 if rev else 1)
            d = cum - anchor
            e = jnp.exp(jnp.where(is_q, d, -d))
            qe = jnp.where(is_q, q * e, 0.0).astype(BF16)
            ke = jnp.where(is_q, 0.0, k * e).astype(BF16)
            same_pair = (rowi // (2 * h)) == (coli // (2 * h))
            scores = scores + jnp.where(same_pair, _dot_nt(qe, ke), 0.0)
            h //= 2

        own = _dot((q * k).astype(BF16), ones)
        own = own[:, 0:L] if L <= LANES else jnp.concatenate([own] * (L // LANES), axis=1)
        scores = jnp.where(rowi == coli, own, scores)
        o = _dot(scores.astype(BF16), v)

        end = 0 if rev else L - 1
        cend = cum[end:end + 1, :]
        kd = (k * jnp.exp(cend - cum)).astype(BF16)
        return q_dec, o, _dot_tn(v, kd), jnp.exp(cend)

    lb_f, lb_b = lower_bound(0), lower_bound(1)
    tri_f = jnp.where(coli <= rowi, 1.0, 0.0).astype(F32)
    tri_b = jnp.where(coli >= rowi, 1.0, 0.0).astype(F32)
    heads = [slice(hh * LANES, (hh + 1) * LANES) for hh in range(HGRN_HP)]
    chains = [(hh, hs, drc) for hh, hs in enumerate(heads) for drc in range(2)]
    oaccs = (oacc_f, oacc_b)

    def local(ci, carry):
        r0 = pl.multiple_of(ci * L, L)
        for c, (hh, hs, drc) in enumerate(chains):
            q_dec, o, add, dec = chunk(drc == 1, r0, fb_ref if drc else ff_ref, (lb_b if drc else lb_f)[:, hs],
                                       tri_b if drc else tri_f, hs)
            qd_scr[c, pl.ds(r0, L), :] = q_dec
            oaccs[drc][pl.ds(r0, L), hs] = o
            add_scr[c * nc + ci] = add
            dec_scr[c * nc + ci] = jnp.broadcast_to(dec, (SUBLANES, LANES))
        return carry

    lax.fori_loop(0, nc, local, 0, unroll=min(2, nc))

    for c, (hh, hs, drc) in enumerate(chains):
        st_scr[c] = s0_ref[drc, hh].T

    def scan(n, carry):
        for c, (hh, hs, drc) in enumerate(chains):
            slot = c * nc + ((nc - 1 - n) if drc else n)
            st = st_scr[c]
            start_scr[slot] = st.astype(BF16)
            st_scr[c] = st * dec_scr[slot][0:1, :] + add_scr[slot]
        return carry

    lax.fori_loop(0, nc, scan, 0)

    def carried(ci, carry):
        r0 = pl.multiple_of(ci * L, L)
        for c, (hh, hs, drc) in enumerate(chains):
            oaccs[drc][pl.ds(r0, L), hs] += _dot_nt(qd_scr[c, pl.ds(r0, L), :], start_scr[c * nc + ci])
        return carry

    lax.fori_loop(0, nc, carried, 0)

    for hh, hs in enumerate(heads):
        sfin_ref[0, hh] = st_scr[2 * hh].T
        sfin_ref[1, hh] = st_scr[2 * hh + 1].T
        o = oacc_f[:, hs] + oacc_b[:, hs]
        ms = jnp.mean(o * o, axis=-1, keepdims=True)
        o_ref[:, hs] = o * lax.rsqrt(ms + EPS) * ng_ref[...] * _silu(gc_ref[:, hs])


def _hgrn_call(proj_h, lb_logits, norm_g, s0, ones, *, n_seq, seq, row0, lidx):
    sblk0 = row0 // seq

    hp = HGRN_HP
    width = hp * LANES

    def col(section):
        return pl.BlockSpec((seq, width), lambda b, h: (sblk0 + b, section * (HC // hp) + h))

    st_spec = pl.BlockSpec((None, 2, hp, DK_C, DV_C), lambda b, h: (b, 0, h, 0, 0))
    return pl.pallas_call(
        functools.partial(_hgrn_kernel, seq=seq, lidx=lidx),
        grid=(n_seq, HC // hp),
        in_specs=[
            col(0), col(1), col(2), col(3), col(4),
            pl.BlockSpec((2, DEPTH, width), lambda b, h: (0, 0, h)),
            pl.BlockSpec((1, LANES), lambda b, h: (0, 0)),
            st_spec,
            pl.BlockSpec((LANES, LANES), lambda b, h: (0, 0)),
        ],
        out_specs=[pl.BlockSpec((seq, width), lambda b, h: (b, h)), st_spec],
        out_shape=[jax.ShapeDtypeStruct((n_seq * seq, D_C), F32),
                   jax.ShapeDtypeStruct((n_seq, 2, HC, DK_C, DV_C), F32)],
        scratch_shapes=[pltpu.VMEM((seq, width), F32), pltpu.VMEM((seq, width), F32),
                        pltpu.VMEM((2 * hp, DV_C, DK_C), F32),
                        pltpu.VMEM((2 * hp, seq, DK_C), BF16),
                        pltpu.VMEM((2 * hp * (seq // HGRN_L), DV_C, DK_C), F32),
                        pltpu.VMEM((2 * hp * (seq // HGRN_L), SUBLANES, DK_C), F32),
                        pltpu.VMEM((2 * hp * (seq // HGRN_L), DV_C, DK_C), BF16)],
        compiler_params=_params(("arbitrary", "arbitrary")),
        name="hgrn_dec" if row0 else "hgrn_ctx",
    )(proj_h, proj_h, proj_h, proj_h, proj_h, lb_logits, norm_g, s0, ones)


def _rwkv_prep_kernel(p_ref, a0_ref, aup_ref, gup_ref, kk_ref, ka_ref, rk_ref, w0_ref, wup0_ref, wup1_ref,
                      gs_ref, w0o_ref, w1o_ref, nkk_ref, bb_ref, kt_ref, g_ref, rkv_ref):
    r = p_ref[:, 0:512]
    k = p_ref[:, 512:1024]
    v = p_ref[:, 1024:1536]
    wd = p_ref[:, 1536:1664]
    gd = p_ref[:, 1664:1792]
    ad = p_ref[:, 1792:1920]
    gs = gs_ref[...]

    def group_sum(t):
        return jnp.concatenate([_dot_b16(t[:, j * LANES:(j + 1) * LANES], gs) for j in range(4)], axis=1)

    a = _sigmoid(a0_ref[...] + _dot_b16(ad, aup_ref[...]))
    g_ref[...] = _dot_b16(_sigmoid(gd), gup_ref[...])
    kkr = k * kk_ref[...]
    kk = kkr / jnp.maximum(jnp.sqrt(group_sum(kkr * kkr)), 1e-12)
    kt = k * (1.0 + (a - 1.0) * ka_ref[...])
    th = jnp.tanh(wd)
    decay = math.exp(-0.5)
    w0o_ref[...] = jnp.exp(-decay * _sigmoid(w0_ref[0:1, :] + _dot_b16(th, wup0_ref[...])))
    w1o_ref[...] = jnp.exp(-decay * _sigmoid(w0_ref[1:2, :] + _dot_b16(th, wup1_ref[...])))
    nkk_ref[...] = -kk
    bb_ref[...] = kk * a
    kt_ref[...] = kt
    rkv_ref[...] = group_sum(r * kt * rk_ref[...]) * v


def _rwkv_prep_call(proj_r, a0, aup, gup, kk_k, k_a, r_k, w0, wup0, wup1, gs):
    tm = 512
    full = lambda shape: pl.BlockSpec(shape, lambda i: (0,) * len(shape))
    ospec = pl.BlockSpec((tm, D_D), lambda i: (i, 0))
    return pl.pallas_call(
        _rwkv_prep_kernel,
        grid=(N_TOK // tm,),
        in_specs=[
            pl.BlockSpec((tm, RWKV_PAD), lambda i: (i, 0)),
            full((1, D_D)), full((LANES, D_D)), full((LANES, D_D)), full((1, D_D)), full((1, D_D)), full((1, D_D)),
            full((2, D_D)), full((LANES, D_D)), full((LANES, D_D)), full((LANES, LANES)),
        ],
        out_specs=[ospec] * 7,
        out_shape=[jax.ShapeDtypeStruct((N_TOK, D_D), F32)] * 7,
        compiler_params=_params(("arbitrary",)),
        name="rwkv_prep",
    )(proj_r, a0, aup, gup, kk_k, k_a, r_k, w0, wup0, wup1, gs)


def _rwkv_scan_kernel(rf_ref, wf_ref, nf_ref, bf_ref, kf_ref, vf_ref, rb_ref, wb_ref, nb_ref, bb_ref, kb_ref,
                      vb_ref, s0_ref, q1_ref, wsp_ref, of_ref, ob_ref, sfin_ref, *scratch):
    tb = pl.program_id(1)
    ns = RWKV_NS
    ngrp = D_D // LANES
    steps = RWKV_TB
    members = [(s, drc) for s in range(ns) for drc in range(2)]
    groups = [members[i:i + RWKV_GROUP] for i in range(0, len(members), RWKV_GROUP)]
    st, ot, pabuf, obuf = (scratch[i * len(groups):(i + 1) * len(groups)] for i in range(4))
    m_rows = RWKV_GROUP * ngrp * DH_D
    dirs = ((rf_ref, wf_ref, nf_ref, bf_ref, kf_ref, vf_ref), (rb_ref, wb_ref, nb_ref, bb_ref, kb_ref, vb_ref))
    o_refs = (of_ref, ob_ref)

    def rows(c, n=DH_D, base=0):
        return slice(base + c * n, base + (c + 1) * n)

    def chains(q):
        return [(m * ngrp + g, m, s, drc, g) for m, (s, drc) in enumerate(groups[q]) for g in range(ngrp)]

    @pl.when(tb == 0)
    def _():
        for q in range(len(groups)):
            for c, _, s, drc, g in chains(q):
                st[q][rows(c), :] = s0_ref[s, drc, g]

    for q in range(len(groups)):
        ot[q][...] = jnp.zeros_like(ot[q])
    lane = lax.broadcasted_iota(jnp.int32, (1, LANES), 1)
    diag = (lax.broadcasted_iota(jnp.int32, (DH_D, LANES), 1) & (DH_D - 1)) == lax.broadcasted_iota(
        jnp.int32, (DH_D, LANES), 0)

    def step_group(t8, carry):
        bases = (pl.multiple_of(t8 * SUBLANES, SUBLANES), pl.multiple_of(steps - (t8 + 1) * SUBLANES, SUBLANES))
        blk = [[[ref[s, pl.ds(bases[drc], SUBLANES), :] for ref in dirs[drc]] for s, drc in grp] for grp in groups]

        def issue(j, q):
            for c, m, s, drc, g in chains(q):
                loc = (SUBLANES - 1 - j) if drc else j
                sl = slice(g * LANES, (g + 1) * LANES)
                n_b, v_b = blk[q][m][2], blk[q][m][5]
                pabuf[q][rows(c), :] = (st[q][rows(c), :] * n_b[loc:loc + 1, sl]).astype(BF16)
                pabuf[q][rows(c, base=m_rows), :] = jnp.where(diag, v_b[loc:loc + 1, sl], 0.0).astype(BF16)
            return _dot(pabuf[q][...], q1_ref[...])

        def retire(j, q, sums):
            for c, m, s, drc, g in chains(q):
                loc = (SUBLANES - 1 - j) if drc else j
                sl = slice(g * LANES, (g + 1) * LANES)
                r_b, w_b, _, b_b, k_b, _ = blk[q][m]
                sv = (st[q][rows(c), :] * w_b[loc:loc + 1, sl] + sums[rows(c)] * b_b[loc:loc + 1, sl]
                      + sums[rows(c, base=m_rows)] * k_b[loc:loc + 1, sl])
                st[q][rows(c), :] = sv
                obuf[q][rows(c), loc * LANES:(loc + 1) * LANES] = (sv * r_b[loc:loc + 1, sl]).astype(BF16)

        order = [(j, q) for j in range(SUBLANES) for q in range(len(groups))]
        inflight = []
        for slot, (j, q) in enumerate(order):
            inflight.append((j, q, issue(j, q)))
            if len(inflight) > RWKV_LAG:
                retire(*inflight.pop(0))
        for item in inflight:
            retire(*item)
        for q in range(len(groups)):
            o_all = _dot(obuf[q][...], wsp_ref[...])
            for c, m, s, drc, g in chains(q):
                fresh = (lane >= bases[drc]) & (lane < bases[drc] + SUBLANES)
                top, bot = slice(c * LANES, c * LANES + DH_D), slice(c * LANES + DH_D, (c + 1) * LANES)
                ot[q][top, :] = jnp.where(fresh, o_all[rows(c), 0:LANES], ot[q][top, :])
                ot[q][bot, :] = jnp.where(fresh, o_all[rows(c), LANES:2 * LANES], ot[q][bot, :])
        return carry

    lax.fori_loop(0, steps // SUBLANES, step_group, 0)

    for q in range(len(groups)):
        for c, m, s, drc, g in chains(q):
            o_refs[drc][s, :, g * LANES:(g + 1) * LANES] = ot[q][rows(c, LANES), :].T

    @pl.when(tb == pl.num_programs(1) - 1)
    def _():
        for q in range(len(groups)):
            for c, m, s, drc, g in chains(q):
                sfin_ref[s, drc, g] = st[q][rows(c), :]


def _rwkv_scan_call(r_src, w0, w1, nkk, bb, kt, s0, q1, wsp, *, n_seq, seq, row0):
    ns = RWKV_NS
    ntb = seq // RWKV_TB
    ngrp = D_D // LANES
    grp0 = row0 // (seq * ns)

    def view(t):
        return t.reshape(N_TOK // (seq * ns), ns, ntb, RWKV_TB, t.shape[-1])

    def tok(rev, cb=0):
        if rev:
            return pl.BlockSpec((None, ns, None, RWKV_TB, D_D), lambda b, t: (grp0 + b, 0, ntb - 1 - t, 0, cb))
        return pl.BlockSpec((None, ns, None, RWKV_TB, D_D), lambda b, t: (grp0 + b, 0, t, 0, cb))

    def out(rev):
        if rev:
            return pl.BlockSpec((None, ns, None, RWKV_TB, D_D), lambda b, t: (b, 0, ntb - 1 - t, 0, 0))
        return pl.BlockSpec((None, ns, None, RWKV_TB, D_D), lambda b, t: (b, 0, t, 0, 0))

    st_spec = pl.BlockSpec((ns, 2, ngrp, DH_D, LANES), lambda b, t: (b, 0, 0, 0, 0))
    in_specs = []
    for rev in (False, True):
        in_specs += [tok(rev, 0), tok(rev), tok(rev), tok(rev), tok(rev), tok(rev, 2)]
    in_specs += [st_spec, pl.BlockSpec((LANES, LANES), lambda b, t: (0, 0)),
                 pl.BlockSpec((SUBLANES * LANES, 2 * LANES), lambda b, t: (0, 0))]
    rv, w0v, w1v, nv, bv, kv = [view(t) for t in (r_src, w0, w1, nkk, bb, kt)]
    o_shape = jax.ShapeDtypeStruct((n_seq // ns, ns, ntb, RWKV_TB, D_D), F32)
    m_rows = RWKV_GROUP * ngrp * DH_D
    n_groups = ns * 2 // RWKV_GROUP
    scratch = ([pltpu.VMEM((m_rows, LANES), F32)] * n_groups + [pltpu.VMEM((2 * m_rows, LANES), F32)] * n_groups
               + [pltpu.VMEM((2 * m_rows, LANES), BF16)] * n_groups
               + [pltpu.VMEM((m_rows, SUBLANES * LANES), BF16)] * n_groups)
    o_f, o_b, s_fin = pl.pallas_call(
        _rwkv_scan_kernel,
        grid=(n_seq // ns, ntb),
        in_specs=in_specs,
        out_specs=[out(False), out(True), st_spec],
        out_shape=[o_shape, o_shape, jax.ShapeDtypeStruct((n_seq, 2, ngrp, DH_D, LANES), F32)],
        scratch_shapes=scratch,
        compiler_params=_params(("arbitrary", "arbitrary")),
        name="rwkv_dec" if row0 else "rwkv_ctx",
    )(rv, w0v, nv, bv, kv, rv, rv, w1v, nv, bv, kv, rv, s0, q1, wsp)
    return o_f.reshape(n_seq * seq, D_D), o_b.reshape(n_seq * seq, D_D), s_fin


def _rwkv_post_kernel(ofl_ref, ofh_ref, obl_ref, obh_ref, rkv_ref, g_ref, lng_ref, lnb_ref, gm_ref, o_ref, *, tm):
    gm = gm_ref[...]
    o_all = _pick(tm, ofl_ref, ofh_ref) + _pick(tm, obl_ref, obh_ref)
    for j in range(D_D // LANES):
        sl = slice(j * LANES, (j + 1) * LANES)
        o = o_all[:, sl]
        dlt = o - _dot_b16(o, gm)
        var = _dot_b16(dlt * dlt, gm)
        y = dlt * lax.rsqrt(var + RWKV_GN_EPS) * lng_ref[:, sl] + lnb_ref[:, sl]
        o_ref[:, sl] = (y + rkv_ref[:, sl]) * g_ref[:, sl]


def _rwkv_post_call(o_f, o_b, rkv, g, ln_g, ln_b, gm):
    tm = 512
    tok = pl.BlockSpec((tm, D_D), lambda i: (i, 0))
    row = pl.BlockSpec((1, D_D), lambda i: (0, 0))
    return pl.pallas_call(
        functools.partial(_rwkv_post_kernel, tm=tm),
        grid=(N_TOK // tm,),
        in_specs=_pair_specs(tm, D_D, o_f) + _pair_specs(tm, D_D, o_b)
        + [tok, tok, row, row, pl.BlockSpec((LANES, LANES), lambda i: (0, 0))],
        out_specs=tok,
        out_shape=jax.ShapeDtypeStruct((N_TOK, D_D), F32),
        compiler_params=_params(("arbitrary",)),
        name="rwkv_post",
    )(o_f[0], o_f[1], o_b[0], o_b[1], rkv, g, ln_g, ln_b, gm)


def _rope_tables():
    pos = np.arange(DEC_SEQ)
    pr, pc = pos // GRID_W, pos % GRID_W
    lane = np.arange(LANES)
    dd = lane % DH_A
    use_col = (dd // 32) == 1
    j = dd % 16
    is_lo = (dd % 32) < 16
    freq = ROPE_THETA ** (-(j.astype(np.float64)) / 16.0)
    p = np.where(use_col[None, :], pc[:, None], pr[:, None]).astype(np.float64)
    ang = (p.astype(np.float32) * freq.astype(np.float32)[None, :]).astype(np.float32)
    cos = np.cos(ang).astype(np.float32)
    sin = np.sin(ang).astype(np.float32)
    s1 = np.where(is_lo[None, :], -sin, 0.0).astype(np.float32)
    s2 = np.where(is_lo[None, :], 0.0, sin).astype(np.float32)
    ident = 512
    cos = np.concatenate([cos, np.ones((ident, LANES), np.float32)], 0)
    s1 = np.concatenate([s1, np.zeros((ident, LANES), np.float32)], 0)
    s2 = np.concatenate([s2, np.zeros((ident, LANES), np.float32)], 0)
    return jnp.asarray(cos), jnp.asarray(s1), jnp.asarray(s2)


def _block_diag(value):
    m = np.zeros((LANES, LANES), np.float32)
    half = LANES // 2
    m[:half, :half] = value
    m[half:, half:] = value
    return jnp.asarray(m)


def _head_spread():
    step = np.arange(SUBLANES * LANES) // LANES
    head = (np.arange(SUBLANES * LANES) % LANES) // DH_D
    out_head = np.arange(2 * LANES) // LANES
    out_step = (np.arange(2 * LANES) % LANES) % SUBLANES
    m = (head[:, None] == out_head[None, :]) & (step[:, None] == out_step[None, :])
    return jnp.asarray(m.astype(np.float32))


def _qb_perm():
    idx = np.zeros(D_B, np.int32)
    for j in range(HB // 2):
        for hh in range(2):
            for dch in range(DH_B):
                idx[j * LANES + hh * DH_B + dch] = (hh * (HB // 2) + j) * DH_B + dch
    return idx


def kernel(x_prompt, x_sample, cache_a_k, cache_a_v, cache_b_k, cache_b_v, state_hgrn, state_rwkv, c, c_ctx, ada_w, ada_b, norm_mix_g, norm_ffn_g, final_norm_g, ev_w_in, ev_w_out, a_lambda, a_subln_g, b_q_norm_g, b_k_norm_g, od_w_in, od_w_out, hgrn_lb_logits, hgrn_norm_g, rwkv_mu, rwkv_w0, rwkv_w_up, rwkv_a0, rwkv_a_up, rwkv_g_up, rwkv_k_k, rwkv_k_a, rwkv_r_k, rwkv_ln_g, rwkv_ln_b, ffn_w_up, ffn_conv_w, ffn_conv_b, ffn_w_down):
    d = D_MODEL
    x = (x_prompt.reshape(N_CTX, d), x_sample.reshape(N_DEC, d), 0)
    cond =jnp.concatenate([c_ctx[None, :], c, jnp.zeros((COND_ROWS - 1 - DEC_BATCH, d), F32)], axis=0)
    mod = _mod_call(cond, ada_w, ada_b).reshape(DEPTH, COND_ROWS, 1, 6 * d)

    cos, s1, s2 = _rope_tables()
    g_mean = _block_diag(1.0 / DH_B)
    g_sum = _block_diag(1.0)
    perm = _qb_perm()
    row = lambda t: t.reshape(1, -1)

    new_ctx = None
    for l in range(DEPTH):
        i = l // 2
        mod_l = mod[l]
        if l % 2 == 0:
            w_in = ev_w_in[i]
            w_in = jnp.concatenate([w_in[:, :1536], w_in[:, 1536:2048][:, perm], w_in[:, 2048:]], axis=1).astype(BF16)
            w_out = ev_w_out[i]
            w_out = jnp.concatenate([w_out[:D_A], w_out[D_A:][perm]], axis=0).astype(BF16)
            qg = row(jnp.tile(b_q_norm_g[i], 2))
            kg = row(jnp.tile(b_k_norm_g[i], 2))
            qa, ka, va, qb, kb, vb, *even_ctx = _even_proj_call(x, mod_l, row(norm_mix_g[l]), w_in, qg, kg, cos, s1,
                                                                s2, g_mean)
            even_ctx = tuple(even_ctx)
            lam_init = 0.8 - 0.6 * math.exp(-0.3 * l)
            sub_g = row(a_subln_g[i])
            oa_c, ob_c = _attn_call(qa, qb, ka, va, kb, vb, None, a_lambda[i], sub_g, lam_init,
                                    n_seq=BATCH, seq=SEQ, row0=0)
            cbk = jnp.transpose(cache_b_k[:, i], (0, 2, 1, 3)).reshape(DEC_BATCH, PAST_LEN, HKV_B * DH_B)
            cbv = jnp.transpose(cache_b_v[:, i], (0, 2, 1, 3)).reshape(DEC_BATCH, PAST_LEN, HKV_B * DH_B)
            cache = (cache_a_k[:, i], cache_a_v[:, i], cbk, cbv)
            oa_s, ob_s = _attn_call(qa, qb, ka, va, kb, vb, cache, a_lambda[i], sub_g, lam_init,
                                    n_seq=DEC_BATCH, seq=DEC_SEQ, row0=N_CTX)
            mix_a = (oa_c, oa_s, 0)
            mix_b = (ob_c, ob_s, 0)
        else:
            w = od_w_in[i]
            w_h = w[:, :HGRN_PROJ].astype(BF16)
            wr = w[:, HGRN_PROJ:]
            mu = rwkv_mu[i]

            def rwkv_cols(t):
                z = jnp.zeros(t.shape[:-1] + (RWKV_PAD - RWKV_PROJ,), t.dtype)
                return jnp.concatenate([t[..., :1664], t[..., 1728:1856], t[..., 1664:1728], z], axis=-1)

            w_r = rwkv_cols(wr).astype(BF16)
            mu_r = row(rwkv_cols(mu))
            g_l = row(norm_mix_g[l])
            assert x[0] is x[1]
            proj_h = _odd_proj_call(x[0], mod_l, g_l, w_h, jnp.zeros((1, HGRN_PROJ), F32), shift=False)
            proj_r = _odd_proj_call(x[0], mod_l, g_l, w_r, mu_r, shift=True)

            ng = row(hgrn_norm_g[i])
            zero_h = jnp.zeros((BATCH, 2, HC, DK_C, DV_C), F32)
            ones = jnp.ones((LANES, LANES), BF16)
            oc_c, sh_c = _hgrn_call(proj_h, hgrn_lb_logits, ng, zero_h, ones, n_seq=BATCH, seq=SEQ, row0=0, lidx=l)
            oc_s, _ = _hgrn_call(proj_h, hgrn_lb_logits, ng, state_hgrn[:, i], ones, n_seq=DEC_BATCH, seq=DEC_SEQ,
                                 row0=N_CTX, lidx=l)

            pad_rows = lambda t: jnp.concatenate([t, jnp.zeros((LANES - t.shape[0], t.shape[1]), F32)], axis=0)
            wup0 = pad_rows(rwkv_w_up[i, 0])
            wup1 = jnp.concatenate([jnp.zeros((W_LORA, D_D), F32), rwkv_w_up[i, 1]], axis=0)
            w0o, w1o, nkk, bb, kt, gg, rkv = _rwkv_prep_call(
                proj_r, row(rwkv_a0[i]), pad_rows(rwkv_a_up[i]), rwkv_g_up[i], row(rwkv_k_k[i]), row(rwkv_k_a[i]),
                row(rwkv_r_k[i]), rwkv_w0[i], wup0, wup1, g_sum)

            def to_tiles(s):
                b = s.shape[0]
                s = s.reshape(b, 2, HD // 2, 2, DH_D, DH_D)
                return jnp.transpose(s, (0, 1, 2, 4, 3, 5)).reshape(b, 2, HD // 2, DH_D, LANES)

            def from_tiles(s):
                b = s.shape[0]
                s = s.reshape(b, 2, HD // 2, DH_D, 2, DH_D)
                return jnp.transpose(s, (0, 1, 2, 4, 3, 5)).reshape(b, 2, HD, DH_D, DH_D)

            zero_r = jnp.zeros((BATCH, 2, HD // 2, DH_D, LANES), F32)
            q1 = g_sum.astype(BF16)
            wsp = _head_spread().astype(BF16)
            of_c, ob_c, sr_c = _rwkv_scan_call(proj_r, w0o, w1o, nkk, bb, kt, zero_r, q1, wsp,
                                               n_seq=BATCH, seq=SEQ, row0=0)
            of_s, ob_s, _ = _rwkv_scan_call(proj_r, w0o, w1o, nkk, bb, kt, to_tiles(state_rwkv[:, i]), q1, wsp,
                                            n_seq=DEC_BATCH, seq=DEC_SEQ, row0=N_CTX)
            mix_b = _whole(_rwkv_post_call((of_c, of_s, 0), (ob_c, ob_s, 0), rkv, gg, row(rwkv_ln_g[i]),
                                           row(rwkv_ln_b[i]), g_mean))
            mix_a = (oc_c, oc_s, 0)
            w_out = od_w_out[i].astype(BF16)
            odd_ctx = (sh_c[:, None], from_tiles(sr_c)[:, None])

        x1 = _out_proj_call(x, mix_a, mix_b, mod_l, w_out)
        x2 = _ffn_call(x1, mod_l, row(norm_ffn_g[l]), ffn_w_up[l].astype(BF16), ffn_conv_w[l], row(ffn_conv_b[l]),
                       ffn_w_down[l].astype(BF16), row(final_norm_g), final=(l == DEPTH - 1))
        if l < DEPTH - 1:
            x = _whole(x2)

    y_prompt = x2[0].reshape(BATCH, SEQ, d)
    y_sample = x2[1].reshape(DEC_BATCH, DEC_SEQ, d)
    return (y_prompt, y_sample) + even_ctx + odd_ctx
```

```python
import functools
import math

import jax
import jax.numpy as jnp
import numpy as np
from jax import lax
from jax.experimental import pallas as pl
from jax.experimental.pallas import tpu as pltpu

D_MODEL = 1024
BATCH = 16
SEQ = 256
DEPTH = 2
DEC_BATCH = 4
DEC_SEQ = 1024
PAST_LEN = 512
GRID_W = 64
ROPE_THETA = 10000.0
EPS = 1e-6
RWKV_GN_EPS = 64e-5
HA = 4
DH_A = 64
DV_A = 2 * DH_A
HB = 8
HKV_B = 2
DH_B = 64
HC = 4
DK_C = 128
DV_C = 128
HD = 8
DH_D = 64
W_LORA = 64
A_LORA = 64
G_LORA = 128
D_FF = 2816

D_A = HA * DV_A
D_B = HB * DH_B
D_C = HC * DV_C
D_D = HD * DH_D
EVEN_PROJ = 2304
HGRN_PROJ = 2560
RWKV_PROJ = 1856
RWKV_PAD = 1920

N_CTX = BATCH * SEQ
N_DEC = DEC_BATCH * DEC_SEQ
N_TOK = N_CTX + N_DEC
COND_ROWS = 8

LANES = 128
SUBLANES = 8
VMEM_LIMIT = 56 * 1024 * 1024

F32 = jnp.float32
BF16 = jnp.bfloat16
HI = lax.Precision.HIGHEST

FF_TILE = 256
HGRN_L = 128
HGRN_HP = 2
RWKV_TB = 128
RWKV_NS = 4
RWKV_GROUP = 1
RWKV_LAG = 3
assert RWKV_LAG < 2 * RWKV_NS // RWKV_GROUP


def _dot(a, b):
    return jnp.dot(a, b, preferred_element_type=F32)


def _dot_hi(a, b):
    return jnp.dot(a, b, preferred_element_type=F32, precision=HI)


def _dot_b16(a, b):
    return _dot(a.astype(BF16), b.astype(BF16))


def _dot_nt(a, b, precision=None):
    return lax.dot_general(a, b, (((1,), (1,)), ((), ())), preferred_element_type=F32, precision=precision)


def _dot_tn(a, b, precision=None):
    return lax.dot_general(a, b, (((0,), (0,)), ((), ())), preferred_element_type=F32, precision=precision)


def _sigmoid(x):
    return 1.0 / (1.0 + jnp.exp(-x))


def _silu(x):
    return x * _sigmoid(x)


def _norm_mod(x, g, sc, sh):
    ms = jnp.mean(x * x, axis=-1, keepdims=True)
    return (x * lax.rsqrt(ms + EPS) * g) * (1.0 + sc) + sh


def _params(sem):
    return pltpu.CompilerParams(dimension_semantics=sem, vmem_limit_bytes=VMEM_LIMIT)


def _cond_row(i, tm):
    r0 = i * tm
    return jnp.where(r0 < N_CTX, 0, 1 + (r0 - N_CTX) // DEC_SEQ)


def _seq_len(i, tm):
    return jnp.where(i * tm < N_CTX, SEQ, DEC_SEQ)


ROW_PAD = SUBLANES


def _zero_row_pads(u_ref, tm):
    for k in range(u_ref.shape[0]):
        u_ref[k, 0:ROW_PAD, :] = jnp.zeros((ROW_PAD, u_ref.shape[2]), u_ref.dtype)
        u_ref[k, ROW_PAD + tm:2 * ROW_PAD + tm, :] = jnp.zeros((ROW_PAD, u_ref.shape[2]), u_ref.dtype)


def _neighbour_rows(u_ref, k, c, ch, seq_len):
    r0 = ROW_PAD + c * ch
    row8 = lax.broadcasted_iota(jnp.int32, (SUBLANES, 1), 0)
    at_start = (c * ch) % seq_len == 0
    at_end = ((c + 1) * ch) % seq_len == 0
    prev = u_ref[k, r0 - 1:r0 - 1 + ch, :]
    nxt = u_ref[k, r0 + 1:r0 + 1 + ch, :]
    prev = jnp.concatenate([jnp.where((row8 == 0) & at_start, 0.0, prev[0:SUBLANES]), prev[SUBLANES:]], axis=0)
    nxt = jnp.concatenate(
        [nxt[:ch - SUBLANES], jnp.where((row8 == SUBLANES - 1) & at_end, 0.0, nxt[ch - SUBLANES:])], axis=0)
    return prev, nxt


def _pick(tm, lo_ref, hi_ref, rows=slice(None)):
    return jnp.where(pl.program_id(0) * tm < N_CTX, lo_ref[rows, :], hi_ref[rows, :])


def _pair_specs(tm, width, pair):
    nct = N_CTX // tm
    hi0 = pair[2] // tm
    return [pl.BlockSpec((tm, width), lambda i: (jnp.minimum(i, nct - 1), 0)),
            pl.BlockSpec((tm, width), lambda i: (hi0 + jnp.maximum(i - nct, 0), 0))]


def _whole(t):
    return (t, t, N_CTX)


def _mod_kernel(c_ref, w_ref, b_ref, o_ref):
    s = _silu(c_ref[...]).astype(BF16)
    o_ref[...] = _dot(s, w_ref[...].astype(BF16)) + b_ref[...]


def _mod_call(cond, ada_w, ada_b):
    tn = 1536
    n = 6 * D_MODEL
    return pl.pallas_call(
        _mod_kernel,
        grid=(DEPTH, n // tn),
        in_specs=[
            pl.BlockSpec((COND_ROWS, D_MODEL), lambda l, j: (0, 0)),
            pl.BlockSpec((None, D_MODEL, tn), lambda l, j: (l, 0, j)),
            pl.BlockSpec((None, 1, tn), lambda l, j: (l, 0, j)),
        ],
        out_specs=pl.BlockSpec((None, COND_ROWS, tn), lambda l, j: (l, 0, j)),
        out_shape=jax.ShapeDtypeStruct((DEPTH, COND_ROWS, n), F32),
        compiler_params=_params(("arbitrary", "arbitrary")),
        name="ada_mod",
    )(cond, ada_w, ada_b.reshape(DEPTH, 1, n))


def _even_proj_kernel(xl_ref, xh_ref, mod_ref, g_ref, w_ref, qg_ref, kg_ref, cos_ref, s1_ref, s2_ref, gm_ref,
                      qa_ref, ka_ref, va_ref, qb_ref, kb_ref, vb_ref, cak_ref, cav_ref, cbk_ref, cbv_ref, *, tm):
    d = D_MODEL
    gm = gm_ref[...]
    scale = DH_A ** -0.5
    is_ctx = pl.program_id(0) * tm < N_CTX

    def head_norm(t, g):
        return t * lax.rsqrt(_dot_b16(t * t, gm) + EPS) * g

    for c in range(tm // SEQ):
        rows = slice(c * SEQ, (c + 1) * SEQ)
        h = _norm_mod(_pick(tm, xl_ref, xh_ref, rows), g_ref[...], mod_ref[:, d:2 * d], mod_ref[:, 0:d]).astype(BF16)
        proj = _dot(h, w_ref[...])
        cos, s1, s2 = cos_ref[rows, :], s1_ref[rows, :], s2_ref[rows, :]

        def rope(t):
            return t * cos + pltpu.roll(t, LANES - 16, 1) * s1 + pltpu.roll(t, 16, 1) * s2

        for j in range(4):
            sl = slice(j * LANES, (j + 1) * LANES)
            qa_ref[rows, sl] = rope(proj[:, j * LANES:(j + 1) * LANES]) * scale
            ka_ref[rows, sl] = rope(proj[:, 512 + j * LANES:512 + (j + 1) * LANES])
            va_ref[rows, sl] = proj[:, 1024 + j * LANES:1024 + (j + 1) * LANES]
            qb_ref[rows, sl] = rope(head_norm(proj[:, 1536 + j * LANES:1536 + (j + 1) * LANES], qg_ref[...])) * scale
        kb_ref[rows, :] = rope(head_norm(proj[:, 2048:2176], kg_ref[...]))
        vb_ref[rows, :] = proj[:, 2176:2304]

        @pl.when(is_ctx)
        def _(c=c, rows=rows):
            for hh in range(HA):
                sl = slice(hh * DV_A, (hh + 1) * DV_A)
                cak_ref[c, hh] = ka_ref[rows, sl]
                cav_ref[c, hh] = va_ref[rows, sl]
            for hh in range(HKV_B):
                sl = slice(hh * DH_B, (hh + 1) * DH_B)
                cbk_ref[c, hh] = kb_ref[rows, sl]
                cbv_ref[c, hh] = vb_ref[rows, sl]


def _even_proj_call(x, mod_l, g, w, qg, kg, cos, s1, s2, gm):
    tm = 512
    nt = N_TOK // tm
    n_rope_blk = DEC_SEQ // tm

    def rope_idx(i):
        return (jnp.where(i * tm < N_CTX, n_rope_blk, (i - N_CTX // tm) % n_rope_blk), 0)

    full = lambda shape: pl.BlockSpec(shape, lambda i: (0,) * len(shape))
    out512 = pl.BlockSpec((tm, 512), lambda i: (i, 0))
    out128 = pl.BlockSpec((tm, LANES), lambda i: (i, 0))
    last_ctx = N_CTX // tm - 1

    def cache(heads, width):
        return pl.BlockSpec((tm // SEQ, None, heads, SEQ, width), lambda i: (jnp.minimum(i, last_ctx), 0, 0, 0, 0))

    return pl.pallas_call(
        functools.partial(_even_proj_kernel, tm=tm),
        grid=(nt,),
        in_specs=_pair_specs(tm, D_MODEL, x) + [
            pl.BlockSpec((None, 1, 6 * D_MODEL), lambda i: (_cond_row(i, tm), 0, 0)),
            full((1, D_MODEL)),
            full((D_MODEL, EVEN_PROJ)),
            full((1, LANES)),
            full((1, LANES)),
            pl.BlockSpec((tm, LANES), rope_idx),
            pl.BlockSpec((tm, LANES), rope_idx),
            pl.BlockSpec((tm, LANES), rope_idx),
            full((LANES, LANES)),
        ],
        out_specs=[out512, out512, out512, out512, out128, out128, cache(HA, DV_A), cache(HA, DV_A),
                   cache(HKV_B, DH_B), cache(HKV_B, DH_B)],
        out_shape=[jax.ShapeDtypeStruct((N_TOK, 512), F32)] * 4 + [jax.ShapeDtypeStruct((N_TOK, LANES), F32)] * 2
        + [jax.ShapeDtypeStruct((BATCH, 1, HA, SEQ, DV_A), F32)] * 2
        + [jax.ShapeDtypeStruct((BATCH, 1, HKV_B, SEQ, DH_B), F32)] * 2,
        compiler_params=_params(("arbitrary",)),
        name="even_proj",
    )(x[0], x[1], mod_l, g, w, qg, kg, cos, s1, s2, gm)


def _softmax_pv(q, ks, vs):
    ss = [_dot_nt(q, k) for k in ks]
    m = functools.reduce(jnp.maximum, [jnp.max(s, axis=-1, keepdims=True) for s in ss])
    ps = [jnp.exp(s - m) for s in ss]
    l = functools.reduce(jnp.add, [jnp.sum(p, axis=-1, keepdims=True) for p in ps])
    acc = functools.reduce(jnp.add, [_dot(p.astype(BF16), v) for p, v in zip(ps, vs)])
    return acc / l


def _attn_kernel(*refs, has_cache, lam_init):
    if has_cache:
        (qa_ref, qb_ref, ka_ref, va_ref, kb_ref, vb_ref, cak_ref, cav_ref, cbk_ref, cbv_ref,
         al_ref, sg_ref, oa_ref, ob_ref) = refs
    else:
        qa_ref, qb_ref, ka_ref, va_ref, kb_ref, vb_ref, al_ref, sg_ref, oa_ref, ob_ref = refs
    al = al_ref[...]
    lam = (jnp.exp(jnp.sum(al[0:1] * al[1:2], axis=-1, keepdims=True))
           - jnp.exp(jnp.sum(al[2:3] * al[3:4], axis=-1, keepdims=True)) + lam_init)
    lo = lax.broadcasted_iota(jnp.int32, (1, LANES), 1) < DH_A

    for h in range(HA):
        sl = slice(h * LANES, (h + 1) * LANES)
        q = qa_ref[:, sl]
        ks = [ka_ref[:, sl].astype(BF16)]
        vs = [va_ref[:, sl].astype(BF16)]
        if has_cache:
            ks.insert(0, cak_ref[h].astype(BF16))
            vs.insert(0, cav_ref[h].astype(BF16))
        a1 = _softmax_pv(jnp.where(lo, q, 0.0).astype(BF16), ks, vs)
        a2 = _softmax_pv(jnp.where(lo, 0.0, q).astype(BF16), ks, vs)
        dlt = a1 - lam * a2
        ms = jnp.mean(dlt * dlt, axis=-1, keepdims=True)
        oa_ref[:, sl] = dlt * lax.rsqrt(ms + EPS) * sg_ref[...] * (1.0 - lam_init)

    ks = [kb_ref[...].astype(BF16)]
    vs = [vb_ref[...].astype(BF16)]
    if has_cache:
        ks.insert(0, cbk_ref[...].astype(BF16))
        vs.insert(0, cbv_ref[...].astype(BF16))
    for j in range(HB // 2):
        sl = slice(j * LANES, (j + 1) * LANES)
        q = qb_ref[:, sl]
        o0 = _softmax_pv(jnp.where(lo, q, 0.0).astype(BF16), ks, vs)
        o1 = _softmax_pv(jnp.where(lo, 0.0, q).astype(BF16), ks, vs)
        ob_ref[:, sl] = jnp.where(lo, o0, o1)


def _attn_call(qa, qb, ka, va, kb, vb, cache, a_lambda, subln_g, lam_init, *, n_seq, seq, row0):
    tq = 256
    nq = seq // tq
    qblk0 = row0 // tq
    sblk0 = row0 // seq
    has_cache = cache is not None
    qspec = pl.BlockSpec((tq, 512), lambda b, i: (qblk0 + b * nq + i, 0))
    own512 = pl.BlockSpec((seq, 512), lambda b, i: (sblk0 + b, 0))
    own128 = pl.BlockSpec((seq, LANES), lambda b, i: (sblk0 + b, 0))
    in_specs = [qspec, qspec, own512, own512, own128, own128]
    args = [qa, qb, ka, va, kb, vb]
    if has_cache:
        in_specs += [
            pl.BlockSpec((None, HA, PAST_LEN, LANES), lambda b, i: (b, 0, 0, 0)),
            pl.BlockSpec((None, HA, PAST_LEN, LANES), lambda b, i: (b, 0, 0, 0)),
            pl.BlockSpec((None, PAST_LEN, LANES), lambda b, i: (b, 0, 0)),
            pl.BlockSpec((None, PAST_LEN, LANES), lambda b, i: (b, 0, 0)),
        ]
        args += list(cache)
    in_specs += [pl.BlockSpec((4, DH_A), lambda b, i: (0, 0)), pl.BlockSpec((1, LANES), lambda b, i: (0, 0))]
    args += [a_lambda, subln_g]
    ospec = pl.BlockSpec((tq, 512), lambda b, i: (b * nq + i, 0))
    return pl.pallas_call(
        functools.partial(_attn_kernel, has_cache=has_cache, lam_init=lam_init),
        grid=(n_seq, nq),
        in_specs=in_specs,
        out_specs=[ospec, ospec],
        out_shape=[jax.ShapeDtypeStruct((n_seq * seq, 512), F32)] * 2,
        compiler_params=_params(("arbitrary", "arbitrary")),
        name="attn_dec" if has_cache else "attn_ctx",
    )(*args)


def _out_proj_kernel(xl_ref, xh_ref, al_ref, ah_ref, bl_ref, bh_ref, mod_ref, w_ref, o_ref, *, tm):
    d = D_MODEL
    half = al_ref.shape[1]
    a = _pick(tm, al_ref, ah_ref).astype(BF16)
    b = _pick(tm, bl_ref, bh_ref).astype(BF16)
    mix = _dot(a, w_ref[0:half, :]) + _dot(b, w_ref[half:2 * half, :])
    o_ref[...] = _pick(tm, xl_ref, xh_ref) + mod_ref[:, 2 * d:3 * d] * mix


def _out_proj_call(x, a, b, mod_l, w):
    tm = 512
    return pl.pallas_call(
        functools.partial(_out_proj_kernel, tm=tm),
        grid=(N_TOK // tm,),
        in_specs=_pair_specs(tm, D_MODEL, x) + _pair_specs(tm, 512, a) + _pair_specs(tm, 512, b) + [
            pl.BlockSpec((None, 1, 6 * D_MODEL), lambda i: (_cond_row(i, tm), 0, 0)),
            pl.BlockSpec((D_MODEL, D_MODEL), lambda i: (0, 0)),
        ],
        out_specs=pl.BlockSpec((tm, D_MODEL), lambda i: (i, 0)),
        out_shape=jax.ShapeDtypeStruct((N_TOK, D_MODEL), F32),
        compiler_params=_params(("arbitrary",)),
        name="out_proj",
    )(x[0], x[1], a[0], a[1], b[0], b[1], mod_l, w)


def _ffn_kernel(x_ref, mod_ref, g_ref, wv_ref, wg_ref, cwv_ref, cwg_ref, cbv_ref, cbg_ref, wd_ref, fg_ref,
                *rest, tm, final):
    if final:
        oc_ref, od_ref, h_scr, acc_scr, u_scr = rest
    else:
        o_ref, h_scr, acc_scr, u_scr = rest
    d = D_MODEL
    i, j = pl.program_id(0), pl.program_id(1)
    ch = SEQ

    @pl.when(j == 0)
    def _():
        h_scr[...] = _norm_mod(x_ref[...], g_ref[...], mod_ref[:, 4 * d:5 * d], mod_ref[:, 3 * d:4 * d]).astype(BF16)
        acc_scr[...] = jnp.zeros_like(acc_scr)
        _zero_row_pads(u_scr, tm)

    seq_len = _seq_len(i, tm)

    def project(c):
        rows = slice(c * ch, (c + 1) * ch)
        u_scr[0, ROW_PAD + c * ch:ROW_PAD + (c + 1) * ch, :] = _dot(h_scr[rows, :], wv_ref[...])
        u_scr[1, ROW_PAD + c * ch:ROW_PAD + (c + 1) * ch, :] = _dot(h_scr[rows, :], wg_ref[...])

    def conv(k, c, cw_ref, cb_ref):
        prev, nxt = _neighbour_rows(u_scr, k, c, ch, seq_len)
        cur = u_scr[k, ROW_PAD + c * ch:ROW_PAD + (c + 1) * ch, :]
        return cw_ref[0:1, :] * prev + cw_ref[1:2, :] * cur + cw_ref[2:3, :] * nxt + cb_ref[...]

    def activate(c):
        val = conv(0, c, cwv_ref, cbv_ref)
        gate = conv(1, c, cwg_ref, cbg_ref)
        act = (_silu(gate) * val).astype(BF16)
        acc_scr[c * ch:(c + 1) * ch, :] += _dot(act, wd_ref[...])

    n_chunks = tm // ch
    project(0)
    for c in range(n_chunks):
        if c + 1 < n_chunks:
            project(c + 1)
        activate(c)

    @pl.when(j == pl.num_programs(1) - 1)
    def _():
        y = x_ref[...] + mod_ref[:, 5 * d:6 * d] * acc_scr[...]
        if not final:
            o_ref[...] = y
            return
        ms = jnp.mean(y * y, axis=-1, keepdims=True)
        y = y * lax.rsqrt(ms + EPS) * fg_ref[...]

        @pl.when(i * tm < N_CTX)
        def _():
            oc_ref[...] = y

        @pl.when(i * tm >= N_CTX)
        def _():
            od_ref[...] = y


def _ffn_call(x, mod_l, g, w_up, conv_w, conv_b, w_down, final_g, *, layer, final):
    tm = DEC_SEQ
    nf = D_FF // FF_TILE
    nct = N_CTX // tm
    if final:
        out_specs = [pl.BlockSpec((tm, D_MODEL), lambda i, j: (jnp.minimum(i, nct - 1), 0)),
                     pl.BlockSpec((tm, D_MODEL), lambda i, j: (jnp.maximum(i - nct, 0), 0))]
        out_shape = [jax.ShapeDtypeStruct((N_CTX, D_MODEL), F32), jax.ShapeDtypeStruct((N_DEC, D_MODEL), F32)]
    else:
        out_specs = pl.BlockSpec((tm, D_MODEL), lambda i, j: (i, 0))
        out_shape = jax.ShapeDtypeStruct((N_TOK, D_MODEL), F32)
    return pl.pallas_call(
        functools.partial(_ffn_kernel, tm=tm, final=final),
        grid=(N_TOK // tm, nf),
        in_specs=[
            pl.BlockSpec((tm, D_MODEL), lambda i, j: (i, 0)),
            pl.BlockSpec((None, 1, 6 * D_MODEL), lambda i, j: (_cond_row(i, tm), 0, 0)),
            pl.BlockSpec((1, D_MODEL), lambda i, j: (0, 0)),
            pl.BlockSpec((None, D_MODEL, FF_TILE), lambda i, j: (layer, 0, j)),
            pl.BlockSpec((None, D_MODEL, FF_TILE), lambda i, j: (layer, 0, nf + j)),
            pl.BlockSpec((None, 3, FF_TILE), lambda i, j: (layer, 0, j)),
            pl.BlockSpec((None, 3, FF_TILE), lambda i, j: (layer, 0, nf + j)),
            pl.BlockSpec((None, 1, FF_TILE), lambda i, j: (layer, 0, j)),
            pl.BlockSpec((None, 1, FF_TILE), lambda i, j: (layer, 0, nf + j)),
            pl.BlockSpec((None, FF_TILE, D_MODEL), lambda i, j: (layer, j, 0)),
            pl.BlockSpec((1, D_MODEL), lambda i, j: (0, 0)),
        ],
        out_specs=out_specs,
        out_shape=out_shape,
        scratch_shapes=[pltpu.VMEM((tm, D_MODEL), BF16), pltpu.VMEM((tm, D_MODEL), F32),
                        pltpu.VMEM((2, tm + 2 * SUBLANES, FF_TILE), F32)],
        compiler_params=_params(("arbitrary", "arbitrary")),
        name="conv_ffn",
    )(x, mod_l, g, w_up, w_up, conv_w, conv_w, conv_b, conv_b, w_down, final_g)


def _odd_proj_kernel(x_ref, mod_ref, g_ref, w_ref, mu_ref, o_ref, h_scr, *u_scr, tm, shift):
    d = D_MODEL
    i, j = pl.program_id(0), pl.program_id(1)
    ch = SEQ
    n_chunks = tm // ch

    @pl.when(j == 0)
    def _():
        h_scr[...] = _norm_mod(x_ref[...], g_ref[...], mod_ref[:, d:2 * d], mod_ref[:, 0:d]).astype(BF16)
        if shift:
            _zero_row_pads(u_scr[0], tm)

    if not shift:
        for c in range(n_chunks):
            o_ref[c * ch:(c + 1) * ch, :] = _dot(h_scr[c * ch:(c + 1) * ch, :], w_ref[...])
        return

    u = u_scr[0]
    seq_len = _seq_len(i, tm)

    def project(c):
        u[0, ROW_PAD + c * ch:ROW_PAD + (c + 1) * ch, :] = _dot(h_scr[c * ch:(c + 1) * ch, :], w_ref[...])

    project(0)
    for c in range(n_chunks):
        if c + 1 < n_chunks:
            project(c + 1)
        prev, nxt = _neighbour_rows(u, 0, c, ch, seq_len)
        p = u[0, ROW_PAD + c * ch:ROW_PAD + (c + 1) * ch, :]
        o_ref[c * ch:(c + 1) * ch, :] = p + mu_ref[...] * (0.5 * (prev + nxt) - p)


def _odd_proj_call(x, mod_l, g, w, mu, *, shift):
    tm = DEC_SEQ
    n = w.shape[1]
    tn = n if shift else n // 2
    return pl.pallas_call(
        functools.partial(_odd_proj_kernel, tm=tm, shift=shift),
        grid=(N_TOK // tm, n // tn),
        in_specs=[
            pl.BlockSpec((tm, D_MODEL), lambda i, j: (i, 0)),
            pl.BlockSpec((None, 1, 6 * D_MODEL), lambda i, j: (_cond_row(i, tm), 0, 0)),
            pl.BlockSpec((1, D_MODEL), lambda i, j: (0, 0)),
            pl.BlockSpec((D_MODEL, tn), lambda i, j: (0, j)),
            pl.BlockSpec((1, tn), lambda i, j: (0, j)),
        ],
        out_specs=pl.BlockSpec((tm, tn), lambda i, j: (i, j)),
        out_shape=jax.ShapeDtypeStruct((N_TOK, n), F32),
        scratch_shapes=[pltpu.VMEM((tm, D_MODEL), BF16)]
        + ([pltpu.VMEM((1, tm + 2 * ROW_PAD, tn), F32)] if shift else []),
        compiler_params=_params(("arbitrary", "arbitrary")),
        name="odd_proj_rwkv" if shift else "odd_proj_hgrn",
    )(x, mod_l, g, w, mu)


def _hgrn_kernel(q_ref, ff_ref, fb_ref, v_ref, gc_ref, lbl_ref, ng_ref, s0_ref, ones_ref, o_ref, sfin_ref,
                 oacc_f, oacc_b, st_scr, qd_scr, add_scr, dec_scr, start_scr, *, seq, lidx):
    L = HGRN_L
    nc = seq // L
    nb = L // SUBLANES
    ones = ones_ref[...]
    rowi = lax.broadcasted_iota(jnp.int32, (L, L), 0)
    coli = lax.broadcasted_iota(jnp.int32, (L, L), 1)
    rowv = lax.broadcasted_iota(jnp.int32, (L, 1), 0)

    def lower_bound(drc):
        lg = lbl_ref[drc]
        e = jnp.exp(lg - jnp.max(lg, axis=0, keepdims=True))
        sm = e / jnp.sum(e, axis=0, keepdims=True)
        return functools.reduce(jnp.add, [sm[i:i + 1, :] for i in range(1, lidx + 1)])

    def spread(t, s_l):
        return jnp.concatenate(
            [jnp.broadcast_to(t[b * SUBLANES + s_l:b * SUBLANES + s_l + 1, :], (SUBLANES, LANES)) for b in range(nb)],
            axis=0)

    def chunk(rev, r0, f_ref, lb, tri, hs):
        q = _silu(q_ref[pl.ds(r0, L), hs])
        f = lb + (1.0 - lb) * _sigmoid(f_ref[pl.ds(r0, L), hs])
        k = 1.0 - f
        lf = jnp.log(f)
        v = v_ref[pl.ds(r0, L), hs].astype(BF16)
        cum = _dot_hi(tri, lf)
        cum_ex = cum - lf
        q_dec = (q * jnp.exp(cum)).astype(BF16)

        scores = jnp.zeros((L, L), F32)
        sub = rowv % SUBLANES
        h = L // 2
        while h >= 1:
            if h >= SUBLANES:
                pieces = []
                for a in range(0, L, 2 * h):
                    edge = a + h - 1 if rev else a + h
                    pieces.append(jnp.broadcast_to(cum_ex[edge:edge + 1, :], (2 * h, LANES)))
                anchor = jnp.concatenate(pieces, axis=0) if len(pieces) > 1 else pieces[0]
            else:
                anchor = None
                for a in range(0, SUBLANES, 2 * h):
                    edge = a + h - 1 if rev else a + h
                    cand = spread(cum_ex, edge)
                    anchor = cand if anchor is None else jnp.where(sub >= a, cand, anchor)
            is_q = ((rowv // h) % 2) == (0 if rev else 1)
            d = cum - anchor
            e = jnp.exp(jnp.where(is_q, d, -d))
            qe = jnp.where(is_q, q * e, 0.0).astype(BF16)
            ke = jnp.where(is_q, 0.0, k * e).astype(BF16)
            same_pair = (rowi // (2 * h)) == (coli // (2 * h))
            scores = scores + jnp.where(same_pair, _dot_nt(qe, ke), 0.0)
            h //= 2

        own = _dot((q * k).astype(BF16), ones)
        own = own[:, 0:L] if L <= LANES else jnp.concatenate([own] * (L // LANES), axis=1)
        scores = jnp.where(rowi == coli, own, scores)
        o = _dot(scores.astype(BF16), v)

        end = 0 if rev else L - 1
        cend = cum[end:end + 1, :]
        kd = (k * jnp.exp(cend - cum)).astype(BF16)
        return q_dec, o, _dot_tn(v, kd), jnp.exp(cend)

    lb_f, lb_b = lower_bound(0), lower_bound(1)
    tri_f = jnp.where(coli <= rowi, 1.0, 0.0).astype(F32)
    tri_b = jnp.where(coli >= rowi, 1.0, 0.0).astype(F32)
    heads = [slice(hh * LANES, (hh + 1) * LANES) for hh in range(HGRN_HP)]
    chains = [(hh, hs, drc) for hh, hs in enumerate(heads) for drc in range(2)]
    oaccs = (oacc_f, oacc_b)

    def local(ci, carry):
        r0 = pl.multiple_of(ci * L, L)
        for c, (hh, hs, drc) in enumerate(chains):
            q_dec, o, add, dec = chunk(drc == 1, r0, fb_ref if drc else ff_ref, (lb_b if drc else lb_f)[:, hs],
                                       tri_b if drc else tri_f, hs)
            qd_scr[c, pl.ds(r0, L), :] = q_dec
            oaccs[drc][pl.ds(r0, L), hs] = o
            add_scr[c * nc + ci] = add
            dec_scr[c * nc + ci] = jnp.broadcast_to(dec, (SUBLANES, LANES))
        return carry

    lax.fori_loop(0, nc, local, 0, unroll=min(2, nc))

    for c, (hh, hs, drc) in enumerate(chains):
        st_scr[c] = s0_ref[drc, hh].T

    def scan(n, carry):
        for c, (hh, hs, drc) in enumerate(chains):
            slot = c * nc + ((nc - 1 - n) if drc else n)
            st = st_scr[c]
            start_scr[slot] = st.astype(BF16)
            st_scr[c] = st * dec_scr[slot][0:1, :] + add_scr[slot]
        return carry

    lax.fori_loop(0, nc, scan, 0)

    def carried(ci, carry):
        r0 = pl.multiple_of(ci * L, L)
        for c, (hh, hs, drc) in enumerate(chains):
            oaccs[drc][pl.ds(r0, L), hs] += _dot_nt(qd_scr[c, pl.ds(r0, L), :], start_scr[c * nc + ci])
        return carry

    lax.fori_loop(0, nc, carried, 0)

    for hh, hs in enumerate(heads):
        sfin_ref[0, hh] = st_scr[2 * hh].T
        sfin_ref[1, hh] = st_scr[2 * hh + 1].T
        o = oacc_f[:, hs] + oacc_b[:, hs]
        ms = jnp.mean(o * o, axis=-1, keepdims=True)
        o_ref[:, hs] = o * lax.rsqrt(ms + EPS) * ng_ref[...] * _silu(gc_ref[:, hs])


def _hgrn_call(proj_h, lb_logits, norm_g, s0, ones, *, n_seq, seq, row0, lidx):
    sblk0 = row0 // seq

    hp = HGRN_HP
    width = hp * LANES

    def col(section):
        return pl.BlockSpec((seq, width), lambda b, h: (sblk0 + b, section * (HC // hp) + h))

    st_spec = pl.BlockSpec((None, 2, hp, DK_C, DV_C), lambda b, h: (b, 0, h, 0, 0))
    return pl.pallas_call(
        functools.partial(_hgrn_kernel, seq=seq, lidx=lidx),
        grid=(n_seq, HC // hp),
        in_specs=[
            col(0), col(1), col(2), col(3), col(4),
            pl.BlockSpec((2, DEPTH, width), lambda b, h: (0, 0, h)),
            pl.BlockSpec((1, LANES), lambda b, h: (0, 0)),
            st_spec,
            pl.BlockSpec((LANES, LANES), lambda b, h: (0, 0)),
        ],
        out_specs=[pl.BlockSpec((seq, width), lambda b, h: (b, h)), st_spec],
        out_shape=[jax.ShapeDtypeStruct((n_seq * seq, D_C), F32),
                   jax.ShapeDtypeStruct((n_seq, 2, HC, DK_C, DV_C), F32)],
        scratch_shapes=[pltpu.VMEM((seq, width), F32), pltpu.VMEM((seq, width), F32),
                        pltpu.VMEM((2 * hp, DV_C, DK_C), F32),
                        pltpu.VMEM((2 * hp, seq, DK_C), BF16),
                        pltpu.VMEM((2 * hp * (seq // HGRN_L), DV_C, DK_C), F32),
                        pltpu.VMEM((2 * hp * (seq // HGRN_L), SUBLANES, DK_C), F32),
                        pltpu.VMEM((2 * hp * (seq // HGRN_L), DV_C, DK_C), BF16)],
        compiler_params=_params(("arbitrary", "arbitrary")),
        name="hgrn_dec" if row0 else "hgrn_ctx",
    )(proj_h, proj_h, proj_h, proj_h, proj_h, lb_logits, norm_g, s0, ones)


def _rwkv_prep_kernel(p_ref, a0_ref, aup_ref, gup_ref, kk_ref, ka_ref, rk_ref, w0_ref, wup0_ref, wup1_ref,
                      gs_ref, w0o_ref, w1o_ref, nkk_ref, bb_ref, kt_ref, g_ref, rkv_ref):
    r = p_ref[:, 0:512]
    k = p_ref[:, 512:1024]
    v = p_ref[:, 1024:1536]
    wd = p_ref[:, 1536:1664]
    gd = p_ref[:, 1664:1792]
    ad = p_ref[:, 1792:1920]
    gs = gs_ref[...]

    def group_sum(t):
        return jnp.concatenate([_dot_b16(t[:, j * LANES:(j + 1) * LANES], gs) for j in range(4)], axis=1)

    a = _sigmoid(a0_ref[...] + _dot_b16(ad, aup_ref[...]))
    g_ref[...] = _dot_b16(_sigmoid(gd), gup_ref[...])
    kkr = k * kk_ref[...]
    kk = kkr / jnp.maximum(jnp.sqrt(group_sum(kkr * kkr)), 1e-12)
    kt = k * (1.0 + (a - 1.0) * ka_ref[...])
    th = jnp.tanh(wd)
    decay = math.exp(-0.5)
    w0o_ref[...] = jnp.exp(-decay * _sigmoid(w0_ref[0:1, :] + _dot_b16(th, wup0_ref[...])))
    w1o_ref[...] = jnp.exp(-decay * _sigmoid(w0_ref[1:2, :] + _dot_b16(th, wup1_ref[...])))
    nkk_ref[...] = -kk
    bb_ref[...] = kk * a
    kt_ref[...] = kt
    rkv_ref[...] = group_sum(r * kt * rk_ref[...]) * v


def _rwkv_prep_call(proj_r, a0, aup, gup, kk_k, k_a, r_k, w0, wup0, wup1, gs):
    tm = 512
    full = lambda shape: pl.BlockSpec(shape, lambda i: (0,) * len(shape))
    ospec = pl.BlockSpec((tm, D_D), lambda i: (i, 0))
    return pl.pallas_call(
        _rwkv_prep_kernel,
        grid=(N_TOK // tm,),
        in_specs=[
            pl.BlockSpec((tm, RWKV_PAD), lambda i: (i, 0)),
            full((1, D_D)), full((LANES, D_D)), full((LANES, D_D)), full((1, D_D)), full((1, D_D)), full((1, D_D)),
            full((2, D_D)), full((LANES, D_D)), full((LANES, D_D)), full((LANES, LANES)),
        ],
        out_specs=[ospec] * 7,
        out_shape=[jax.ShapeDtypeStruct((N_TOK, D_D), F32)] * 7,
        compiler_params=_params(("arbitrary",)),
        name="rwkv_prep",
    )(proj_r, a0, aup, gup, kk_k, k_a, r_k, w0, wup0, wup1, gs)


def _rwkv_scan_kernel(rf_ref, wf_ref, nf_ref, bf_ref, kf_ref, vf_ref, rb_ref, wb_ref, nb_ref, bb_ref, kb_ref,
                      vb_ref, s0_ref, q1_ref, wsp_ref, of_ref, ob_ref, sfin_ref, *scratch):
    tb = pl.program_id(1)
    ns = RWKV_NS
    ngrp = D_D // LANES
    steps = RWKV_TB
    members = [(s, drc) for s in range(ns) for drc in range(2)]
    groups = [members[i:i + RWKV_GROUP] for i in range(0, len(members), RWKV_GROUP)]
    st, ot, pabuf, obuf = (scratch[i * len(groups):(i + 1) * len(groups)] for i in range(4))
    m_rows = RWKV_GROUP * ngrp * DH_D
    dirs = ((rf_ref, wf_ref, nf_ref, bf_ref, kf_ref, vf_ref), (rb_ref, wb_ref, nb_ref, bb_ref, kb_ref, vb_ref))
    o_refs = (of_ref, ob_ref)

    def rows(c, n=DH_D, base=0):
        return slice(base + c * n, base + (c + 1) * n)

    def chains(q):
        return [(m * ngrp + g, m, s, drc, g) for m, (s, drc) in enumerate(groups[q]) for g in range(ngrp)]

    @pl.when(tb == 0)
    def _():
        for q in range(len(groups)):
            for c, _, s, drc, g in chains(q):
                st[q][rows(c), :] = s0_ref[s, drc, g]

    for q in range(len(groups)):
        ot[q][...] = jnp.zeros_like(ot[q])
    lane = lax.broadcasted_iota(jnp.int32, (1, LANES), 1)
    diag = (lax.broadcasted_iota(jnp.int32, (DH_D, LANES), 1) & (DH_D - 1)) == lax.broadcasted_iota(
        jnp.int32, (DH_D, LANES), 0)

    def step_group(t8, carry):
        bases = (pl.multiple_of(t8 * SUBLANES, SUBLANES), pl.multiple_of(steps - (t8 + 1) * SUBLANES, SUBLANES))
        blk = [[[ref[s, pl.ds(bases[drc], SUBLANES), :] for ref in dirs[drc]] for s, drc in grp] for grp in groups]

        def issue(j, q):
            for c, m, s, drc, g in chains(q):
                loc = (SUBLANES - 1 - j) if drc else j
                sl = slice(g * LANES, (g + 1) * LANES)
                n_b, v_b = blk[q][m][2], blk[q][m][5]
                pabuf[q][rows(c), :] = (st[q][rows(c), :] * n_b[loc:loc + 1, sl]).astype(BF16)
                pabuf[q][rows(c, base=m_rows), :] = jnp.where(diag, v_b[loc:loc + 1, sl], 0.0).astype(BF16)
            return _dot(pabuf[q][...], q1_ref[...])

        def retire(j, q, sums):
            for c, m, s, drc, g in chains(q):
                loc = (SUBLANES - 1 - j) if drc else j
                sl = slice(g * LANES, (g + 1) * LANES)
                r_b, w_b, _, b_b, k_b, _ = blk[q][m]
                sv = (st[q][rows(c), :] * w_b[loc:loc + 1, sl] + sums[rows(c)] * b_b[loc:loc + 1, sl]
                      + sums[rows(c, base=m_rows)] * k_b[loc:loc + 1, sl])
                st[q][rows(c), :] = sv
                obuf[q][rows(c), loc * LANES:(loc + 1) * LANES] = (sv * r_b[loc:loc + 1, sl]).astype(BF16)

        order = [(j, q) for j in range(SUBLANES) for q in range(len(groups))]
        inflight = []
        for slot, (j, q) in enumerate(order):
            inflight.append((j, q, issue(j, q)))
            if len(inflight) > RWKV_LAG:
                retire(*inflight.pop(0))
        for item in inflight:
            retire(*item)
        for q in range(len(groups)):
            o_all = _dot(obuf[q][...], wsp_ref[...])
            for c, m, s, drc, g in chains(q):
                fresh = (lane >= bases[drc]) & (lane < bases[drc] + SUBLANES)
                top, bot = slice(c * LANES, c * LANES + DH_D), slice(c * LANES + DH_D, (c + 1) * LANES)
                ot[q][top, :] = jnp.where(fresh, o_all[rows(c), 0:LANES], ot[q][top, :])
                ot[q][bot, :] = jnp.where(fresh, o_all[rows(c), LANES:2 * LANES], ot[q][bot, :])
        return carry

    lax.fori_loop(0, steps // SUBLANES, step_group, 0)

    for q in range(len(groups)):
        for c, m, s, drc, g in chains(q):
            o_refs[drc][s, :, g * LANES:(g + 1) * LANES] = ot[q][rows(c, LANES), :].T

    @pl.when(tb == pl.num_programs(1) - 1)
    def _():
        for q in range(len(groups)):
            for c, m, s, drc, g in chains(q):
                sfin_ref[s, drc, g] = st[q][rows(c), :]


def _rwkv_scan_call(r_src, w0, w1, nkk, bb, kt, s0, q1, wsp, *, n_seq, seq, row0):
    ns = RWKV_NS
    ntb = seq // RWKV_TB
    ngrp = D_D // LANES
    grp0 = row0 // (seq * ns)

    def view(t):
        return t.reshape(N_TOK // (seq * ns), ns, ntb, RWKV_TB, t.shape[-1])

    def tok(rev, cb=0):
        if rev:
            return pl.BlockSpec((None, ns, None, RWKV_TB, D_D), lambda b, t: (grp0 + b, 0, ntb - 1 - t, 0, cb))
        return pl.BlockSpec((None, ns, None, RWKV_TB, D_D), lambda b, t: (grp0 + b, 0, t, 0, cb))

    def out(rev):
        if rev:
            return pl.BlockSpec((None, ns, None, RWKV_TB, D_D), lambda b, t: (b, 0, ntb - 1 - t, 0, 0))
        return pl.BlockSpec((None, ns, None, RWKV_TB, D_D), lambda b, t: (b, 0, t, 0, 0))

    st_spec = pl.BlockSpec((ns, 2, ngrp, DH_D, LANES), lambda b, t: (b, 0, 0, 0, 0))
    in_specs = []
    for rev in (False, True):
        in_specs += [tok(rev, 0), tok(rev), tok(rev), tok(rev), tok(rev), tok(rev, 2)]
    in_specs += [st_spec, pl.BlockSpec((LANES, LANES), lambda b, t: (0, 0)),
                 pl.BlockSpec((SUBLANES * LANES, 2 * LANES), lambda b, t: (0, 0))]
    rv, w0v, w1v, nv, bv, kv = [view(t) for t in (r_src, w0, w1, nkk, bb, kt)]
    o_shape = jax.ShapeDtypeStruct((n_seq // ns, ns, ntb, RWKV_TB, D_D), F32)
    m_rows = RWKV_GROUP * ngrp * DH_D
    n_groups = ns * 2 // RWKV_GROUP
    scratch = ([pltpu.VMEM((m_rows, LANES), F32)] * n_groups + [pltpu.VMEM((2 * m_rows, LANES), F32)] * n_groups
               + [pltpu.VMEM((2 * m_rows, LANES), BF16)] * n_groups
               + [pltpu.VMEM((m_rows, SUBLANES * LANES), BF16)] * n_groups)
    o_f, o_b, s_fin = pl.pallas_call(
        _rwkv_scan_kernel,
        grid=(n_seq // ns, ntb),
        in_specs=in_specs,
        out_specs=[out(False), out(True), st_spec],
        out_shape=[o_shape, o_shape, jax.ShapeDtypeStruct((n_seq, 2, ngrp, DH_D, LANES), F32)],
        scratch_shapes=scratch,
        compiler_params=_params(("arbitrary", "arbitrary")),
        name="rwkv_dec" if row0 else "rwkv_ctx",
    )(rv, w0v, nv, bv, kv, rv, rv, w1v, nv, bv, kv, rv, s0, q1, wsp)
    return o_f.reshape(n_seq * seq, D_D), o_b.reshape(n_seq * seq, D_D), s_fin


def _rwkv_post_kernel(ofl_ref, ofh_ref, obl_ref, obh_ref, rkv_ref, g_ref, lng_ref, lnb_ref, gm_ref, o_ref, *, tm):
    gm = gm_ref[...]
    o_all = _pick(tm, ofl_ref, ofh_ref) + _pick(tm, obl_ref, obh_ref)
    for j in range(D_D // LANES):
        sl = slice(j * LANES, (j + 1) * LANES)
        o = o_all[:, sl]
        dlt = o - _dot_b16(o, gm)
        var = _dot_b16(dlt * dlt, gm)
        y = dlt * lax.rsqrt(var + RWKV_GN_EPS) * lng_ref[:, sl] + lnb_ref[:, sl]
        o_ref[:, sl] = (y + rkv_ref[:, sl]) * g_ref[:, sl]


def _rwkv_post_call(o_f, o_b, rkv, g, ln_g, ln_b, gm):
    tm = 512
    tok = pl.BlockSpec((tm, D_D), lambda i: (i, 0))
    row = pl.BlockSpec((1, D_D), lambda i: (0, 0))
    return pl.pallas_call(
        functools.partial(_rwkv_post_kernel, tm=tm),
        grid=(N_TOK // tm,),
        in_specs=_pair_specs(tm, D_D, o_f) + _pair_specs(tm, D_D, o_b)
        + [tok, tok, row, row, pl.BlockSpec((LANES, LANES), lambda i: (0, 0))],
        out_specs=tok,
        out_shape=jax.ShapeDtypeStruct((N_TOK, D_D), F32),
        compiler_params=_params(("arbitrary",)),
        name="rwkv_post",
    )(o_f[0], o_f[1], o_b[0], o_b[1], rkv, g, ln_g, ln_b, gm)


def _rope_tables():
    pos = np.arange(DEC_SEQ)
    pr, pc = pos // GRID_W, pos % GRID_W
    lane = np.arange(LANES)
    dd = lane % DH_A
    use_col = (dd // 32) == 1
    j = dd % 16
    is_lo = (dd % 32) < 16
    freq = ROPE_THETA ** (-(j.astype(np.float64)) / 16.0)
    p = np.where(use_col[None, :], pc[:, None], pr[:, None]).astype(np.float64)
    ang = (p.astype(np.float32) * freq.astype(np.float32)[None, :]).astype(np.float32)
    cos = np.cos(ang).astype(np.float32)
    sin = np.sin(ang).astype(np.float32)
    s1 = np.where(is_lo[None, :], -sin, 0.0).astype(np.float32)
    s2 = np.where(is_lo[None, :], 0.0, sin).astype(np.float32)
    ident = 512
    cos = np.concatenate([cos, np.ones((ident, LANES), np.float32)], 0)
    s1 = np.concatenate([s1, np.zeros((ident, LANES), np.float32)], 0)
    s2 = np.concatenate([s2, np.zeros((ident, LANES), np.float32)], 0)
    return jnp.asarray(cos), jnp.asarray(s1), jnp.asarray(s2)


def _block_diag(value):
    m = np.zeros((LANES, LANES), np.float32)
    half = LANES // 2
    m[:half, :half] = value
    m[half:, half:] = value
    return jnp.asarray(m)


def _head_spread():
    step = np.arange(SUBLANES * LANES) // LANES
    head = (np.arange(SUBLANES * LANES) % LANES) // DH_D
    out_head = np.arange(2 * LANES) // LANES
    out_step = (np.arange(2 * LANES) % LANES) % SUBLANES
    m = (head[:, None] == out_head[None, :]) & (step[:, None] == out_step[None, :])
    return jnp.asarray(m.astype(np.float32))


def _qb_perm():
    idx = np.zeros(D_B, np.int32)
    for j in range(HB // 2):
        for hh in range(2):
            for dch in range(DH_B):
                idx[j * LANES + hh * DH_B + dch] = (hh * (HB // 2) + j) * DH_B + dch
    return idx


def kernel(x_prompt, x_sample, cache_a_k, cache_a_v, cache_b_k, cache_b_v, state_hgrn, state_rwkv, c, c_ctx, ada_w, ada_b, norm_mix_g, norm_ffn_g, final_norm_g, ev_w_in, ev_w_out, a_lambda, a_subln_g, b_q_norm_g, b_k_norm_g, od_w_in, od_w_out, hgrn_lb_logits, hgrn_norm_g, rwkv_mu, rwkv_w0, rwkv_w_up, rwkv_a0, rwkv_a_up, rwkv_g_up, rwkv_k_k, rwkv_k_a, rwkv_r_k, rwkv_ln_g, rwkv_ln_b, ffn_w_up, ffn_conv_w, ffn_conv_b, ffn_w_down):
    d = D_MODEL
    x = (x_prompt.reshape(N_CTX, d), x_sample.reshape(N_DEC, d), 0)
    cond =jnp.concatenate([c_ctx[None, :], c, jnp.zeros((COND_ROWS - 1 - DEC_BATCH, d), F32)], axis=0)
    mod = _mod_call(cond, ada_w, ada_b).reshape(DEPTH, COND_ROWS, 1, 6 * d)

    cos, s1, s2 = _rope_tables()
    g_mean = _block_diag(1.0 / DH_B)
    g_sum = _block_diag(1.0)
    perm = _qb_perm()
    row = lambda t: t.reshape(1, -1)

    w_up_all = ffn_w_up.astype(BF16)
    w_down_all = ffn_w_down.astype(BF16)
    conv_b_all = ffn_conv_b.reshape(DEPTH, 1, 2 * D_FF)
    for l in range(DEPTH):
        i = l // 2
        mod_l = mod[l]
        if l % 2 == 0:
            w_in = ev_w_in[i]
            w_in = jnp.concatenate([w_in[:, :1536], w_in[:, 1536:2048][:, perm], w_in[:, 2048:]], axis=1).astype(BF16)
            w_out = ev_w_out[i]
            w_out = jnp.concatenate([w_out[:D_A], w_out[D_A:][perm]], axis=0).astype(BF16)
            qg = row(jnp.tile(b_q_norm_g[i], 2))
            kg = row(jnp.tile(b_k_norm_g[i], 2))
            qa, ka, va, qb, kb, vb, *even_ctx = _even_proj_call(x, mod_l, row(norm_mix_g[l]), w_in, qg, kg, cos, s1,
                                                                s2, g_mean)
            even_ctx = tuple(even_ctx)
            lam_init = 0.8 - 0.6 * math.exp(-0.3 * l)
            sub_g = row(a_subln_g[i])
            oa_c, ob_c = _attn_call(qa, qb, ka, va, kb, vb, None, a_lambda[i], sub_g, lam_init,
                                    n_seq=BATCH, seq=SEQ, row0=0)
            cbk = jnp.transpose(cache_b_k[:, i], (0, 2, 1, 3)).reshape(DEC_BATCH, PAST_LEN, HKV_B * DH_B)
            cbv = jnp.transpose(cache_b_v[:, i], (0, 2, 1, 3)).reshape(DEC_BATCH, PAST_LEN, HKV_B * DH_B)
            cache = (cache_a_k[:, i], cache_a_v[:, i], cbk, cbv)
            oa_s, ob_s = _attn_call(qa, qb, ka, va, kb, vb, cache, a_lambda[i], sub_g, lam_init,
                                    n_seq=DEC_BATCH, seq=DEC_SEQ, row0=N_CTX)
            mix_a = (oa_c, oa_s, 0)
            mix_b = (ob_c, ob_s, 0)
        else:
            w = od_w_in[i]
            w_h = w[:, :HGRN_PROJ].astype(BF16)
            wr = w[:, HGRN_PROJ:]
            mu = rwkv_mu[i]

            def rwkv_cols(t):
                z = jnp.zeros(t.shape[:-1] + (RWKV_PAD - RWKV_PROJ,), t.dtype)
                return jnp.concatenate([t[..., :1664], t[..., 1728:1856], t[..., 1664:1728], z], axis=-1)

            w_r = rwkv_cols(wr).astype(BF16)
            mu_r = row(rwkv_cols(mu))
            g_l = row(norm_mix_g[l])
            assert x[0] is x[1]
            proj_h = _odd_proj_call(x[0], mod_l, g_l, w_h, jnp.zeros((1, HGRN_PROJ), F32), shift=False)
            proj_r = _odd_proj_call(x[0], mod_l, g_l, w_r, mu_r, shift=True)

            ng = row(hgrn_norm_g[i])
            zero_h = jnp.zeros((BATCH, 2, HC, DK_C, DV_C), F32)
            ones = jnp.ones((LANES, LANES), BF16)
            oc_c, sh_c = _hgrn_call(proj_h, hgrn_lb_logits, ng, zero_h, ones, n_seq=BATCH, seq=SEQ, row0=0, lidx=l)
            oc_s, _ = _hgrn_call(proj_h, hgrn_lb_logits, ng, state_hgrn[:, i], ones, n_seq=DEC_BATCH, seq=DEC_SEQ,
                                 row0=N_CTX, lidx=l)

            pad_rows = lambda t: jnp.concatenate([t, jnp.zeros((LANES - t.shape[0], t.shape[1]), F32)], axis=0)
            wup0 = pad_rows(rwkv_w_up[i, 0])
            wup1 = jnp.concatenate([jnp.zeros((W_LORA, D_D), F32), rwkv_w_up[i, 1]], axis=0)
            w0o, w1o, nkk, bb, kt, gg, rkv = _rwkv_prep_call(
                proj_r, row(rwkv_a0[i]), pad_rows(rwkv_a_up[i]), rwkv_g_up[i], row(rwkv_k_k[i]), row(rwkv_k_a[i]),
                row(rwkv_r_k[i]), rwkv_w0[i], wup0, wup1, g_sum)

            def to_tiles(s):
                b = s.shape[0]
                s = s.reshape(b, 2, HD // 2, 2, DH_D, DH_D)
                return jnp.transpose(s, (0, 1, 2, 4, 3, 5)).reshape(b, 2, HD // 2, DH_D, LANES)

            def from_tiles(s):
                b = s.shape[0]
                s = s.reshape(b, 2, HD // 2, DH_D, 2, DH_D)
                return jnp.transpose(s, (0, 1, 2, 4, 3, 5)).reshape(b, 2, HD, DH_D, DH_D)

            zero_r = jnp.zeros((BATCH, 2, HD // 2, DH_D, LANES), F32)
            q1 = g_sum.astype(BF16)
            wsp = _head_spread().astype(BF16)
            of_c, ob_c, sr_c = _rwkv_scan_call(proj_r, w0o, w1o, nkk, bb, kt, zero_r, q1, wsp,
                                               n_seq=BATCH, seq=SEQ, row0=0)
            of_s, ob_s, _ = _rwkv_scan_call(proj_r, w0o, w1o, nkk, bb, kt, to_tiles(state_rwkv[:, i]), q1, wsp,
                                            n_seq=DEC_BATCH, seq=DEC_SEQ, row0=N_CTX)
            mix_b = _whole(_rwkv_post_call((of_c, of_s, 0), (ob_c, ob_s, 0), rkv, gg, row(rwkv_ln_g[i]),
                                           row(rwkv_ln_b[i]), g_mean))
            mix_a = (oc_c, oc_s, 0)
            w_out = od_w_out[i].astype(BF16)
            odd_ctx = (sh_c[:, None], from_tiles(sr_c)[:, None])

        x1 = _out_proj_call(x, mix_a, mix_b, mod_l, w_out)
        x2 = _ffn_call(x1, mod_l, row(norm_ffn_g[l]), w_up_all, ffn_conv_w, conv_b_all, w_down_all,
                       row(final_norm_g), layer=l, final=(l == DEPTH - 1))
        if l < DEPTH - 1:
            x = _whole(x2)

    y_prompt = x2[0].reshape(BATCH, SEQ, d)
    y_sample = x2[1].reshape(DEC_BATCH, DEC_SEQ, d)
    return (y_prompt, y_sample) + even_ctx + odd_ctx
```

```python
import functools
import math

import jax
import jax.numpy as jnp
import numpy as np
from jax import lax
from jax.experimental import pallas as pl
from jax.experimental.pallas import tpu as pltpu

D_MODEL = 1024
BATCH = 16
SEQ = 256
DEPTH = 2
DEC_BATCH = 4
DEC_SEQ = 1024
PAST_LEN = 512
GRID_W = 64
ROPE_THETA = 10000.0
EPS = 1e-6
RWKV_GN_EPS = 64e-5
HA = 4
DH_A = 64
DV_A = 2 * DH_A
HB = 8
HKV_B = 2
DH_B = 64
HC = 4
DK_C = 128
DV_C = 128
HD = 8
DH_D = 64
W_LORA = 64
A_LORA = 64
G_LORA = 128
D_FF = 2816

D_A = HA * DV_A
D_B = HB * DH_B
D_C = HC * DV_C
D_D = HD * DH_D
EVEN_PROJ = 2304
HGRN_PROJ = 2560
RWKV_PROJ = 1856
RWKV_PAD = 1920

N_CTX = BATCH * SEQ
N_DEC = DEC_BATCH * DEC_SEQ
N_TOK = N_CTX + N_DEC
COND_ROWS = 8

LANES = 128
SUBLANES = 8
VMEM_LIMIT = 56 * 1024 * 1024

F32 = jnp.float32
BF16 = jnp.bfloat16
HI = lax.Precision.HIGHEST

FF_TILE = 256
HGRN_L = 128
HGRN_HP = 2
RWKV_TB = 128
RWKV_NS = 4
RWKV_GROUP = 1
RWKV_LAG = 3
assert RWKV_LAG < 2 * RWKV_NS // RWKV_GROUP


def _dot(a, b):
    return jnp.dot(a, b, preferred_element_type=F32)


def _dot_hi(a, b):
    return jnp.dot(a, b, preferred_element_type=F32, precision=HI)


def _dot_b16(a, b):
    return _dot(a.astype(BF16), b.astype(BF16))


def _dot_nt(a, b, precision=None):
    return lax.dot_general(a, b, (((1,), (1,)), ((), ())), preferred_element_type=F32, precision=precision)


def _dot_tn(a, b, precision=None):
    return lax.dot_general(a, b, (((0,), (0,)), ((), ())), preferred_element_type=F32, precision=precision)


def _sigmoid(x):
    return 1.0 / (1.0 + jnp.exp(-x))


def _silu(x):
    return x * _sigmoid(x)


def _norm_mod(x, g, sc, sh):
    ms = jnp.mean(x * x, axis=-1, keepdims=True)
    return (x * lax.rsqrt(ms + EPS) * g) * (1.0 + sc) + sh


def _params(sem):
    return pltpu.CompilerParams(dimension_semantics=sem, vmem_limit_bytes=VMEM_LIMIT)


def _cond_row(i, tm):
    r0 = i * tm
    return jnp.where(r0 < N_CTX, 0, 1 + (r0 - N_CTX) // DEC_SEQ)


def _seq_len(i, tm):
    return jnp.where(i * tm < N_CTX, SEQ, DEC_SEQ)


ROW_PAD = SUBLANES


def _zero_row_pads(u_ref, tm):
    for k in range(u_ref.shape[0]):
        u_ref[k, 0:ROW_PAD, :] = jnp.zeros((ROW_PAD, u_ref.shape[2]), u_ref.dtype)
        u_ref[k, ROW_PAD + tm:2 * ROW_PAD + tm, :] = jnp.zeros((ROW_PAD, u_ref.shape[2]), u_ref.dtype)


def _neighbour_rows(u_ref, k, c, ch, seq_len):
    r0 = ROW_PAD + c * ch
    row8 = lax.broadcasted_iota(jnp.int32, (SUBLANES, 1), 0)
    at_start = (c * ch) % seq_len == 0
    at_end = ((c + 1) * ch) % seq_len == 0
    prev = u_ref[k, r0 - 1:r0 - 1 + ch, :]
    nxt = u_ref[k, r0 + 1:r0 + 1 + ch, :]
    prev = jnp.concatenate([jnp.where((row8 == 0) & at_start, 0.0, prev[0:SUBLANES]), prev[SUBLANES:]], axis=0)
    nxt = jnp.concatenate(
        [nxt[:ch - SUBLANES], jnp.where((row8 == SUBLANES - 1) & at_end, 0.0, nxt[ch - SUBLANES:])], axis=0)
    return prev, nxt


def _pick(tm, lo_ref, hi_ref, rows=slice(None)):
    return jnp.where(pl.program_id(0) * tm < N_CTX, lo_ref[rows, :], hi_ref[rows, :])


def _pair_specs(tm, width, pair):
    nct = N_CTX // tm
    hi0 = pair[2] // tm
    return [pl.BlockSpec((tm, width), lambda i: (jnp.minimum(i, nct - 1), 0)),
            pl.BlockSpec((tm, width), lambda i: (hi0 + jnp.maximum(i - nct, 0), 0))]


def _whole(t):
    return (t, t, N_CTX)


def _mod_kernel(c_ref, w_ref, b_ref, o_ref):
    s = _silu(c_ref[...]).astype(BF16)
    o_ref[...] = _dot(s, w_ref[...].astype(BF16)) + b_ref[...]


def _mod_call(cond, ada_w, ada_b):
    tn = 1536
    n = 6 * D_MODEL
    return pl.pallas_call(
        _mod_kernel,
        grid=(DEPTH, n // tn),
        in_specs=[
            pl.BlockSpec((COND_ROWS, D_MODEL), lambda l, j: (0, 0)),
            pl.BlockSpec((None, D_MODEL, tn), lambda l, j: (l, 0, j)),
            pl.BlockSpec((None, 1, tn), lambda l, j: (l, 0, j)),
        ],
        out_specs=pl.BlockSpec((None, COND_ROWS, tn), lambda l, j: (l, 0, j)),
        out_shape=jax.ShapeDtypeStruct((DEPTH, COND_ROWS, n), F32),
        compiler_params=_params(("arbitrary", "arbitrary")),
        name="ada_mod",
    )(cond, ada_w, ada_b.reshape(DEPTH, 1, n))


def _even_proj_kernel(xl_ref, xh_ref, mod_ref, g_ref, w_ref, qg_ref, kg_ref, cos_ref, s1_ref, s2_ref, gm_ref,
                      qa_ref, ka_ref, va_ref, qb_ref, kb_ref, vb_ref, cak_ref, cav_ref, cbk_ref, cbv_ref, *, tm):
    d = D_MODEL
    gm = gm_ref[...]
    scale = DH_A ** -0.5
    is_ctx = pl.program_id(0) * tm < N_CTX

    def head_norm(t, g):
        return t * lax.rsqrt(_dot_b16(t * t, gm) + EPS) * g

    for c in range(tm // SEQ):
        rows = slice(c * SEQ, (c + 1) * SEQ)
        h = _norm_mod(_pick(tm, xl_ref, xh_ref, rows), g_ref[...], mod_ref[:, d:2 * d], mod_ref[:, 0:d]).astype(BF16)
        proj = _dot(h, w_ref[...])
        cos, s1, s2 = cos_ref[rows, :], s1_ref[rows, :], s2_ref[rows, :]

        def rope(t):
            return t * cos + pltpu.roll(t, LANES - 16, 1) * s1 + pltpu.roll(t, 16, 1) * s2

        for j in range(4):
            sl = slice(j * LANES, (j + 1) * LANES)
            qa_ref[rows, sl] = rope(proj[:, j * LANES:(j + 1) * LANES]) * scale
            ka_ref[rows, sl] = rope(proj[:, 512 + j * LANES:512 + (j + 1) * LANES])
            va_ref[rows, sl] = proj[:, 1024 + j * LANES:1024 + (j + 1) * LANES]
            qb_ref[rows, sl] = rope(head_norm(proj[:, 1536 + j * LANES:1536 + (j + 1) * LANES], qg_ref[...])) * scale
        kb_ref[rows, :] = rope(head_norm(proj[:, 2048:2176], kg_ref[...]))
        vb_ref[rows, :] = proj[:, 2176:2304]

        @pl.when(is_ctx)
        def _(c=c, rows=rows):
            for hh in range(HA):
                sl = slice(hh * DV_A, (hh + 1) * DV_A)
                cak_ref[c, hh] = ka_ref[rows, sl]
                cav_ref[c, hh] = va_ref[rows, sl]
            for hh in range(HKV_B):
                sl = slice(hh * DH_B, (hh + 1) * DH_B)
                cbk_ref[c, hh] = kb_ref[rows, sl]
                cbv_ref[c, hh] = vb_ref[rows, sl]


def _even_proj_call(x, mod_l, g, w, qg, kg, cos, s1, s2, gm):
    tm = 512
    nt = N_TOK // tm
    n_rope_blk = DEC_SEQ // tm

    def rope_idx(i):
        return (jnp.where(i * tm < N_CTX, n_rope_blk, (i - N_CTX // tm) % n_rope_blk), 0)

    full = lambda shape: pl.BlockSpec(shape, lambda i: (0,) * len(shape))
    out512 = pl.BlockSpec((tm, 512), lambda i: (i, 0))
    out128 = pl.BlockSpec((tm, LANES), lambda i: (i, 0))
    last_ctx = N_CTX // tm - 1

    def cache(heads, width):
        return pl.BlockSpec((tm // SEQ, None, heads, SEQ, width), lambda i: (jnp.minimum(i, last_ctx), 0, 0, 0, 0))

    return pl.pallas_call(
        functools.partial(_even_proj_kernel, tm=tm),
        grid=(nt,),
        in_specs=_pair_specs(tm, D_MODEL, x) + [
            pl.BlockSpec((None, 1, 6 * D_MODEL), lambda i: (_cond_row(i, tm), 0, 0)),
            full((1, D_MODEL)),
            full((D_MODEL, EVEN_PROJ)),
            full((1, LANES)),
            full((1, LANES)),
            pl.BlockSpec((tm, LANES), rope_idx),
            pl.BlockSpec((tm, LANES), rope_idx),
            pl.BlockSpec((tm, LANES), rope_idx),
            full((LANES, LANES)),
        ],
        out_specs=[out512, out512, out512, out512, out128, out128, cache(HA, DV_A), cache(HA, DV_A),
                   cache(HKV_B, DH_B), cache(HKV_B, DH_B)],
        out_shape=[jax.ShapeDtypeStruct((N_TOK, 512), F32)] * 4 + [jax.ShapeDtypeStruct((N_TOK, LANES), F32)] * 2
        + [jax.ShapeDtypeStruct((BATCH, 1, HA, SEQ, DV_A), F32)] * 2
        + [jax.ShapeDtypeStruct((BATCH, 1, HKV_B, SEQ, DH_B), F32)] * 2,
        compiler_params=_params(("arbitrary",)),
        name="even_proj",
    )(x[0], x[1], mod_l, g, w, qg, kg, cos, s1, s2, gm)


def _softmax_pv(q, ks, vs):
    ss = [_dot_nt(q, k) for k in ks]
    m = functools.reduce(jnp.maximum, [jnp.max(s, axis=-1, keepdims=True) for s in ss])
    ps = [jnp.exp(s - m) for s in ss]
    l = functools.reduce(jnp.add, [jnp.sum(p, axis=-1, keepdims=True) for p in ps])
    acc = functools.reduce(jnp.add, [_dot(p.astype(BF16), v) for p, v in zip(ps, vs)])
    return acc / l


def _attn_kernel(*refs, has_cache, lam_init):
    if has_cache:
        (qa_ref, qb_ref, ka_ref, va_ref, kb_ref, vb_ref, cak_ref, cav_ref, cbk_ref, cbv_ref,
         al_ref, sg_ref, oa_ref, ob_ref) = refs
    else:
        qa_ref, qb_ref, ka_ref, va_ref, kb_ref, vb_ref, al_ref, sg_ref, oa_ref, ob_ref = refs
    al = al_ref[...]
    lam = (jnp.exp(jnp.sum(al[0:1] * al[1:2], axis=-1, keepdims=True))
           - jnp.exp(jnp.sum(al[2:3] * al[3:4], axis=-1, keepdims=True)) + lam_init)
    lo = lax.broadcasted_iota(jnp.int32, (1, LANES), 1) < DH_A

    for h in range(HA):
        sl = slice(h * LANES, (h + 1) * LANES)
        q = qa_ref[:, sl]
        ks = [ka_ref[:, sl].astype(BF16)]
        vs = [va_ref[:, sl].astype(BF16)]
        if has_cache:
            ks.insert(0, cak_ref[h].astype(BF16))
            vs.insert(0, cav_ref[h].astype(BF16))
        a1 = _softmax_pv(jnp.where(lo, q, 0.0).astype(BF16), ks, vs)
        a2 = _softmax_pv(jnp.where(lo, 0.0, q).astype(BF16), ks, vs)
        dlt = a1 - lam * a2
        ms = jnp.mean(dlt * dlt, axis=-1, keepdims=True)
        oa_ref[:, sl] = dlt * lax.rsqrt(ms + EPS) * sg_ref[...] * (1.0 - lam_init)

    ks = [kb_ref[...].astype(BF16)]
    vs = [vb_ref[...].astype(BF16)]
    if has_cache:
        ks.insert(0, cbk_ref[...].astype(BF16))
        vs.insert(0, cbv_ref[...].astype(BF16))
    for j in range(HB // 2):
        sl = slice(j * LANES, (j + 1) * LANES)
        q = qb_ref[:, sl]
        o0 = _softmax_pv(jnp.where(lo, q, 0.0).astype(BF16), ks, vs)
        o1 = _softmax_pv(jnp.where(lo, 0.0, q).astype(BF16), ks, vs)
        ob_ref[:, sl] = jnp.where(lo, o0, o1)


def _attn_call(qa, qb, ka, va, kb, vb, cache, a_lambda, subln_g, lam_init, *, n_seq, seq, row0):
    tq = 256
    nq = seq // tq
    qblk0 = row0 // tq
    sblk0 = row0 // seq
    has_cache = cache is not None
    qspec = pl.BlockSpec((tq, 512), lambda b, i: (qblk0 + b * nq + i, 0))
    own512 = pl.BlockSpec((seq, 512), lambda b, i: (sblk0 + b, 0))
    own128 = pl.BlockSpec((seq, LANES), lambda b, i: (sblk0 + b, 0))
    in_specs = [qspec, qspec, own512, own512, own128, own128]
    args = [qa, qb, ka, va, kb, vb]
    if has_cache:
        in_specs += [
            pl.BlockSpec((None, HA, PAST_LEN, LANES), lambda b, i: (b, 0, 0, 0)),
            pl.BlockSpec((None, HA, PAST_LEN, LANES), lambda b, i: (b, 0, 0, 0)),
            pl.BlockSpec((None, PAST_LEN, LANES), lambda b, i: (b, 0, 0)),
            pl.BlockSpec((None, PAST_LEN, LANES), lambda b, i: (b, 0, 0)),
        ]
        args += list(cache)
    in_specs += [pl.BlockSpec((4, DH_A), lambda b, i: (0, 0)), pl.BlockSpec((1, LANES), lambda b, i: (0, 0))]
    args += [a_lambda, subln_g]
    ospec = pl.BlockSpec((tq, 512), lambda b, i: (b * nq + i, 0))
    return pl.pallas_call(
        functools.partial(_attn_kernel, has_cache=has_cache, lam_init=lam_init),
        grid=(n_seq, nq),
        in_specs=in_specs,
        out_specs=[ospec, ospec],
        out_shape=[jax.ShapeDtypeStruct((n_seq * seq, 512), F32)] * 2,
        compiler_params=_params(("arbitrary", "arbitrary")),
        name="attn_dec" if has_cache else "attn_ctx",
    )(*args)


def _out_proj_kernel(xl_ref, xh_ref, al_ref, ah_ref, bl_ref, bh_ref, mod_ref, w_ref, o_ref, *, tm):
    d = D_MODEL
    half = al_ref.shape[1]
    a = _pick(tm, al_ref, ah_ref).astype(BF16)
    b = _pick(tm, bl_ref, bh_ref).astype(BF16)
    mix = _dot(a, w_ref[0:half, :]) + _dot(b, w_ref[half:2 * half, :])
    o_ref[...] = _pick(tm, xl_ref, xh_ref) + mod_ref[:, 2 * d:3 * d] * mix


def _out_proj_call(x, a, b, mod_l, w):
    tm = 512
    return pl.pallas_call(
        functools.partial(_out_proj_kernel, tm=tm),
        grid=(N_TOK // tm,),
        in_specs=_pair_specs(tm, D_MODEL, x) + _pair_specs(tm, 512, a) + _pair_specs(tm, 512, b) + [
            pl.BlockSpec((None, 1, 6 * D_MODEL), lambda i: (_cond_row(i, tm), 0, 0)),
            pl.BlockSpec((D_MODEL, D_MODEL), lambda i: (0, 0)),
        ],
        out_specs=pl.BlockSpec((tm, D_MODEL), lambda i: (i, 0)),
        out_shape=jax.ShapeDtypeStruct((N_TOK, D_MODEL), F32),
        compiler_params=_params(("arbitrary",)),
        name="out_proj",
    )(x[0], x[1], a[0], a[1], b[0], b[1], mod_l, w)


def _ffn_kernel(x_ref, mod_ref, g_ref, wv_ref, wg_ref, cwv_ref, cwg_ref, cbv_ref, cbg_ref, wd_ref, fg_ref,
                *rest, tm, final):
    if final:
        oc_ref, od_ref, h_scr, acc_scr, u_scr = rest
    else:
        o_ref, h_scr, acc_scr, u_scr = rest
    d = D_MODEL
    i, j = pl.program_id(0), pl.program_id(1)
    ch = SEQ

    @pl.when(j == 0)
    def _():
        h_scr[...] = _norm_mod(x_ref[...], g_ref[...], mod_ref[:, 4 * d:5 * d], mod_ref[:, 3 * d:4 * d]).astype(BF16)
        acc_scr[...] = jnp.zeros_like(acc_scr)
        _zero_row_pads(u_scr, tm)

    seq_len = _seq_len(i, tm)

    def project(c):
        rows = slice(c * ch, (c + 1) * ch)
        u_scr[0, ROW_PAD + c * ch:ROW_PAD + (c + 1) * ch, :] = _dot(h_scr[rows, :], wv_ref[...])
        u_scr[1, ROW_PAD + c * ch:ROW_PAD + (c + 1) * ch, :] = _dot(h_scr[rows, :], wg_ref[...])

    def conv(k, c, cw_ref, cb_ref):
        prev, nxt = _neighbour_rows(u_scr, k, c, ch, seq_len)
        cur = u_scr[k, ROW_PAD + c * ch:ROW_PAD + (c + 1) * ch, :]
        return cw_ref[0:1, :] * prev + cw_ref[1:2, :] * cur + cw_ref[2:3, :] * nxt + cb_ref[...]

    def activate(c):
        val = conv(0, c, cwv_ref, cbv_ref)
        gate = conv(1, c, cwg_ref, cbg_ref)
        act = (_silu(gate) * val).astype(BF16)
        acc_scr[c * ch:(c + 1) * ch, :] += _dot(act, wd_ref[...])

    n_chunks = tm // ch
    project(0)
    for c in range(n_chunks):
        if c + 1 < n_chunks:
            project(c + 1)
        activate(c)

    @pl.when(j == pl.num_programs(1) - 1)
    def _():
        y = x_ref[...] + mod_ref[:, 5 * d:6 * d] * acc_scr[...]
        if not final:
            o_ref[...] = y
            return
        ms = jnp.mean(y * y, axis=-1, keepdims=True)
        y = y * lax.rsqrt(ms + EPS) * fg_ref[...]

        @pl.when(i * tm < N_CTX)
        def _():
            oc_ref[...] = y

        @pl.when(i * tm >= N_CTX)
        def _():
            od_ref[...] = y


def _ffn_call(x, mod_l, g, w_up, conv_w, conv_b, w_down, final_g, *, layer, final):
    tm = DEC_SEQ
    nf = D_FF // FF_TILE
    nct = N_CTX // tm
    if final:
        out_specs = [pl.BlockSpec((tm, D_MODEL), lambda i, j: (jnp.minimum(i, nct - 1), 0)),
                     pl.BlockSpec((tm, D_MODEL), lambda i, j: (jnp.maximum(i - nct, 0), 0))]
        out_shape = [jax.ShapeDtypeStruct((N_CTX, D_MODEL), F32), jax.ShapeDtypeStruct((N_DEC, D_MODEL), F32)]
    else:
        out_specs = pl.BlockSpec((tm, D_MODEL), lambda i, j: (i, 0))
        out_shape = jax.ShapeDtypeStruct((N_TOK, D_MODEL), F32)
    return pl.pallas_call(
        functools.partial(_ffn_kernel, tm=tm, final=final),
        grid=(N_TOK // tm, nf),
        in_specs=[
            pl.BlockSpec((tm, D_MODEL), lambda i, j: (i, 0)),
            pl.BlockSpec((None, 1, 6 * D_MODEL), lambda i, j: (_cond_row(i, tm), 0, 0)),
            pl.BlockSpec((1, D_MODEL), lambda i, j: (0, 0)),
            pl.BlockSpec((None, D_MODEL, FF_TILE), lambda i, j: (layer, 0, j)),
            pl.BlockSpec((None, D_MODEL, FF_TILE), lambda i, j: (layer, 0, nf + j)),
            pl.BlockSpec((None, 3, FF_TILE), lambda i, j: (layer, 0, j)),
            pl.BlockSpec((None, 3, FF_TILE), lambda i, j: (layer, 0, nf + j)),
            pl.BlockSpec((None, 1, FF_TILE), lambda i, j: (layer, 0, j)),
            pl.BlockSpec((None, 1, FF_TILE), lambda i, j: (layer, 0, nf + j)),
            pl.BlockSpec((None, FF_TILE, D_MODEL), lambda i, j: (layer, j, 0)),
            pl.BlockSpec((1, D_MODEL), lambda i, j: (0, 0)),
        ],
        out_specs=out_specs,
        out_shape=out_shape,
        scratch_shapes=[pltpu.VMEM((tm, D_MODEL), BF16), pltpu.VMEM((tm, D_MODEL), F32),
                        pltpu.VMEM((2, tm + 2 * SUBLANES, FF_TILE), F32)],
        compiler_params=_params(("arbitrary", "arbitrary")),
        name="conv_ffn",
    )(x, mod_l, g, w_up, w_up, conv_w, conv_w, conv_b, conv_b, w_down, final_g)


def _odd_proj_kernel(x_ref, mod_ref, g_ref, w_ref, mu_ref, o_ref, h_scr, *u_scr, tm, shift):
    d = D_MODEL
    i, j = pl.program_id(0), pl.program_id(1)
    ch = SEQ
    n_chunks = tm // ch

    @pl.when(j == 0)
    def _():
        h_scr[...] = _norm_mod(x_ref[...], g_ref[...], mod_ref[:, d:2 * d], mod_ref[:, 0:d]).astype(BF16)
        if shift:
            _zero_row_pads(u_scr[0], tm)

    if not shift:
        for c in range(n_chunks):
            o_ref[c * ch:(c + 1) * ch, :] = _dot(h_scr[c * ch:(c + 1) * ch, :], w_ref[...])
        return

    u = u_scr[0]
    seq_len = _seq_len(i, tm)

    def project(c):
        u[0, ROW_PAD + c * ch:ROW_PAD + (c + 1) * ch, :] = _dot(h_scr[c * ch:(c + 1) * ch, :], w_ref[...])

    project(0)
    for c in range(n_chunks):
        if c + 1 < n_chunks:
            project(c + 1)
        prev, nxt = _neighbour_rows(u, 0, c, ch, seq_len)
        p = u[0, ROW_PAD + c * ch:ROW_PAD + (c + 1) * ch, :]
        o_ref[c * ch:(c + 1) * ch, :] = p + mu_ref[...] * (0.5 * (prev + nxt) - p)


def _odd_proj_call(x, mod_l, g, w, mu, *, shift):
    tm = DEC_SEQ
    n = w.shape[1]
    tn = n if shift else n // 2
    return pl.pallas_call(
        functools.partial(_odd_proj_kernel, tm=tm, shift=shift),
        grid=(N_TOK // tm, n // tn),
        in_specs=[
            pl.BlockSpec((tm, D_MODEL), lambda i, j: (i, 0)),
            pl.BlockSpec((None, 1, 6 * D_MODEL), lambda i, j: (_cond_row(i, tm), 0, 0)),
            pl.BlockSpec((1, D_MODEL), lambda i, j: (0, 0)),
            pl.BlockSpec((D_MODEL, tn), lambda i, j: (0, j)),
            pl.BlockSpec((1, tn), lambda i, j: (0, j)),
        ],
        out_specs=pl.BlockSpec((tm, tn), lambda i, j: (i, j)),
        out_shape=jax.ShapeDtypeStruct((N_TOK, n), F32),
        scratch_shapes=[pltpu.VMEM((tm, D_MODEL), BF16)]
        + ([pltpu.VMEM((1, tm + 2 * ROW_PAD, tn), F32)] if shift else []),
        compiler_params=_params(("arbitrary", "arbitrary")),
        name="odd_proj_rwkv" if shift else "odd_proj_hgrn",
    )(x, mod_l, g, w, mu)


def _hgrn_kernel(q_ref, ff_ref, fb_ref, v_ref, gc_ref, lbl_ref, ng_ref, s0_ref, ones_ref, o_ref, sfin_ref,
                 oacc_f, oacc_b, st_scr, qd_scr, add_scr, dec_scr, start_scr, *, seq, lidx):
    L = HGRN_L
    nc = seq // L
    nb = L // SUBLANES
    ones = ones_ref[...]
    rowi = lax.broadcasted_iota(jnp.int32, (L, L), 0)
    coli = lax.broadcasted_iota(jnp.int32, (L, L), 1)
    rowv = lax.broadcasted_iota(jnp.int32, (L, 1), 0)

    def lower_bound(drc):
        lg = lbl_ref[drc]
        e = jnp.exp(lg - jnp.max(lg, axis=0, keepdims=True))
        sm = e / jnp.sum(e, axis=0, keepdims=True)
        return functools.reduce(jnp.add, [sm[i:i + 1, :] for i in range(1, lidx + 1)])

    def spread(t, s_l):
        return jnp.concatenate(
            [jnp.broadcast_to(t[b * SUBLANES + s_l:b * SUBLANES + s_l + 1, :], (SUBLANES, LANES)) for b in range(nb)],
            axis=0)

    def chunk(rev, r0, f_ref, lb, tri, hs):
        q = _silu(q_ref[pl.ds(r0, L), hs])
        f = lb + (1.0 - lb) * _sigmoid(f_ref[pl.ds(r0, L), hs])
        k = 1.0 - f
        lf = jnp.log(f)
        v = v_ref[pl.ds(r0, L), hs].astype(BF16)
        cum = _dot_hi(tri, lf)
        cum_ex = cum - lf
        q_dec = (q * jnp.exp(cum)).astype(BF16)

        scores = jnp.zeros((L, L), F32)
        sub = rowv % SUBLANES
        h = L // 2
        while h >= 1:
            if h >= SUBLANES:
                pieces = []
                for a in range(0, L, 2 * h):
                    edge = a + h - 1 if rev else a + h
                    pieces.append(jnp.broadcast_to(cum_ex[edge:edge + 1, :], (2 * h, LANES)))
                anchor = jnp.concatenate(pieces, axis=0) if len(pieces) > 1 else pieces[0]
            else:
                anchor = None
                for a in range(0, SUBLANES, 2 * h):
                    edge = a + h - 1 if rev else a + h
                    cand = spread(cum_ex, edge)
                    anchor = cand if anchor is None else jnp.where(sub >= a, cand, anchor)
            is_q = ((rowv // h) % 2) == (0 if rev else 1)
            d = cum - anchor
            e = jnp.exp(jnp.where(is_q, d, -d))
            qe = jnp.where(is_q, q * e, 0.0).astype(BF16)
            ke = jnp.where(is_q, 0.0, k * e).astype(BF16)
            same_pair = (rowi // (2 * h)) == (coli // (2 * h))
            scores = scores + jnp.where(same_pair, _dot_nt(qe, ke), 0.0)
            h //= 2

        own = _dot((q * k).astype(BF16), ones)
        own = own[:, 0:L] if L <= LANES else jnp.concatenate([own] * (L // LANES), axis=1)
        scores = jnp.where(rowi == coli, own, scores)
        o = _dot(scores.astype(BF16), v)

        end = 0 if rev else L - 1
        cend = cum[end:end + 1, :]
        kd = (k * jnp.exp(cend - cum)).astype(BF16)
        return q_dec, o, _dot_tn(v, kd), jnp.exp(cend)

    lb_f, lb_b = lower_bound(0), lower_bound(1)
    tri_f = jnp.where(coli <= rowi, 1.0, 0.0).astype(F32)
    tri_b = jnp.where(coli >= rowi, 1.0, 0.0).astype(F32)
    heads = [slice(hh * LANES, (hh + 1) * LANES) for hh in range(HGRN_HP)]
    chains = [(hh, hs, drc) for hh, hs in enumerate(heads) for drc in range(2)]
    oaccs = (oacc_f, oacc_b)

    def local(ci, carry):
        r0 = pl.multiple_of(ci * L, L)
        for c, (hh, hs, drc) in enumerate(chains):
            q_dec, o, add, dec = chunk(drc == 1, r0, fb_ref if drc else ff_ref, (lb_b if drc else lb_f)[:, hs],
                                       tri_b if drc else tri_f, hs)
            qd_scr[c, pl.ds(r0, L), :] = q_dec
            oaccs[drc][pl.ds(r0, L), hs] = o
            add_scr[c * nc + ci] = add
            dec_scr[c * nc + ci] = jnp.broadcast_to(dec, (SUBLANES, LANES))
        return carry

    lax.fori_loop(0, nc, local, 0, unroll=min(2, nc))

    for c, (hh, hs, drc) in enumerate(chains):
        st_scr[c] = s0_ref[drc, hh].T

    def scan(n, carry):
        for c, (hh, hs, drc) in enumerate(chains):
            slot = c * nc + ((nc - 1 - n) if drc else n)
            st = st_scr[c]
            start_scr[slot] = st.astype(BF16)
            st_scr[c] = st * dec_scr[slot][0:1, :] + add_scr[slot]
        return carry

    lax.fori_loop(0, nc, scan, 0)

    def carried(ci, carry):
        r0 = pl.multiple_of(ci * L, L)
        for c, (hh, hs, drc) in enumerate(chains):
            oaccs[drc][pl.ds(r0, L), hs] += _dot_nt(qd_scr[c, pl.ds(r0, L), :], start_scr[c * nc + ci])
        return carry

    lax.fori_loop(0, nc, carried, 0)

    for hh, hs in enumerate(heads):
        sfin_ref[0, hh] = st_scr[2 * hh].T
        sfin_ref[1, hh] = st_scr[2 * hh + 1].T
        o = oacc_f[:, hs] + oacc_b[:, hs]
        ms = jnp.mean(o * o, axis=-1, keepdims=True)
        o_ref[:, hs] = o * lax.rsqrt(ms + EPS) * ng_ref[...] * _silu(gc_ref[:, hs])


def _hgrn_call(proj_h, lb_logits, norm_g, s0, ones, *, n_seq, seq, row0, lidx):
    sblk0 = row0 // seq

    hp = HGRN_HP
    width = hp * LANES

    def col(section):
        return pl.BlockSpec((seq, width), lambda b, h: (sblk0 + b, section * (HC // hp) + h))

    st_spec = pl.BlockSpec((None, 2, hp, DK_C, DV_C), lambda b, h: (b, 0, h, 0, 0))
    return pl.pallas_call(
        functools.partial(_hgrn_kernel, seq=seq, lidx=lidx),
        grid=(n_seq, HC // hp),
        in_specs=[
            col(0), col(1), col(2), col(3), col(4),
            pl.BlockSpec((2, DEPTH, width), lambda b, h: (0, 0, h)),
            pl.BlockSpec((1, LANES), lambda b, h: (0, 0)),
            st_spec,
            pl.BlockSpec((LANES, LANES), lambda b, h: (0, 0)),
        ],
        out_specs=[pl.BlockSpec((seq, width), lambda b, h: (b, h)), st_spec],
        out_shape=[jax.ShapeDtypeStruct((n_seq * seq, D_C), F32),
                   jax.ShapeDtypeStruct((n_seq, 2, HC, DK_C, DV_C), F32)],
        scratch_shapes=[pltpu.VMEM((seq, width), F32), pltpu.VMEM((seq, width), F32),
                        pltpu.VMEM((2 * hp, DV_C, DK_C), F32),
                        pltpu.VMEM((2 * hp, seq, DK_C), BF16),
                        pltpu.VMEM((2 * hp * (seq // HGRN_L), DV_C, DK_C), F32),
                        pltpu.VMEM((2 * hp * (seq // HGRN_L), SUBLANES, DK_C), F32),
                        pltpu.VMEM((2 * hp * (seq // HGRN_L), DV_C, DK_C), BF16)],
        compiler_params=_params(("arbitrary", "arbitrary")),
        name="hgrn_dec" if row0 else "hgrn_ctx",
    )(proj_h, proj_h, proj_h, proj_h, proj_h, lb_logits, norm_g, s0, ones)


def _rwkv_prep_kernel(p_ref, a0_ref, aup_ref, gup_ref, kk_ref, ka_ref, rk_ref, w0_ref, wup0_ref, wup1_ref,
                      gs_ref, w0o_ref, w1o_ref, nkk_ref, bb_ref, kt_ref, g_ref, rkv_ref):
    r = p_ref[:, 0:512]
    k = p_ref[:, 512:1024]
    v = p_ref[:, 1024:1536]
    wd = p_ref[:, 1536:1664]
    gd = p_ref[:, 1664:1792]
    ad = p_ref[:, 1792:1920]
    gs = gs_ref[...]

    def group_sum(t):
        return jnp.concatenate([_dot_b16(t[:, j * LANES:(j + 1) * LANES], gs) for j in range(4)], axis=1)

    a = _sigmoid(a0_ref[...] + _dot_b16(ad, aup_ref[...]))
    g_ref[...] = _dot_b16(_sigmoid(gd), gup_ref[...])
    kkr = k * kk_ref[...]
    kk = kkr / jnp.maximum(jnp.sqrt(group_sum(kkr * kkr)), 1e-12)
    kt = k * (1.0 + (a - 1.0) * ka_ref[...])
    th = jnp.tanh(wd)
    decay = math.exp(-0.5)
    w0o_ref[...] = jnp.exp(-decay * _sigmoid(w0_ref[0:1, :] + _dot_b16(th, wup0_ref[...])))
    w1o_ref[...] = jnp.exp(-decay * _sigmoid(w0_ref[1:2, :] + _dot_b16(th, wup1_ref[...])))
    nkk_ref[...] = -kk
    bb_ref[...] = kk * a
    kt_ref[...] = kt
    rkv_ref[...] = group_sum(r * kt * rk_ref[...]) * v


def _rwkv_prep_call(proj_r, a0, aup, gup, kk_k, k_a, r_k, w0, wup0, wup1, gs):
    tm = 512
    full = lambda shape: pl.BlockSpec(shape, lambda i: (0,) * len(shape))
    ospec = pl.BlockSpec((tm, D_D), lambda i: (i, 0))
    return pl.pallas_call(
        _rwkv_prep_kernel,
        grid=(N_TOK // tm,),
        in_specs=[
            pl.BlockSpec((tm, RWKV_PAD), lambda i: (i, 0)),
            full((1, D_D)), full((LANES, D_D)), full((LANES, D_D)), full((1, D_D)), full((1, D_D)), full((1, D_D)),
            full((2, D_D)), full((LANES, D_D)), full((LANES, D_D)), full((LANES, LANES)),
        ],
        out_specs=[ospec] * 7,
        out_shape=[jax.ShapeDtypeStruct((N_TOK, D_D), F32)] * 7,
        compiler_params=_params(("arbitrary",)),
        name="rwkv_prep",
    )(proj_r, a0, aup, gup, kk_k, k_a, r_k, w0, wup0, wup1, gs)


def _rwkv_scan_kernel(rf_ref, wf_ref, nf_ref, bf_ref, kf_ref, vf_ref, rb_ref, wb_ref, nb_ref, bb_ref, kb_ref,
                      vb_ref, s0_ref, q1_ref, wsp_ref, of_ref, ob_ref, sfin_ref, *scratch):
    tb = pl.program_id(1)
    ns = RWKV_NS
    ngrp = D_D // LANES
    steps = RWKV_TB
    members = [(s, drc) for s in range(ns) for drc in range(2)]
    groups = [members[i:i + RWKV_GROUP] for i in range(0, len(members), RWKV_GROUP)]
    st, ot, pabuf, obuf = (scratch[i * len(groups):(i + 1) * len(groups)] for i in range(4))
    m_rows = RWKV_GROUP * ngrp * DH_D
    dirs = ((rf_ref, wf_ref, nf_ref, bf_ref, kf_ref, vf_ref), (rb_ref, wb_ref, nb_ref, bb_ref, kb_ref, vb_ref))
    o_refs = (of_ref, ob_ref)

    def rows(c, n=DH_D, base=0):
        return slice(base + c * n, base + (c + 1) * n)

    def chains(q):
        return [(m * ngrp + g, m, s, drc, g) for m, (s, drc) in enumerate(groups[q]) for g in range(ngrp)]

    @pl.when(tb == 0)
    def _():
        for q in range(len(groups)):
            for c, _, s, drc, g in chains(q):
                st[q][rows(c), :] = s0_ref[s, drc, g]

    for q in range(len(groups)):
        ot[q][...] = jnp.zeros_like(ot[q])
    lane = lax.broadcasted_iota(jnp.int32, (1, LANES), 1)
    diag = (lax.broadcasted_iota(jnp.int32, (DH_D, LANES), 1) & (DH_D - 1)) == lax.broadcasted_iota(
        jnp.int32, (DH_D, LANES), 0)

    def step_group(t8, carry):
        bases = (pl.multiple_of(t8 * SUBLANES, SUBLANES), pl.multiple_of(steps - (t8 + 1) * SUBLANES, SUBLANES))
        blk = [[[ref[s, pl.ds(bases[drc], SUBLANES), :] for ref in dirs[drc]] for s, drc in grp] for grp in groups]

        def issue(j, q):
            for c, m, s, drc, g in chains(q):
                loc = (SUBLANES - 1 - j) if drc else j
                sl = slice(g * LANES, (g + 1) * LANES)
                n_b, v_b = blk[q][m][2], blk[q][m][5]
                pabuf[q][rows(c), :] = (st[q][rows(c), :] * n_b[loc:loc + 1, sl]).astype(BF16)
                pabuf[q][rows(c, base=m_rows), :] = jnp.where(diag, v_b[loc:loc + 1, sl], 0.0).astype(BF16)
            return _dot(pabuf[q][...], q1_ref[...])

        def retire(j, q, sums):
            for c, m, s, drc, g in chains(q):
                loc = (SUBLANES - 1 - j) if drc else j
                sl = slice(g * LANES, (g + 1) * LANES)
                r_b, w_b, _, b_b, k_b, _ = blk[q][m]
                sv = (st[q][rows(c), :] * w_b[loc:loc + 1, sl] + sums[rows(c)] * b_b[loc:loc + 1, sl]
                      + sums[rows(c, base=m_rows)] * k_b[loc:loc + 1, sl])
                st[q][rows(c), :] = sv
                obuf[q][rows(c), loc * LANES:(loc + 1) * LANES] = (sv * r_b[loc:loc + 1, sl]).astype(BF16)

        order = [(j, q) for j in range(SUBLANES) for q in range(len(groups))]
        inflight = []
        for slot, (j, q) in enumerate(order):
            inflight.append((j, q, issue(j, q)))
            if len(inflight) > RWKV_LAG:
                retire(*inflight.pop(0))
        for item in inflight:
            retire(*item)
        for q in range(len(groups)):
            o_all = _dot(obuf[q][...], wsp_ref[...])
            for c, m, s, drc, g in chains(q):
                fresh = (lane >= bases[drc]) & (lane < bases[drc] + SUBLANES)
                top, bot = slice(c * LANES, c * LANES + DH_D), slice(c * LANES + DH_D, (c + 1) * LANES)
                ot[q][top, :] = jnp.where(fresh, o_all[rows(c), 0:LANES], ot[q][top, :])
                ot[q][bot, :] = jnp.where(fresh, o_all[rows(c), LANES:2 * LANES], ot[q][bot, :])
        return carry

    lax.fori_loop(0, steps // SUBLANES, step_group, 0)

    for q in range(len(groups)):
        for c, m, s, drc, g in chains(q):
            o_refs[drc][s, :, g * LANES:(g + 1) * LANES] = ot[q][rows(c, LANES), :].T

    @pl.when(tb == pl.num_programs(1) - 1)
    def _():
        for q in range(len(groups)):
            for c, m, s, drc, g in chains(q):
                sfin_ref[s, drc, g] = st[q][rows(c), :]


def _rwkv_scan_call(r_src, w0, w1, nkk, bb, kt, s0, q1, wsp, *, n_seq, seq, row0):
    ns = RWKV_NS
    ntb = seq // RWKV_TB
    ngrp = D_D // LANES
    grp0 = row0 // (seq * ns)

    def view(t):
        return t.reshape(N_TOK // (seq * ns), ns, ntb, RWKV_TB, t.shape[-1])

    def tok(rev, cb=0):
        if rev:
            return pl.BlockSpec((None, ns, None, RWKV_TB, D_D), lambda b, t: (grp0 + b, 0, ntb - 1 - t, 0, cb))
        return pl.BlockSpec((None, ns, None, RWKV_TB, D_D), lambda b, t: (grp0 + b, 0, t, 0, cb))

    def out(rev):
        if rev:
            return pl.BlockSpec((None, ns, None, RWKV_TB, D_D), lambda b, t: (b, 0, ntb - 1 - t, 0, 0))
        return pl.BlockSpec((None, ns, None, RWKV_TB, D_D), lambda b, t: (b, 0, t, 0, 0))

    st_spec = pl.BlockSpec((ns, 2, ngrp, DH_D, LANES), lambda b, t: (b, 0, 0, 0, 0))
    in_specs = []
    for rev in (False, True):
        in_specs += [tok(rev, 0), tok(rev), tok(rev), tok(rev), tok(rev), tok(rev, 2)]
    in_specs += [st_spec, pl.BlockSpec((LANES, LANES), lambda b, t: (0, 0)),
                 pl.BlockSpec((SUBLANES * LANES, 2 * LANES), lambda b, t: (0, 0))]
    rv, w0v, w1v, nv, bv, kv = [view(t) for t in (r_src, w0, w1, nkk, bb, kt)]
    o_shape = jax.ShapeDtypeStruct((n_seq // ns, ns, ntb, RWKV_TB, D_D), F32)
    m_rows = RWKV_GROUP * ngrp * DH_D
    n_groups = ns * 2 // RWKV_GROUP
    scratch = ([pltpu.VMEM((m_rows, LANES), F32)] * n_groups + [pltpu.VMEM((2 * m_rows, LANES), F32)] * n_groups
               + [pltpu.VMEM((2 * m_rows, LANES), BF16)] * n_groups
               + [pltpu.VMEM((m_rows, SUBLANES * LANES), BF16)] * n_groups)
    o_f, o_b, s_fin = pl.pallas_call(
        _rwkv_scan_kernel,
        grid=(n_seq // ns, ntb),
        in_specs=in_specs,
        out_specs=[out(False), out(True), st_spec],
        out_shape=[o_shape, o_shape, jax.ShapeDtypeStruct((n_seq, 2, ngrp, DH_D, LANES), F32)],
        scratch_shapes=scratch,
        compiler_params=_params(("arbitrary", "arbitrary")),
        name="rwkv_dec" if row0 else "rwkv_ctx",
    )(rv, w0v, nv, bv, kv, rv, rv, w1v, nv, bv, kv, rv, s0, q1, wsp)
    return o_f.reshape(n_seq * seq, D_D), o_b.reshape(n_seq * seq, D_D), s_fin


def _odd_out_kernel(xl_ref, xh_ref, al_ref, ah_ref, ofl_ref, ofh_ref, obl_ref, obh_ref, rkv_ref, g_ref, lng_ref,
                    lnb_ref, gm_ref, mod_ref, w_ref, o_ref, *, tm):
    d = D_MODEL
    gm = gm_ref[...]
    o_all = _pick(tm, ofl_ref, ofh_ref) + _pick(tm, obl_ref, obh_ref)
    parts = []
    for j in range(D_D // LANES):
        sl = slice(j * LANES, (j + 1) * LANES)
        o = o_all[:, sl]
        dlt = o - _dot_b16(o, gm)
        var = _dot_b16(dlt * dlt, gm)
        y = dlt * lax.rsqrt(var + RWKV_GN_EPS) * lng_ref[:, sl] + lnb_ref[:, sl]
        parts.append(((y + rkv_ref[:, sl]) * g_ref[:, sl]).astype(BF16))
    a = _pick(tm, al_ref, ah_ref).astype(BF16)
    mix = _dot(a, w_ref[0:D_C, :]) + _dot(jnp.concatenate(parts, axis=1), w_ref[D_C:D_C + D_D, :])
    o_ref[...] = _pick(tm, xl_ref, xh_ref) + mod_ref[:, 2 * d:3 * d] * mix


def _odd_out_call(x, a, o_f, o_b, rkv, g, ln_g, ln_b, gm, mod_l, w):
    tm = 512
    tok = pl.BlockSpec((tm, D_D), lambda i: (i, 0))
    row = pl.BlockSpec((1, D_D), lambda i: (0, 0))
    return pl.pallas_call(
        functools.partial(_odd_out_kernel, tm=tm),
        grid=(N_TOK // tm,),
        in_specs=_pair_specs(tm, D_MODEL, x) + _pair_specs(tm, D_C, a) + _pair_specs(tm, D_D, o_f)
        + _pair_specs(tm, D_D, o_b)
        + [tok, tok, row, row, pl.BlockSpec((LANES, LANES), lambda i: (0, 0)),
           pl.BlockSpec((None, 1, 6 * D_MODEL), lambda i: (_cond_row(i, tm), 0, 0)),
           pl.BlockSpec((D_C + D_D, D_MODEL), lambda i: (0, 0))],
        out_specs=pl.BlockSpec((tm, D_MODEL), lambda i: (i, 0)),
        out_shape=jax.ShapeDtypeStruct((N_TOK, D_MODEL), F32),
        compiler_params=_params(("arbitrary",)),
        name="odd_out_proj",
    )(x[0], x[1], a[0], a[1], o_f[0], o_f[1], o_b[0], o_b[1], rkv, g, ln_g, ln_b, gm, mod_l, w)


def _rope_tables():
    pos = np.arange(DEC_SEQ)
    pr, pc = pos // GRID_W, pos % GRID_W
    lane = np.arange(LANES)
    dd = lane % DH_A
    use_col = (dd // 32) == 1
    j = dd % 16
    is_lo = (dd % 32) < 16
    freq = ROPE_THETA ** (-(j.astype(np.float64)) / 16.0)
    p = np.where(use_col[None, :], pc[:, None], pr[:, None]).astype(np.float64)
    ang = (p.astype(np.float32) * freq.astype(np.float32)[None, :]).astype(np.float32)
    cos = np.cos(ang).astype(np.float32)
    sin = np.sin(ang).astype(np.float32)
    s1 = np.where(is_lo[None, :], -sin, 0.0).astype(np.float32)
    s2 = np.where(is_lo[None, :], 0.0, sin).astype(np.float32)
    ident = 512
    cos = np.concatenate([cos, np.ones((ident, LANES), np.float32)], 0)
    s1 = np.concatenate([s1, np.zeros((ident, LANES), np.float32)], 0)
    s2 = np.concatenate([s2, np.zeros((ident, LANES), np.float32)], 0)
    return jnp.asarray(cos), jnp.asarray(s1), jnp.asarray(s2)


def _block_diag(value):
    m = np.zeros((LANES, LANES), np.float32)
    half = LANES // 2
    m[:half, :half] = value
    m[half:, half:] = value
    return jnp.asarray(m)


def _head_spread():
    step = np.arange(SUBLANES * LANES) // LANES
    head = (np.arange(SUBLANES * LANES) % LANES) // DH_D
    out_head = np.arange(2 * LANES) // LANES
    out_step = (np.arange(2 * LANES) % LANES) % SUBLANES
    m = (head[:, None] == out_head[None, :]) & (step[:, None] == out_step[None, :])
    return jnp.asarray(m.astype(np.float32))


def _qb_perm():
    idx = np.zeros(D_B, np.int32)
    for j in range(HB // 2):
        for hh in range(2):
            for dch in range(DH_B):
                idx[j * LANES + hh * DH_B + dch] = (hh * (HB // 2) + j) * DH_B + dch
    return idx


def kernel(x_prompt, x_sample, cache_a_k, cache_a_v, cache_b_k, cache_b_v, state_hgrn, state_rwkv, c, c_ctx, ada_w, ada_b, norm_mix_g, norm_ffn_g, final_norm_g, ev_w_in, ev_w_out, a_lambda, a_subln_g, b_q_norm_g, b_k_norm_g, od_w_in, od_w_out, hgrn_lb_logits, hgrn_norm_g, rwkv_mu, rwkv_w0, rwkv_w_up, rwkv_a0, rwkv_a_up, rwkv_g_up, rwkv_k_k, rwkv_k_a, rwkv_r_k, rwkv_ln_g, rwkv_ln_b, ffn_w_up, ffn_conv_w, ffn_conv_b, ffn_w_down):
    d = D_MODEL
    x = (x_prompt.reshape(N_CTX, d), x_sample.reshape(N_DEC, d), 0)
    cond =jnp.concatenate([c_ctx[None, :], c, jnp.zeros((COND_ROWS - 1 - DEC_BATCH, d), F32)], axis=0)
    mod = _mod_call(cond, ada_w, ada_b).reshape(DEPTH, COND_ROWS, 1, 6 * d)

    cos, s1, s2 = _rope_tables()
    g_mean = _block_diag(1.0 / DH_B)
    g_sum = _block_diag(1.0)
    perm = _qb_perm()
    row = lambda t: t.reshape(1, -1)

    w_up_all = ffn_w_up.astype(BF16)
    w_down_all = ffn_w_down.astype(BF16)
    conv_b_all = ffn_conv_b.reshape(DEPTH, 1, 2 * D_FF)
    for l in range(DEPTH):
        i = l // 2
        mod_l = mod[l]
        if l % 2 == 0:
            w_in = ev_w_in[i]
            w_in = jnp.concatenate([w_in[:, :1536], w_in[:, 1536:2048][:, perm], w_in[:, 2048:]], axis=1).astype(BF16)
            w_out = ev_w_out[i]
            w_out = jnp.concatenate([w_out[:D_A], w_out[D_A:][perm]], axis=0).astype(BF16)
            qg = row(jnp.tile(b_q_norm_g[i], 2))
            kg = row(jnp.tile(b_k_norm_g[i], 2))
            qa, ka, va, qb, kb, vb, *even_ctx = _even_proj_call(x, mod_l, row(norm_mix_g[l]), w_in, qg, kg, cos, s1,
                                                                s2, g_mean)
            even_ctx = tuple(even_ctx)
            lam_init = 0.8 - 0.6 * math.exp(-0.3 * l)
            sub_g = row(a_subln_g[i])
            oa_c, ob_c = _attn_call(qa, qb, ka, va, kb, vb, None, a_lambda[i], sub_g, lam_init,
                                    n_seq=BATCH, seq=SEQ, row0=0)
            cbk = jnp.transpose(cache_b_k[:, i], (0, 2, 1, 3)).reshape(DEC_BATCH, PAST_LEN, HKV_B * DH_B)
            cbv = jnp.transpose(cache_b_v[:, i], (0, 2, 1, 3)).reshape(DEC_BATCH, PAST_LEN, HKV_B * DH_B)
            cache = (cache_a_k[:, i], cache_a_v[:, i], cbk, cbv)
            oa_s, ob_s = _attn_call(qa, qb, ka, va, kb, vb, cache, a_lambda[i], sub_g, lam_init,
                                    n_seq=DEC_BATCH, seq=DEC_SEQ, row0=N_CTX)
            x1 = _out_proj_call(x, (oa_c, oa_s, 0), (ob_c, ob_s, 0), mod_l, w_out)
        else:
            w = od_w_in[i]
            w_h = w[:, :HGRN_PROJ].astype(BF16)
            wr = w[:, HGRN_PROJ:]
            mu = rwkv_mu[i]

            def rwkv_cols(t):
                z = jnp.zeros(t.shape[:-1] + (RWKV_PAD - RWKV_PROJ,), t.dtype)
                return jnp.concatenate([t[..., :1664], t[..., 1728:1856], t[..., 1664:1728], z], axis=-1)

            w_r = rwkv_cols(wr).astype(BF16)
            mu_r = row(rwkv_cols(mu))
            g_l = row(norm_mix_g[l])
            assert x[0] is x[1]
            proj_h = _odd_proj_call(x[0], mod_l, g_l, w_h, jnp.zeros((1, HGRN_PROJ), F32), shift=False)
            proj_r = _odd_proj_call(x[0], mod_l, g_l, w_r, mu_r, shift=True)

            ng = row(hgrn_norm_g[i])
            zero_h = jnp.zeros((BATCH, 2, HC, DK_C, DV_C), F32)
            ones = jnp.ones((LANES, LANES), BF16)
            oc_c, sh_c = _hgrn_call(proj_h, hgrn_lb_logits, ng, zero_h, ones, n_seq=BATCH, seq=SEQ, row0=0, lidx=l)
            oc_s, _ = _hgrn_call(proj_h, hgrn_lb_logits, ng, state_hgrn[:, i], ones, n_seq=DEC_BATCH, seq=DEC_SEQ,
                                 row0=N_CTX, lidx=l)

            pad_rows = lambda t: jnp.concatenate([t, jnp.zeros((LANES - t.shape[0], t.shape[1]), F32)], axis=0)
            wup0 = pad_rows(rwkv_w_up[i, 0])
            wup1 = jnp.concatenate([jnp.zeros((W_LORA, D_D), F32), rwkv_w_up[i, 1]], axis=0)
            w0o, w1o, nkk, bb, kt, gg, rkv = _rwkv_prep_call(
                proj_r, row(rwkv_a0[i]), pad_rows(rwkv_a_up[i]), rwkv_g_up[i], row(rwkv_k_k[i]), row(rwkv_k_a[i]),
                row(rwkv_r_k[i]), rwkv_w0[i], wup0, wup1, g_sum)

            def to_tiles(s):
                b = s.shape[0]
                s = s.reshape(b, 2, HD // 2, 2, DH_D, DH_D)
                return jnp.transpose(s, (0, 1, 2, 4, 3, 5)).reshape(b, 2, HD // 2, DH_D, LANES)

            def from_tiles(s):
                b = s.shape[0]
                s = s.reshape(b, 2, HD // 2, DH_D, 2, DH_D)
                return jnp.transpose(s, (0, 1, 2, 4, 3, 5)).reshape(b, 2, HD, DH_D, DH_D)

            zero_r = jnp.zeros((BATCH, 2, HD // 2, DH_D, LANES), F32)
            q1 = g_sum.astype(BF16)
            wsp = _head_spread().astype(BF16)
            of_c, ob_c, sr_c = _rwkv_scan_call(proj_r, w0o, w1o, nkk, bb, kt, zero_r, q1, wsp,
                                               n_seq=BATCH, seq=SEQ, row0=0)
            of_s, ob_s, _ = _rwkv_scan_call(proj_r, w0o, w1o, nkk, bb, kt, to_tiles(state_rwkv[:, i]), q1, wsp,
                                            n_seq=DEC_BATCH, seq=DEC_SEQ, row0=N_CTX)
            x1 = _odd_out_call(x, (oc_c, oc_s, 0), (of_c, of_s, 0), (ob_c, ob_s, 0), rkv, gg, row(rwkv_ln_g[i]),
                               row(rwkv_ln_b[i]), g_mean, mod_l, od_w_out[i].astype(BF16))
            odd_ctx = (sh_c[:, None], from_tiles(sr_c)[:, None])

        x2 = _ffn_call(x1, mod_l, row(norm_ffn_g[l]), w_up_all, ffn_conv_w, conv_b_all, w_down_all,
                       row(final_norm_g), layer=l, final=(l == DEPTH - 1))
        if l < DEPTH - 1:
            x = _whole(x2)

    y_prompt = x2[0].reshape(BATCH, SEQ, d)
    y_sample = x2[1].reshape(DEC_BATCH, DEC_SEQ, d)
    return (y_prompt, y_sample) + even_ctx + odd_ctx
```
